```python
import math
import jax
import jax.numpy as jnp
from jax import lax
import numpy as np

D_MODEL = 2048
BATCH = 2
SEQ = 4096
DEPTH = 1
DEC_BATCH = 128
DEC_SEQ = 8
PAST_LEN = 8192
PAGE_SIZE = 128

GLA_HEADS = 4
GLA_DK = 128
GLA_DV = 256
GLA_GATE_RANK = 16
GLA_TAU = 16.0
GLA_CHUNK = 64
GLA_QK = GLA_HEADS * GLA_DK
GLA_WIDTH = GLA_HEADS * GLA_DV
MLA_HEADS = 8
MLA_Q_RANK = 512
MLA_KV_RANK = 512
MLA_NOPE = 128
MLA_ROPE = 64
MLA_DV = 128
MLA_WIDTH = MLA_HEADS * MLA_DV
ROPE_BASE = 10000.0
Q_BLOCK = 128
ATTN_SCALE = (MLA_NOPE + MLA_ROPE) ** -0.5
MIX_WIDTH = GLA_WIDTH + MLA_WIDTH
OFF_GK = GLA_QK
OFF_GV = 2 * GLA_QK
OFF_GA = OFF_GV + GLA_WIDTH
OFF_GR = OFF_GA + GLA_GATE_RANK
OFF_CQ = OFF_GR + GLA_WIDTH
OFF_CKV = OFF_CQ + MLA_Q_RANK
OFF_KR = OFF_CKV + MLA_KV_RANK
IN_COLS = OFF_KR + MLA_ROPE
N_GROUPS = 8
EXPERTS_PER_GROUP = 8
N_EXPERTS = N_GROUPS * EXPERTS_PER_GROUP
TOP_K = 2
EXPERT_FF = 1408
MOE_BLOCK = 128
EPS = 1e-6

kernel_name = 'hymba_gla_mla_hmoe_adaln_step'


def rms_norm(x, g):
    xf = x.astype(jnp.float32)
    y = xf * lax.rsqrt(jnp.mean(xf * xf, axis=-1, keepdims=True) + EPS)
    return (y * g.astype(jnp.float32)).astype(x.dtype)


def apply_rope(x, pos):
    half = MLA_ROPE // 2
    inv_freq = ROPE_BASE ** (-jnp.arange(half, dtype=jnp.float32) / half)
    ang = pos.astype(jnp.float32)[:, None] * inv_freq[None, :]
    cos = jnp.cos(ang)[None, :, None, :]
    sin = jnp.sin(ang)[None, :, None, :]
    xf = x.astype(jnp.float32)
    x1, x2 = xf[..., :half], xf[..., half:]
    return jnp.concatenate([x1 * cos - x2 * sin, x1 * sin + x2 * cos], axis=-1).astype(x.dtype)


def gla_chunked(q, k, v, log_a, s0):
    bsz, seq, n_h, _ = q.shape
    d_v = v.shape[-1]
    csz = math.gcd(seq, GLA_CHUNK)
    n_c = seq // csz

    def chunks(t):
        return t.astype(jnp.float32).reshape(bsz, n_c, csz, n_h, t.shape[-1]).transpose(1, 0, 3, 2, 4)

    qc = chunks(q) * (GLA_DK ** -0.5)
    kc, vc, gc = chunks(k), chunks(v), chunks(log_a)
    causal = jnp.tril(jnp.ones((csz, csz), dtype=bool))

    def step(s, inp):
        qi, ki, vi, gi = inp
        b = jnp.cumsum(gi, axis=2)
        b_last = b[:, :, -1:, :]
        q_dec = qi * jnp.exp(b)
        k_inv = ki * jnp.exp(-b)
        att = jnp.where(causal, jnp.einsum('bhtd,bhsd->bhts', q_dec, k_inv), 0.0)
        o = jnp.einsum('bhtd,bhdv->bhtv', q_dec, s) + jnp.einsum('bhts,bhsv->bhtv', att, vi)
        s_new = (jnp.exp(b_last[:, :, 0, :])[..., None] * s
                 + jnp.einsum('bhsd,bhsv->bhdv', ki * jnp.exp(b_last - b), vi))
        return s_new, o

    s_fin, oc = lax.scan(step, s0.astype(jnp.float32), (qc, kc, vc, gc))
    o = oc.transpose(1, 0, 3, 2, 4).reshape(bsz, seq, n_h, d_v)
    return o, s_fin


def gla_branch(gq, gk, gv, ga, gr, s0, w_gate, b_gate, g_onorm):
    bsz, seq, _ = gq.shape
    q = gq.reshape(bsz, seq, GLA_HEADS, GLA_DK)
    k = gk.reshape(bsz, seq, GLA_HEADS, GLA_DK)
    v = gv.reshape(bsz, seq, GLA_HEADS, GLA_DV)
    gate_logit = (ga @ w_gate + b_gate).astype(jnp.float32)
    log_a = (jax.nn.log_sigmoid(gate_logit) / GLA_TAU).reshape(bsz, seq, GLA_HEADS, GLA_DK)
    o, s_fin = gla_chunked(q, k, v, log_a, s0)
    o = rms_norm(o, g_onorm).reshape(bsz, seq, GLA_WIDTH) * jax.nn.silu(gr.astype(jnp.float32))
    return o.astype(gq.dtype), s_fin


def mla_project(cq, ckv, kr, pos, g_qnorm, w_uq, g_kvnorm):
    q = jnp.einsum('blr,rhd->blhd', rms_norm(cq, g_qnorm), w_uq)
    q_nope = q[..., :MLA_NOPE]
    q_rope = apply_rope(q[..., MLA_NOPE:], pos)
    ckv_n = rms_norm(ckv, g_kvnorm)
    k_rope = apply_rope(kr[:, :, None, :], pos)[:, :, 0, :]
    return q_nope, q_rope, ckv_n, k_rope


def mla_prompt_attention(q_nope, q_rope, ckv_n, k_rope, w_uk, w_uv):
    bsz, seq = q_nope.shape[:2]
    k_nope = jnp.einsum('bkr,rhn->bhkn', ckv_n, w_uk)
    v = jnp.einsum('bkr,rhv->bhkv', ckv_n, w_uv)
    qb = math.gcd(seq, Q_BLOCK)
    n_b = seq // qb
    qn = q_nope.reshape(bsz, n_b, qb, MLA_HEADS, MLA_NOPE).transpose(1, 0, 3, 2, 4)
    qr = q_rope.reshape(bsz, n_b, qb, MLA_HEADS, MLA_ROPE).transpose(1, 0, 3, 2, 4)
    k_pos = jnp.arange(seq)

    def block(args):
        i, qn_i, qr_i = args
        s = jnp.einsum('bhqn,bhkn->bhqk', qn_i, k_nope) + jnp.einsum('bhqp,bkp->bhqk', qr_i, k_rope)
        q_pos = i * qb + jnp.arange(qb)
        s = jnp.where(k_pos[None, :] <= q_pos[:, None], s.astype(jnp.float32) * ATTN_SCALE, -jnp.inf)
        p = jax.nn.softmax(s, axis=-1).astype(v.dtype)
        return jnp.einsum('bhqk,bhkv->bqhv', p, v)

    o = lax.map(block, (jnp.arange(n_b), qn, qr))
    return o.transpose(1, 0, 2, 3, 4).reshape(bsz, seq, MLA_HEADS, MLA_DV)


def mla_sample_attention(q_nope, q_rope, ckv_n, k_rope, past_ckv, past_kr, w_uk, w_uv):
    n_q = q_nope.shape[1]
    n_past = past_ckv.shape[1]
    q_lat = jnp.einsum('bqhn,rhn->bhqr', q_nope, w_uk)
    q_r = q_rope.transpose(0, 2, 1, 3)
    s_past = jnp.einsum('bhqr,bkr->bhqk', q_lat, past_ckv) + jnp.einsum('bhqp,bkp->bhqk', q_r, past_kr)
    s_new = jnp.einsum('bhqr,bkr->bhqk', q_lat, ckv_n) + jnp.einsum('bhqp,bkp->bhqk', q_r, k_rope)
    causal = jnp.tril(jnp.ones((n_q, n_q), dtype=bool))
    s = jnp.concatenate([s_past.astype(jnp.float32),
                         jnp.where(causal, s_new.astype(jnp.float32), -jnp.inf)], axis=-1) * ATTN_SCALE
    p = jax.nn.softmax(s, axis=-1).astype(past_ckv.dtype)
    o_lat = (jnp.einsum('bhqk,bkr->bqhr', p[..., :n_past], past_ckv)
             + jnp.einsum('bhqk,bkr->bqhr', p[..., n_past:], ckv_n))
    return jnp.einsum('bqhr,rhv->bqhv', o_lat, w_uv)


def token_mixer(h, pos, s0, past_ckv, past_kr, w_in, gla_w_gate, gla_b_gate, gla_g_onorm,
                mla_g_qnorm, mla_w_uq, mla_g_kvnorm, mla_w_uk, mla_w_uv, mla_g_onorm, w_out):
    bsz, seq, _ = h.shape
    proj = h @ w_in
    gq, gk, gv, ga, gr, cq, ckv, kr = jnp.split(
        proj, [OFF_GK, OFF_GV, OFF_GA, OFF_GR, OFF_CQ, OFF_CKV, OFF_KR], axis=-1)
    o_gla, s_fin = gla_branch(gq, gk, gv, ga, gr, s0, gla_w_gate, gla_b_gate, gla_g_onorm)
    q_nope, q_rope, ckv_n, k_rope = mla_project(cq, ckv, kr, pos, mla_g_qnorm, mla_w_uq, mla_g_kvnorm)
    if past_ckv is None:
        o_mla = mla_prompt_attention(q_nope, q_rope, ckv_n, k_rope, mla_w_uk, mla_w_uv)
    else:
        o_mla = mla_sample_attention(q_nope, q_rope, ckv_n, k_rope, past_ckv, past_kr, mla_w_uk, mla_w_uv)
    o_mla = rms_norm(o_mla, mla_g_onorm).reshape(bsz, seq, MLA_WIDTH)
    y = jnp.concatenate([o_gla, o_mla.astype(o_gla.dtype)], axis=-1) @ w_out
    return y, ckv_n, k_rope, s_fin


def routed_experts(x, e_id, e_w, w_gu, w_down):
    n_tok, d = x.shape
    n_as = e_id.shape[0]
    tok = jnp.arange(n_as, dtype=jnp.int32) // TOP_K
    order = jnp.argsort(e_id)
    e_s, tok_s, w_s = e_id[order], tok[order], e_w[order]
    cnt = jnp.zeros((N_EXPERTS,), jnp.int32).at[e_id].add(1)
    pcnt = (cnt + MOE_BLOCK - 1) // MOE_BLOCK * MOE_BLOCK
    pend = jnp.cumsum(pcnt)
    pstart = pend - pcnt
    ustart = jnp.cumsum(cnt) - cnt
    slot = pstart[e_s] + jnp.arange(n_as, dtype=jnp.int32) - ustart[e_s]
    n_blk = -(-n_as // MOE_BLOCK) + N_EXPERTS
    slot_tok = jnp.full((n_blk * MOE_BLOCK,), n_tok, jnp.int32).at[slot].set(tok_s)
    blk_e = jnp.minimum(jnp.searchsorted(pend, jnp.arange(n_blk, dtype=jnp.int32) * MOE_BLOCK, side='right'),
                        N_EXPERTS - 1).astype(jnp.int32)
    xp = jnp.concatenate([x, jnp.zeros((1, d), x.dtype)], axis=0)
    xb = xp[slot_tok].reshape(n_blk, MOE_BLOCK, d)

    def expert_block(args):
        xi, ei = args
        gu = xi @ w_gu[ei]
        return (jax.nn.silu(gu[:, :EXPERT_FF]) * gu[:, EXPERT_FF:]) @ w_down[ei]

    yb = lax.map(expert_block, (xb, blk_e)).reshape(n_blk * MOE_BLOCK, d)
    y_as = yb[slot] * w_s[:, None].astype(yb.dtype)
    return jnp.zeros((n_tok, d), yb.dtype).at[tok_s].add(y_as)


def hier_moe(h, w_group, b_group, w_expert, b_expert, w_gu, w_down):
    bsz, seq, d = h.shape
    x = h.reshape(bsz * seq, d)
    n_tok = x.shape[0]
    rows = jnp.arange(n_tok)
    g_logit = (x @ w_group + b_group).astype(jnp.float32)
    g_prob = jax.nn.softmax(g_logit, axis=-1)
    g_idx = jnp.argmax(g_logit, axis=-1).astype(jnp.int32)
    g_w = g_prob[rows, g_idx]
    e_logit = (x @ w_expert + b_expert).astype(jnp.float32).reshape(n_tok, N_GROUPS, EXPERTS_PER_GROUP)
    e_in_group = e_logit[rows, g_idx]
    top_v, top_i = lax.top_k(e_in_group, TOP_K)
    e_w = jax.nn.softmax(top_v, axis=-1) * g_w[:, None]
    e_id = g_idx[:, None] * EXPERTS_PER_GROUP + top_i.astype(jnp.int32)
    y = routed_experts(x, e_id.reshape(-1), e_w.reshape(-1), w_gu, w_down)
    return y.reshape(bsz, seq, d)


def run_trunk(x, c, pos, state_gla, cache_ckv, cache_krope, page_table,
              w_ada, b_ada, g_mix, g_ffn, w_in, gla_w_gate, gla_b_gate, gla_g_onorm,
              mla_g_qnorm, mla_w_uq, mla_g_kvnorm, mla_w_uk, mla_w_uv, mla_g_onorm, w_out,
              moe_w_group, moe_b_group, moe_w_expert, moe_b_expert, moe_w_gu, moe_w_down, g_final):
    bsz = x.shape[0]
    ckv_rows, kr_rows, states = [], [], []
    for l in range(DEPTH):
        mod = (jax.nn.silu(c) @ w_ada[l] + b_ada[l]).astype(x.dtype)[:, None, :]
        sh1, sc1, gt1, sh2, sc2, gt2 = jnp.split(mod, 6, axis=-1)
        if state_gla is None:
            s0 = jnp.zeros((bsz, GLA_HEADS, GLA_DK, GLA_DV), jnp.float32)
            past_ckv = None
            past_kr = None
        else:
            s0 = state_gla[l]
            past_ckv = cache_ckv[l][page_table].reshape(bsz, -1, MLA_KV_RANK)
            past_kr = cache_krope[l][page_table].reshape(bsz, -1, MLA_ROPE)
        h = rms_norm(x, g_mix[l]) * (1.0 + sc1) + sh1
        y, ckv_n, k_rope, s_fin = token_mixer(
            h, pos, s0, past_ckv, past_kr, w_in[l], gla_w_gate[l], gla_b_gate[l], gla_g_onorm[l],
            mla_g_qnorm[l], mla_w_uq[l], mla_g_kvnorm[l], mla_w_uk[l], mla_w_uv[l], mla_g_onorm[l], w_out[l])
        x = x + gt1 * y
        h = rms_norm(x, g_ffn[l]) * (1.0 + sc2) + sh2
        x = x + gt2 * hier_moe(h, moe_w_group[l], moe_b_group[l], moe_w_expert[l], moe_b_expert[l],
                               moe_w_gu[l], moe_w_down[l])
        ckv_rows.append(ckv_n)
        kr_rows.append(k_rope)
        states.append(s_fin)
    return rms_norm(x, g_final), jnp.stack(ckv_rows), jnp.stack(kr_rows), jnp.stack(states)


def setup_inputs(seed: int = 0) -> dict:
    key = jax.random.key(seed)
    ks = jax.random.split(key, 32)
    f32 = jnp.float32

    def nrm(k, shape, scale):
        return jax.random.normal(k, shape, f32) * scale

    def gain(k, shape):
        return 1.0 + 0.02 * jax.random.normal(k, shape, f32)

    n_pages = PAST_LEN // PAGE_SIZE
    n_used = DEC_BATCH * n_pages
    n_pool = n_used + n_used // 4
    page_table = jax.random.permutation(ks[7], n_pool)[:n_used].reshape(DEC_BATCH, n_pages).astype(jnp.int32)
    return {
        'x_prompt': nrm(ks[0], (BATCH, SEQ, D_MODEL), 1.0),
        'x_sample': nrm(ks[1], (DEC_BATCH, DEC_SEQ, D_MODEL), 1.0),
        'c_prompt': nrm(ks[2], (BATCH, D_MODEL), 1.0),
        'c_sample': nrm(ks[3], (DEC_BATCH, D_MODEL), 1.0),
        'cache_ckv': nrm(ks[4], (DEPTH, n_pool, PAGE_SIZE, MLA_KV_RANK), 1.0),
        'cache_krope': nrm(ks[5], (DEPTH, n_pool, PAGE_SIZE, MLA_ROPE), 1.0),
        'state_gla': nrm(ks[6], (DEPTH, DEC_BATCH, GLA_HEADS, GLA_DK, GLA_DV), 2.0),
        'page_table': page_table,
        'w_ada': nrm(ks[8], (DEPTH, D_MODEL, 6 * D_MODEL), D_MODEL ** -0.5),
        'b_ada': nrm(ks[9], (DEPTH, 6 * D_MODEL), 0.02),
        'g_mix': gain(ks[10], (DEPTH, D_MODEL)),
        'g_ffn': gain(ks[11], (DEPTH, D_MODEL)),
        'w_in': nrm(ks[12], (DEPTH, D_MODEL, IN_COLS), D_MODEL ** -0.5),
        'gla_w_gate': nrm(ks[13], (DEPTH, GLA_GATE_RANK, GLA_QK), GLA_GATE_RANK ** -0.5),
        'gla_b_gate': nrm(ks[14], (DEPTH, GLA_QK), 0.5),
        'gla_g_onorm': gain(ks[15], (DEPTH, GLA_DV)),
        'mla_g_qnorm': gain(ks[16], (DEPTH, MLA_Q_RANK)),
        'mla_w_uq': nrm(ks[17], (DEPTH, MLA_Q_RANK, MLA_HEADS, MLA_NOPE + MLA_ROPE), MLA_Q_RANK ** -0.5),
        'mla_g_kvnorm': gain(ks[18], (DEPTH, MLA_KV_RANK)),
        'mla_w_uk': nrm(ks[19], (DEPTH, MLA_KV_RANK, MLA_HEADS, MLA_NOPE), MLA_KV_RANK ** -0.5),
        'mla_w_uv': nrm(ks[20], (DEPTH, MLA_KV_RANK, MLA_HEADS, MLA_DV), MLA_KV_RANK ** -0.5),
        'mla_g_onorm': gain(ks[21], (DEPTH, MLA_DV)),
        'w_out': nrm(ks[22], (DEPTH, MIX_WIDTH, D_MODEL), MIX_WIDTH ** -0.5),
        'moe_w_group': nrm(ks[23], (DEPTH, D_MODEL, N_GROUPS), D_MODEL ** -0.5),
        'moe_b_group': nrm(ks[24], (DEPTH, N_GROUPS), 0.01),
        'moe_w_expert': nrm(ks[25], (DEPTH, D_MODEL, N_EXPERTS), D_MODEL ** -0.5),
        'moe_b_expert': nrm(ks[26], (DEPTH, N_EXPERTS), 0.01),
        'moe_w_gu': nrm(ks[27], (DEPTH, N_EXPERTS, D_MODEL, 2 * EXPERT_FF), D_MODEL ** -0.5),
        'moe_w_down': nrm(ks[28], (DEPTH, N_EXPERTS, EXPERT_FF, D_MODEL), EXPERT_FF ** -0.5),
        'g_final': gain(ks[29], (D_MODEL,)),
    }


def reference(x_prompt, x_sample, c_prompt, c_sample, cache_ckv, cache_krope, state_gla, page_table,
              w_ada, b_ada, g_mix, g_ffn, w_in, gla_w_gate, gla_b_gate, gla_g_onorm,
              mla_g_qnorm, mla_w_uq, mla_g_kvnorm, mla_w_uk, mla_w_uv, mla_g_onorm, w_out,
              moe_w_group, moe_b_group, moe_w_expert, moe_b_expert, moe_w_gu, moe_w_down, g_final):
    weights = (w_ada, b_ada, g_mix, g_ffn, w_in, gla_w_gate, gla_b_gate, gla_g_onorm,
               mla_g_qnorm, mla_w_uq, mla_g_kvnorm, mla_w_uk, mla_w_uv, mla_g_onorm, w_out,
               moe_w_group, moe_b_group, moe_w_expert, moe_b_expert, moe_w_gu, moe_w_down, g_final)
    n_past = page_table.shape[1] * PAGE_SIZE
    pos_p = jnp.arange(x_prompt.shape[1], dtype=jnp.int32)
    pos_s = n_past + jnp.arange(x_sample.shape[1], dtype=jnp.int32)
    y_prompt, ckv_p, kr_p, st_p = run_trunk(x_prompt, c_prompt, pos_p, None, None, None, None, *weights)
    y_sample, ckv_s, kr_s, st_s = run_trunk(x_sample, c_sample, pos_s, state_gla, cache_ckv, cache_krope,
                                            page_table, *weights)
    return (y_prompt, y_sample, ckv_p, kr_p, st_p, ckv_s, kr_s, st_s)
```

```python
import functools
import math

import jax
import jax.numpy as jnp
from jax import lax
from jax.experimental import pallas as pl
from jax.experimental.pallas import tpu as pltpu

F32 = jnp.float32
BF16 = jnp.bfloat16
HIGHEST = lax.Precision.HIGHEST

D_MODEL = 2048
PAGE_SIZE = 128
GLA_HEADS = 4
GLA_DK = 128
GLA_DV = 256
GLA_GATE_RANK = 16
GLA_TAU = 16.0
GLA_CHUNK = 64
GLA_QK = GLA_HEADS * GLA_DK
GLA_WIDTH = GLA_HEADS * GLA_DV
MLA_HEADS = 8
MLA_Q_RANK = 512
MLA_KV_RANK = 512
MLA_NOPE = 128
MLA_ROPE = 64
MLA_DV = 128
MLA_WIDTH = MLA_HEADS * MLA_DV
ROPE_BASE = 10000.0
ATTN_SCALE = (MLA_NOPE + MLA_ROPE) ** -0.5
N_GROUPS = 8
EXPERTS_PER_GROUP = 8
N_EXPERTS = N_GROUPS * EXPERTS_PER_GROUP
TOP_K = 2
EXPERT_FF = 1408
EPS = 1e-6

LANES = 128
QK_PAD = 2 * LANES
COL_GQ = 0
COL_GK = GLA_QK
COL_GV = 2 * GLA_QK
COL_GR = COL_GV + GLA_WIDTH
COL_CQ = COL_GR + GLA_WIDTH
COL_CKV = COL_CQ + MLA_Q_RANK
COL_KR = COL_CKV + MLA_KV_RANK
IN_COLS_PAD = COL_KR + LANES
GA_LANE = MLA_ROPE
IN_COL_TILE = IN_COLS_PAD // 3

MOE_ROWS = 512
MOE_KSTEPS = 4
MOE_KCHUNK = D_MODEL // MOE_KSTEPS
VMEM_LIMIT = 56 * 1024 * 1024


def _params(sem, vmem=VMEM_LIMIT):
    return pltpu.CompilerParams(dimension_semantics=sem, vmem_limit_bytes=vmem)


def _silu(x):
    return x * jax.nn.sigmoid(x)


def _rms(x, g):
    return x * lax.rsqrt(jnp.mean(x * x, axis=-1, keepdims=True) + EPS) * g


def _ada_kernel(c_ref, w_ref, b_ref, o_ref):
    a = _silu(c_ref[...]).astype(BF16)
    o_ref[...] = jnp.dot(a, w_ref[...].astype(BF16), preferred_element_type=F32) + b_ref[...]


def _ada(c, w, b):
    rows, d = c.shape
    n = w.shape[1]
    tn = 1024
    return pl.pallas_call(
        _ada_kernel,
        grid=(n // tn,),
        in_specs=[pl.BlockSpec((rows, d), lambda j: (0, 0)),
                  pl.BlockSpec((d, tn), lambda j: (0, j)),
                  pl.BlockSpec((1, tn), lambda j: (0, j))],
        out_specs=pl.BlockSpec((rows, tn), lambda j: (0, j)),
        out_shape=jax.ShapeDtypeStruct((rows, n), F32),
        compiler_params=_params(("arbitrary",)),
        name="ada_mod",
    )(c, w, b)


def _inproj_kernel(x_ref, g_ref, sc_ref, sh_ref, w_ref, o_ref, h_scr):
    @pl.when(pl.program_id(1) == 0)
    def _():
        h = _rms(x_ref[...], g_ref[...][None]) * (1.0 + sc_ref[...]) + sh_ref[...]
        h_scr[...] = h.reshape(h_scr.shape).astype(BF16)

    o_ref[...] = jnp.dot(h_scr[...], w_ref[...], preferred_element_type=F32)


def _inproj(x, g, sc, sh, w, bb, lt):
    bsz, seq, d = x.shape
    rows = bb * lt
    nl = seq // lt
    n = w.shape[1]
    return pl.pallas_call(
        _inproj_kernel,
        grid=(bsz * seq // rows, n // IN_COL_TILE),
        in_specs=[pl.BlockSpec((bb, lt, d), lambda i, j: (i // nl, i % nl, 0)),
                  pl.BlockSpec((1, d), lambda i, j: (0, 0)),
                  pl.BlockSpec((bb, 1, d), lambda i, j: (i // nl, 0, 0)),
                  pl.BlockSpec((bb, 1, d), lambda i, j: (i // nl, 0, 0)),
                  pl.BlockSpec((d, IN_COL_TILE), lambda i, j: (0, j))],
        out_specs=pl.BlockSpec((rows, IN_COL_TILE), lambda i, j: (i, j)),
        out_shape=jax.ShapeDtypeStruct((bsz * seq, n), F32),
        scratch_shapes=[pltpu.VMEM((rows, d), BF16)],
        compiler_params=_params(("arbitrary", "arbitrary")),
        name="in_proj",
    )(x, g, sc, sh, w)


def _gla_kernel(q_ref, k_ref, v_ref, gr_ref, ga_ref, wg_ref, bg_ref, gon_ref, s0_ref,
                o_ref, sout_ref, s_scr, *, n_chunks, n_valid):
    c = GLA_CHUNK
    l_step = pl.program_id(2)

    @pl.when(l_step == 0)
    def _():
        s_scr[...] = s0_ref[...]

    row = lax.broadcasted_iota(jnp.int32, (c, c), 0)
    col = lax.broadcasted_iota(jnp.int32, (c, c), 1)
    causal = col <= row
    tri = causal.astype(F32)
    chunks_per_batch = n_chunks // s_scr.shape[0]

    def load(ref, r0):
        x = ref[pl.ds(r0, n_valid), :]
        if n_valid < c:
            x = jnp.concatenate([x, jnp.zeros((c - n_valid, x.shape[1]), x.dtype)], axis=0)
        return x

    def chunk(idx, carry):
        r0 = pl.multiple_of(idx * n_valid, 8)
        bi = idx // chunks_per_batch
        q = load(q_ref, r0) * (GLA_DK ** -0.5)
        k = load(k_ref, r0)
        v = load(v_ref, r0).astype(BF16)
        z = jnp.dot(load(ga_ref, r0), wg_ref[...], precision=HIGHEST, preferred_element_type=F32) + bg_ref[...]
        log_a = (jnp.minimum(z, 0.0) - jnp.log1p(jnp.exp(-jnp.abs(z)))) / GLA_TAU
        if n_valid < c:
            log_a = jnp.where(lax.broadcasted_iota(jnp.int32, log_a.shape, 0) < n_valid, log_a, 0.0)
        b = jnp.dot(tri, log_a, precision=HIGHEST, preferred_element_type=F32)
        b_last = b[c - 1:c, :]
        q_dec = (q * jnp.exp(b)).astype(BF16)
        k_inv = (k * jnp.exp(-b)).astype(BF16)
        att = lax.dot_general(q_dec, k_inv, (((1,), (1,)), ((), ())), preferred_element_type=F32)
        att = jnp.where(causal, att, 0.0).astype(BF16)
        s_old = s_scr[bi]
        o = (jnp.dot(q_dec, s_old.astype(BF16), preferred_element_type=F32)
             + jnp.dot(att, v, preferred_element_type=F32))
        k_dec = (k * jnp.exp(b_last - b)).astype(BF16)
        kv = lax.dot_general(k_dec, v, (((0,), (0,)), ((), ())), preferred_element_type=F32)
        decay = jnp.exp(jnp.broadcast_to(b_last, (GLA_DK, GLA_DK)).T)
        s_scr[bi] = jnp.concatenate([decay] * (GLA_DV // GLA_DK), axis=1) * s_old + kv
        o = _rms(o[:n_valid], gon_ref[...])
        o_ref[pl.ds(r0, n_valid), :] = o * _silu(gr_ref[pl.ds(r0, n_valid), :])
        return carry

    lax.fori_loop(0, n_chunks, chunk, 0)

    @pl.when(l_step == pl.num_programs(2) - 1)
    def _():
        sout_ref[...] = s_scr[...]


def _gla(proj, s0, wg_pad, bg, gon, bsz, seq, bb, lt):
    n_valid = math.gcd(seq, GLA_CHUNK)
    rows = bb * lt
    nl = seq // lt
    rmap = lambda b, h, l: b * nl + l
    kern = functools.partial(_gla_kernel, n_chunks=rows // n_valid, n_valid=n_valid)
    st_spec = pl.BlockSpec((bb, None, GLA_DK, GLA_DV), lambda b, h, l: (b, h, 0, 0))
    return pl.pallas_call(
        kern,
        grid=(bsz // bb, GLA_HEADS, nl),
        in_specs=[pl.BlockSpec((rows, GLA_DK), lambda b, h, l: (rmap(b, h, l), COL_GQ // GLA_DK + h)),
                  pl.BlockSpec((rows, GLA_DK), lambda b, h, l: (rmap(b, h, l), COL_GK // GLA_DK + h)),
                  pl.BlockSpec((rows, GLA_DV), lambda b, h, l: (rmap(b, h, l), COL_GV // GLA_DV + h)),
                  pl.BlockSpec((rows, GLA_DV), lambda b, h, l: (rmap(b, h, l), COL_GR // GLA_DV + h)),
                  pl.BlockSpec((rows, LANES), lambda b, h, l: (rmap(b, h, l), COL_KR // LANES)),
                  pl.BlockSpec((LANES, GLA_DK), lambda b, h, l: (0, h)),
                  pl.BlockSpec((1, GLA_DK), lambda b, h, l: (0, h)),
                  pl.BlockSpec((1, GLA_DV), lambda b, h, l: (0, 0)),
                  st_spec],
        out_specs=[pl.BlockSpec((rows, GLA_DV), lambda b, h, l: (rmap(b, h, l), h)), st_spec],
        out_shape=[jax.ShapeDtypeStruct((bsz * seq, GLA_WIDTH), F32),
                   jax.ShapeDtypeStruct((bsz, GLA_HEADS, GLA_DK, GLA_DV), F32)],
        scratch_shapes=[pltpu.VMEM((bb, GLA_DK, GLA_DV), F32)],
        compiler_params=_params(("arbitrary", "arbitrary", "arbitrary")),
        name="gla",
    )(proj, proj, proj, proj, proj, wg_pad, bg, gon, s0)


def _rope(r, cos, sin_lo, sin_hi):
    return (r * cos + pltpu.roll(r, LANES - MLA_ROPE // 2, axis=1) * sin_lo
            + pltpu.roll(r, MLA_ROPE // 2, axis=1) * sin_hi)


def _mla_proj_kernel(cq_ref, ckv_ref, kr_ref, cos_ref, slo_ref, shi_ref, gq_ref, gkv_ref, wq_ref,
                     wk_ref, wv_ref, *out_refs, absorbed):
    cos, slo, shi = cos_ref[...], slo_ref[...], shi_ref[...]
    cqn = _rms(cq_ref[...], gq_ref[...]).astype(BF16)
    qf = jnp.dot(cqn, wq_ref[...], preferred_element_type=F32)
    ckv_n = _rms(ckv_ref[...], gkv_ref[...])
    k_rot = _rope(kr_ref[...], cos, slo, shi)
    if absorbed:
        qlat_ref, qr_ref, ckvn_ref, krope_ref = out_refs
    else:
        qcat_ref, kcat_ref, v_ref, ckvn_ref, krope_ref = out_refs
    ckvn_ref[...] = ckv_n
    krope_ref[...] = k_rot[:, :MLA_ROPE]
    for h in range(MLA_HEADS):
        q_nope = qf[:, h * QK_PAD:h * QK_PAD + LANES] * ATTN_SCALE
        q_rot = _rope(qf[:, h * QK_PAD + LANES:(h + 1) * QK_PAD], cos, slo, shi) * ATTN_SCALE
        if absorbed:
            qlat_ref[h] = jnp.dot(q_nope.astype(BF16), wk_ref[h], preferred_element_type=F32)
            qr_ref[h] = q_rot
        else:
            qcat_ref[:, h * QK_PAD:h * QK_PAD + LANES] = q_nope.astype(BF16)
            qcat_ref[:, h * QK_PAD + LANES:(h + 1) * QK_PAD] = q_rot.astype(BF16)
    if not absorbed:
        ckv_b = ckv_n.astype(BF16)
        k_nope = jnp.dot(ckv_b, wk_ref[...], preferred_element_type=F32)
        k_rot_b = k_rot.astype(BF16)
        for h in range(MLA_HEADS):
            kcat_ref[:, h * QK_PAD:h * QK_PAD + LANES] = k_nope[:, h * LANES:(h + 1) * LANES].astype(BF16)
            kcat_ref[:, h * QK_PAD + LANES:(h + 1) * QK_PAD] = k_rot_b
        v_ref[...] = jnp.dot(ckv_b, wv_ref[...], preferred_element_type=F32).astype(BF16)


def _mla_proj(proj, tabs, gq, gkv, wq, wk, wv, tm, absorbed):
    t = proj.shape[0]
    ntab = tabs[0].shape[0] // tm
    row = lambda c: pl.BlockSpec((tm, c), lambda i: (i, 0))
    full = lambda a: pl.BlockSpec(a.shape, lambda i: (0,) * a.ndim)
    tab = pl.BlockSpec((tm, LANES), lambda i: (i % ntab, 0))
    in_specs = [pl.BlockSpec((tm, MLA_Q_RANK), lambda i: (i, COL_CQ // MLA_Q_RANK)),
                pl.BlockSpec((tm, MLA_KV_RANK), lambda i: (i, COL_CKV // MLA_KV_RANK)),
                pl.BlockSpec((tm, LANES), lambda i: (i, COL_KR // LANES)),
                tab, tab, tab, full(gq), full(gkv), full(wq), full(wk), full(wv)]
    if absorbed:
        out_specs = [pl.BlockSpec((MLA_HEADS, tm, MLA_KV_RANK), lambda i: (0, i, 0)),
                     pl.BlockSpec((MLA_HEADS, tm, LANES), lambda i: (0, i, 0))]
        out_shape = [jax.ShapeDtypeStruct((MLA_HEADS, t, MLA_KV_RANK), F32),
                     jax.ShapeDtypeStruct((MLA_HEADS, t, LANES), F32)]
    else:
        out_specs = [row(MLA_HEADS * QK_PAD), row(MLA_HEADS * QK_PAD), row(MLA_WIDTH)]
        out_shape = [jax.ShapeDtypeStruct((t, MLA_HEADS * QK_PAD), BF16),
                     jax.ShapeDtypeStruct((t, MLA_HEADS * QK_PAD), BF16),
                     jax.ShapeDtypeStruct((t, MLA_WIDTH), BF16)]
    out_specs += [row(MLA_KV_RANK), row(MLA_ROPE)]
    out_shape += [jax.ShapeDtypeStruct((t, MLA_KV_RANK), F32), jax.ShapeDtypeStruct((t, MLA_ROPE), F32)]
    return pl.pallas_call(
        functools.partial(_mla_proj_kernel, absorbed=absorbed),
        grid=(t // tm,),
        in_specs=in_specs, out_specs=out_specs, out_shape=out_shape,
        compiler_params=_params(("arbitrary",)),
        name="mla_proj_absorbed" if absorbed else "mla_proj",
    )(proj, proj, proj, *tabs, gq, gkv, wq, wk, wv)


def _attn_kernel(q_ref, k_ref, v_ref, g_ref, o_ref, *, tq, tk):
    qi = pl.program_id(2)
    q = q_ref[...]
    q_pos = qi * tq + lax.broadcasted_iota(jnp.int32, (tq, tk), 0)
    k_off = lax.broadcasted_iota(jnp.int32, (tq, tk), 1)

    def body(j, carry):
        m, l, acc = carry
        k0 = pl.multiple_of(j * tk, tk)
        s = lax.dot_general(q, k_ref[pl.ds(k0, tk), :], (((1,), (1,)), ((), ())), preferred_element_type=F32)
        s = jnp.where(k0 + k_off <= q_pos, s, -jnp.inf)
        m_new = jnp.maximum(m, jnp.max(s, axis=-1, keepdims=True))
        alpha = jnp.exp(m - m_new)
        p = jnp.exp(s - m_new)
        l = alpha * l + jnp.sum(p, axis=-1, keepdims=True)
        acc = alpha * acc + jnp.dot(p.astype(BF16), v_ref[pl.ds(k0, tk), :], preferred_element_type=F32)
        return m_new, l, acc

    n_blocks = (qi * tq + tq + tk - 1) // tk
    init = (jnp.full((tq, 1), -jnp.inf, F32), jnp.zeros((tq, 1), F32), jnp.zeros((tq, MLA_DV), F32))
    _, l, acc = lax.fori_loop(0, n_blocks, body, init)
    o_ref[...] = _rms(acc / l, g_ref[...])


def _prompt_attention(q_cat, k_cat, v, g, bsz, seq, tq, tk):
    nq = seq // tq
    return pl.pallas_call(
        functools.partial(_attn_kernel, tq=tq, tk=tk),
        grid=(bsz, MLA_HEADS, nq),
        in_specs=[pl.BlockSpec((tq, QK_PAD), lambda b, h, i: (b * nq + i, h)),
                  pl.BlockSpec((seq, QK_PAD), lambda b, h, i: (b, h)),
                  pl.BlockSpec((seq, MLA_DV), lambda b, h, i: (b, h)),
                  pl.BlockSpec((1, MLA_DV), lambda b, h, i: (0, 0))],
        out_specs=pl.BlockSpec((tq, MLA_DV), lambda b, h, i: (b * nq + i, h)),
        out_shape=jax.ShapeDtypeStruct((bsz * seq, MLA_WIDTH), F32),
        compiler_params=_params(("arbitrary", "arbitrary", "arbitrary")),
        name="prompt_attention",
    )(q_cat, k_cat, v, g)


def _paged_attn_kernel(pt_ref, qlat_ref, qr_ref, cnew_ref, knew_ref, *rest, pp, n_q):
    ckv_refs, kr_refs = rest[:pp], rest[pp:2 * pp]
    o_ref, m_scr, l_scr, acc_scr = rest[2 * pp:]
    step = pl.program_id(1)
    rows = MLA_HEADS * n_q

    @pl.when(step == 0)
    def _():
        m_scr[...] = jnp.full(m_scr.shape, -jnp.inf, F32)
        l_scr[...] = jnp.zeros(l_scr.shape, F32)
        acc_scr[...] = jnp.zeros(acc_scr.shape, F32)

    q = qlat_ref[...].reshape(rows, MLA_KV_RANK).astype(BF16)
    qr = qr_ref[...].reshape(rows, LANES)[:, :MLA_ROPE].astype(BF16)

    def scores(ckv, kr):
        return (lax.dot_general(q, ckv, (((1,), (1,)), ((), ())), preferred_element_type=F32)
                + lax.dot_general(qr, kr, (((1,), (1,)), ((), ())), preferred_element_type=F32))

    def update(s_list, v_list):
        s = jnp.concatenate(s_list, axis=1) if len(s_list) > 1 else s_list[0]
        m = m_scr[...]
        m_new = jnp.maximum(m, jnp.max(s, axis=-1, keepdims=True))
        alpha = jnp.exp(m - m_new)
        p = jnp.exp(s - m_new)
        l_scr[...] = alpha * l_scr[...] + jnp.sum(p, axis=-1, keepdims=True)
        pv = acc_scr[...] * alpha
        for j, vj in enumerate(v_list):
            pv = pv + jnp.dot(p[:, j * PAGE_SIZE:(j + 1) * PAGE_SIZE].astype(BF16), vj,
                              preferred_element_type=F32)
        acc_scr[...] = pv
        m_scr[...] = m_new

    pages = [r[...].astype(BF16) for r in ckv_refs]
    update([scores(pages[j], kr_refs[j][...].astype(BF16)) for j in range(pp)], pages)

    @pl.when(step == pl.num_programs(1) - 1)
    def _():
        pad = PAGE_SIZE - n_q
        c_new = jnp.concatenate([cnew_ref[...], jnp.zeros((pad, MLA_KV_RANK), F32)], axis=0).astype(BF16)
        k_new = jnp.concatenate([knew_ref[...], jnp.zeros((pad, MLA_ROPE), F32)], axis=0).astype(BF16)
        s = scores(c_new, k_new)
        q_idx = lax.broadcasted_iota(jnp.int32, s.shape, 0) % n_q
        k_idx = lax.broadcasted_iota(jnp.int32, s.shape, 1)
        update([jnp.where(k_idx <= q_idx, s, -jnp.inf)], [c_new])
        o_ref[...] = (acc_scr[...] / l_scr[...]).reshape(o_ref.shape)


def _paged_attention(q_lat, q_r, ckv_n, k_rope, cache_ckv, cache_krope, page_table, n_q, pp):
    bsz, n_pages = page_table.shape
    rows = MLA_HEADS * n_q

    def page_spec(width, j):
        return pl.BlockSpec((None, None, PAGE_SIZE, width),
                            lambda b, s, pt: (0, pt[b * n_pages + s * pp + j], 0, 0))

    grid_spec = pltpu.PrefetchScalarGridSpec(
        num_scalar_prefetch=1,
        grid=(bsz, n_pages // pp),
        in_specs=[pl.BlockSpec((MLA_HEADS, n_q, MLA_KV_RANK), lambda b, s, pt: (0, b, 0)),
                  pl.BlockSpec((MLA_HEADS, n_q, LANES), lambda b, s, pt: (0, b, 0)),
                  pl.BlockSpec((n_q, MLA_KV_RANK), lambda b, s, pt: (b, 0)),
                  pl.BlockSpec((n_q, MLA_ROPE), lambda b, s, pt: (b, 0))]
                 + [page_spec(MLA_KV_RANK, j) for j in range(pp)]
                 + [page_spec(MLA_ROPE, j) for j in range(pp)],
        out_specs=pl.BlockSpec((MLA_HEADS, n_q, MLA_KV_RANK), lambda b, s, pt: (0, b, 0)),
        scratch_shapes=[pltpu.VMEM((rows, 1), F32), pltpu.VMEM((rows, 1), F32),
                        pltpu.VMEM((rows, MLA_KV_RANK), F32)],
    )
    return pl.pallas_call(
        functools.partial(_paged_attn_kernel, pp=pp, n_q=n_q),
        grid_spec=grid_spec,
        out_shape=jax.ShapeDtypeStruct((MLA_HEADS, bsz * n_q, MLA_KV_RANK), F32),
        compiler_params=_params(("arbitrary", "arbitrary")),
        name="paged_attention",
    )(page_table.reshape(-1), q_lat, q_r, ckv_n, k_rope, *([cache_ckv] * pp), *([cache_krope] * pp))


def _latent_out_kernel(o_ref, w_ref, g_ref, out_ref):
    o = jnp.dot(o_ref[...].astype(BF16), w_ref[...], preferred_element_type=F32)
    out_ref[...] = _rms(o, g_ref[...])


def _latent_out(o_lat, w_uv_h, g, tm):
    t = o_lat.shape[1]
    return pl.pallas_call(
        _latent_out_kernel,
        grid=(MLA_HEADS, t // tm),
        in_specs=[pl.BlockSpec((None, tm, MLA_KV_RANK), lambda h, i: (h, i, 0)),
                  pl.BlockSpec((None, MLA_KV_RANK, MLA_DV), lambda h, i: (h, 0, 0)),
                  pl.BlockSpec((1, MLA_DV), lambda h, i: (0, 0))],
        out_specs=pl.BlockSpec((tm, MLA_DV), lambda h, i: (i, h)),
        out_shape=jax.ShapeDtypeStruct((t, MLA_WIDTH), F32),
        compiler_params=_params(("arbitrary", "arbitrary")),
        name="latent_out",
    )(o_lat, w_uv_h, g)


def _outproj_kernel(og_ref, om_ref, wg_ref, wm_ref, x_ref, gt_ref, sc_ref, sh_ref, g_ref, x1_ref, h2_ref):
    y = (jnp.dot(og_ref[...].astype(BF16), wg_ref[...], preferred_element_type=F32)
         + jnp.dot(om_ref[...].astype(BF16), wm_ref[...], preferred_element_type=F32))
    x1 = x_ref[...] + gt_ref[...] * y.reshape(x_ref.shape)
    x1_ref[...] = x1
    h2 = _rms(x1, g_ref[...][None]) * (1.0 + sc_ref[...]) + sh_ref[...]
    h2_ref[...] = h2.reshape(h2_ref.shape)


def _outproj(o_gla, o_mla, w_top, w_bot, x, gt, sc, sh, g, bb, lt):
    bsz, seq, d = x.shape
    rows = bb * lt
    nl = seq // lt
    xs = pl.BlockSpec((bb, lt, d), lambda i: (i // nl, i % nl, 0))
    ms = pl.BlockSpec((bb, 1, d), lambda i: (i // nl, 0, 0))
    return pl.pallas_call(
        _outproj_kernel,
        grid=(bsz * seq // rows,),
        in_specs=[pl.BlockSpec((rows, GLA_WIDTH), lambda i: (i, 0)),
                  pl.BlockSpec((rows, MLA_WIDTH), lambda i: (i, 0)),
                  pl.BlockSpec(w_top.shape, lambda i: (0, 0)),
                  pl.BlockSpec(w_bot.shape, lambda i: (0, 0)),
                  xs, ms, ms, ms, pl.BlockSpec((1, d), lambda i: (0, 0))],
        out_specs=[xs, pl.BlockSpec((rows, d), lambda i: (i, 0))],
        out_shape=[jax.ShapeDtypeStruct(x.shape, F32), jax.ShapeDtypeStruct((bsz * seq, d), F32)],
        compiler_params=_params(("arbitrary",)),
        name="out_proj",
    )(o_gla, o_mla, w_top, w_bot, x, gt, sc, sh, g)


def _router_kernel(h_ref, w_ref, b_ref, id_ref, wt_ref):
    logit = jnp.dot(h_ref[...], w_ref[...], precision=HIGHEST, preferred_element_type=F32) + b_ref[...]
    lane = lax.broadcasted_iota(jnp.int32, logit.shape, 1)
    neg = -jnp.inf
    big = jnp.int32(LANES)

    def first_max(vals):
        top = jnp.max(vals, axis=-1, keepdims=True)
        return top, jnp.min(jnp.where(vals == top, lane, big), axis=-1, keepdims=True)

    g_vals = jnp.where(lane < N_GROUPS, logit, neg)
    g_max, g_idx = first_max(g_vals)
    g_w = 1.0 / jnp.sum(jnp.exp(g_vals - g_max), axis=-1, keepdims=True)
    e_lane = lane - N_GROUPS
    in_group = (e_lane >= g_idx * EXPERTS_PER_GROUP) & (e_lane < (g_idx + 1) * EXPERTS_PER_GROUP)
    e_vals = jnp.where(in_group, logit, neg)
    v1, i1 = first_max(e_vals)
    v2, i2 = first_max(jnp.where(lane == i1, neg, e_vals))
    e2 = jnp.exp(v2 - v1)
    w1 = g_w / (1.0 + e2)
    w2 = g_w * e2 / (1.0 + e2)
    id_ref[...] = jnp.where(lane == 0, i1 - N_GROUPS, jnp.where(lane == 1, i2 - N_GROUPS, 0))
    wt_ref[...] = jnp.where(lane == 0, w1, jnp.where(lane == 1, w2, 0.0))


def _router(h2, w_r, b_r, tm):
    t, d = h2.shape
    o = pl.BlockSpec((tm, LANES), lambda i: (i, 0))
    return pl.pallas_call(
        _router_kernel,
        grid=(t // tm,),
        in_specs=[pl.BlockSpec((tm, d), lambda i: (i, 0)),
                  pl.BlockSpec(w_r.shape, lambda i: (0, 0)),
                  pl.BlockSpec((1, LANES), lambda i: (0, 0))],
        out_specs=[o, o],
        out_shape=[jax.ShapeDtypeStruct((t, LANES), jnp.int32), jax.ShapeDtypeStruct((t, LANES), F32)],
        compiler_params=_params(("arbitrary",)),
        name="router",
    )(h2, w_r, b_r)


def _moe_kernel(ie_ref, ir_ref, in_ref, tok_ref, dst_ref, h_hbm, wgu_ref, wd_ref, y_hbm,
                x_scr, gu_scr, y_scr, gsem, ssem):
    item = pl.program_id(0)
    step = pl.program_id(1)
    n = in_ref[item]
    row0 = ir_ref[item]

    def row_copy(r):
        return pltpu.make_async_copy(h_hbm.at[pl.ds(tok_ref[row0 + r], 1)], x_scr.at[pl.ds(r, 1)], gsem)

    def out_copy(r):
        return pltpu.make_async_copy(y_scr.at[pl.ds(r, 1)], y_hbm.at[pl.ds(dst_ref[row0 + r], 1)], ssem)

    @pl.when((step == 0) & (n > 0))
    def _():
        x_scr[...] = jnp.zeros(x_scr.shape, F32)
        lax.fori_loop(0, n, lambda r, c: (row_copy(r).start(), c)[1], 0)
        lax.fori_loop(0, n, lambda r, c: (row_copy(r).wait(), c)[1], 0)

    for kk in range(MOE_KSTEPS):
        @pl.when((step == kk) & (n > 0))
        def _(kk=kk):
            xk = x_scr[:, kk * MOE_KCHUNK:(kk + 1) * MOE_KCHUNK].astype(BF16)
            part = jnp.dot(xk, wgu_ref[...].astype(BF16), preferred_element_type=F32)
            if kk == 0:
                gu_scr[...] = part
            else:
                gu_scr[...] += part

    @pl.when((step == MOE_KSTEPS) & (n > 0))
    def _():
        act = (_silu(gu_scr[:, :EXPERT_FF]) * gu_scr[:, EXPERT_FF:]).astype(BF16)
        y_scr[...] = jnp.dot(act, wd_ref[...].astype(BF16), preferred_element_type=F32)
        lax.fori_loop(0, n, lambda r, c: (out_copy(r).start(), c)[1], 0)
        lax.fori_loop(0, n, lambda r, c: (out_copy(r).wait(), c)[1], 0)


def _moe(items, tok_s, dst_s, h2, w_gu, w_down):
    item_e, item_row0, item_n = items
    n_items = item_e.shape[0]
    t, d = h2.shape
    last_k = MOE_KSTEPS - 1

    def gu_map(i, s, ie, ir, inn, tok, dst):
        return (ie[i], jnp.where(inn[i] > 0, jnp.minimum(s, last_k), last_k), 0)

    grid_spec = pltpu.PrefetchScalarGridSpec(
        num_scalar_prefetch=5,
        grid=(n_items, MOE_KSTEPS + 1),
        in_specs=[pl.BlockSpec(memory_space=pl.ANY),
                  pl.BlockSpec((None, MOE_KCHUNK, 2 * EXPERT_FF), gu_map),
                  pl.BlockSpec((None, EXPERT_FF, d), lambda i, s, ie, ir, inn, tok, dst: (ie[i], 0, 0))],
        out_specs=pl.BlockSpec(memory_space=pl.ANY),
        scratch_shapes=[pltpu.VMEM((MOE_ROWS, d), F32),
                        pltpu.VMEM((MOE_ROWS, 2 * EXPERT_FF), F32),
                        pltpu.VMEM((MOE_ROWS, d), F32),
                        pltpu.SemaphoreType.DMA(()),
                        pltpu.SemaphoreType.DMA(())],
    )
    return pl.pallas_call(
        _moe_kernel,
        grid_spec=grid_spec,
        out_shape=jax.ShapeDtypeStruct((t * TOP_K, d), F32),
        compiler_params=_params(("arbitrary", "arbitrary")),
        name="moe_experts",
    )(item_e, item_row0, item_n, tok_s, dst_s, h2, w_gu, w_down)


def _moe_plan(e_id):
    n_as = e_id.shape[0]
    n_items = N_EXPERTS + n_as // MOE_ROWS
    order = jnp.argsort(e_id).astype(jnp.int32)
    cnt = jnp.zeros((N_EXPERTS,), jnp.int32).at[e_id].add(1)
    ustart = jnp.cumsum(cnt) - cnt
    per_e = (cnt + MOE_ROWS - 1) // MOE_ROWS
    item_end = jnp.cumsum(per_e)
    total = item_end[-1]
    idx = jnp.arange(n_items, dtype=jnp.int32)
    live = idx < total
    e_of = jnp.searchsorted(item_end, jnp.minimum(idx, total - 1), side='right').astype(jnp.int32)
    local = jnp.minimum(idx, total - 1) - (item_end[e_of] - per_e[e_of])
    row0 = ustart[e_of] + local * MOE_ROWS
    n_rows = jnp.where(live, jnp.minimum(MOE_ROWS, cnt[e_of] - local * MOE_ROWS), 0)
    tok_s = order // TOP_K
    return (e_of, row0.astype(jnp.int32), n_rows.astype(jnp.int32)), tok_s, order


def _final_kernel(x_ref, gt_ref, y_ref, wt_ref, g_ref, o_ref):
    wt = wt_ref[...]
    y = wt[:, 0:1] * y_ref[:, :D_MODEL] + wt[:, 1:2] * y_ref[:, D_MODEL:]
    x2 = x_ref[...] + gt_ref[...] * y.reshape(x_ref.shape)
    o_ref[...] = _rms(x2, g_ref[...][None])


def _final(x1, gt, y_pair, wt, g, bb, lt, row_block0):
    bsz, seq, d = x1.shape
    rows = bb * lt
    nl = seq // lt
    xs = pl.BlockSpec((bb, lt, d), lambda i: (i // nl, i % nl, 0))
    return pl.pallas_call(
        _final_kernel,
        grid=(bsz * seq // rows,),
        in_specs=[xs, pl.BlockSpec((bb, 1, d), lambda i: (i // nl, 0, 0)),
                  pl.BlockSpec((rows, TOP_K * d), lambda i: (row_block0 + i, 0)),
                  pl.BlockSpec((rows, LANES), lambda i: (row_block0 + i, 0)),
                  pl.BlockSpec((1, d), lambda i: (0, 0))],
        out_specs=xs,
        out_shape=jax.ShapeDtypeStruct(x1.shape, F32),
        compiler_params=_params(("arbitrary",)),
        name="moe_combine_final_norm",
    )(x1, gt, y_pair, wt, g)


def _rope_tables(pos, rows):
    half = MLA_ROPE // 2
    inv_freq = ROPE_BASE ** (-jnp.arange(half, dtype=F32) / half)
    ang = pos.astype(F32)[:, None] * inv_freq[None, :]
    cos, sin = jnp.cos(ang), jnp.sin(ang)
    zero = jnp.zeros_like(cos)
    pad = jnp.zeros((pos.shape[0], LANES - MLA_ROPE), F32)
    tabs = (jnp.concatenate([cos, cos, pad], axis=1),
            jnp.concatenate([-sin, zero, pad], axis=1),
            jnp.concatenate([zero, sin, pad], axis=1))
    reps = rows // pos.shape[0]
    return tuple(jnp.tile(t, (reps, 1)) for t in tabs)


def _arrange_w_in(w_in):
    d = w_in.shape[0]
    off_gv = 2 * GLA_QK
    off_ga = off_gv + GLA_WIDTH
    off_gr = off_ga + GLA_GATE_RANK
    off_cq = off_gr + GLA_WIDTH
    off_ckv = off_cq + MLA_Q_RANK
    off_kr = off_ckv + MLA_KV_RANK
    pad = jnp.zeros((d, LANES - MLA_ROPE - GLA_GATE_RANK), w_in.dtype)
    return jnp.concatenate([w_in[:, :off_ga], w_in[:, off_gr:off_kr + MLA_ROPE], w_in[:, off_ga:off_gr], pad],
                           axis=1).astype(BF16)


def _arrange_w_uq(w_uq):
    r = w_uq.shape[0]
    pad = jnp.zeros((r, MLA_HEADS, QK_PAD - MLA_NOPE - MLA_ROPE), w_uq.dtype)
    return jnp.concatenate([w_uq, pad], axis=2).reshape(r, MLA_HEADS * QK_PAD).astype(BF16)


def _mixer(x, mods, pos, s0, cache, w, prompt):
    bsz, seq, d = x.shape
    t = bsz * seq
    sh1, sc1, gt1, sh2, sc2, _ = mods
    bb, lt = (1, 512) if prompt else (512 // seq, seq)
    proj = _inproj(x, w['g_mix'], sc1, sh1, w['w_in'], bb, lt)
    gbb, glt = (1, 512) if prompt else (8, seq)
    o_gla, s_fin = _gla(proj, s0, w['wg_pad'], w['bg'], w['g_gla'], bsz, seq, gbb, glt)
    tm = 256
    if prompt:
        tabs = _rope_tables(pos, seq)
        q_cat, k_cat, v, ckv_n, k_rope = _mla_proj(proj, tabs, w['g_q'], w['g_kv'], w['w_uq'], w['w_uk'],
                                                    w['w_uv'], tm, absorbed=False)
        o_mla = _prompt_attention(q_cat, k_cat, v, w['g_mla'], bsz, seq, 512, 512)
    else:
        cache_ckv, cache_krope, page_table = cache
        tabs = _rope_tables(pos, tm)
        q_lat, q_r, ckv_n, k_rope = _mla_proj(proj, tabs, w['g_q'], w['g_kv'], w['w_uq'], w['w_uk_t'],
                                              w['w_uv'], tm, absorbed=True)
        o_lat = _paged_attention(q_lat, q_r, ckv_n, k_rope, cache_ckv, cache_krope, page_table, seq, 8)
        o_mla = _latent_out(o_lat, w['w_uv_h'], w['g_mla'], tm)
    obb, olt = (1, 256) if prompt else (256 // seq, seq)
    x1, h2 = _outproj(o_gla, o_mla, w['w_out_top'], w['w_out_bot'], x, gt1, sc2, sh2, w['g_ffn'], obb, olt)
    return x1, h2, ckv_n.reshape(bsz, seq, MLA_KV_RANK), k_rope.reshape(bsz, seq, MLA_ROPE), s_fin


def kernel(x_prompt, x_sample, c_prompt, c_sample, cache_ckv, cache_krope, state_gla, page_table, w_ada, b_ada, g_mix, g_ffn, w_in, gla_w_gate, gla_b_gate, gla_g_onorm, mla_g_qnorm, mla_w_uq, mla_g_kvnorm, mla_w_uk, mla_w_uv, mla_g_onorm, w_out, moe_w_group, moe_b_group, moe_w_expert, moe_b_expert, moe_w_gu, moe_w_down, g_final):
    depth = w_ada.shape[0]
    assert depth == 1
    bp, lp, d = x_prompt.shape
    bs, ls, _ = x_sample.shape
    n_past = page_table.shape[1] * PAGE_SIZE
    lyr = 0

    n_c = bp + bs
    c_rows = -(-n_c // 8) * 8
    c_all = jnp.concatenate([c_prompt, c_sample, jnp.zeros((c_rows - n_c, d), F32)], axis=0)
    mod = _ada(c_all, w_ada[lyr], b_ada[lyr][None])
    mods_p = tuple(m[:bp, None, :] for m in jnp.split(mod, 6, axis=-1))
    mods_s = tuple(m[bp:n_c, None, :] for m in jnp.split(mod, 6, axis=-1))

    wg_pad = jnp.zeros((LANES, GLA_QK), F32).at[GA_LANE:GA_LANE + GLA_GATE_RANK].set(gla_w_gate[lyr])
    w_uk = mla_w_uk[lyr]
    w_uv = mla_w_uv[lyr]
    w = {
        'g_mix': g_mix[lyr][None], 'g_ffn': g_ffn[lyr][None],
        'w_in': _arrange_w_in(w_in[lyr]),
        'wg_pad': wg_pad, 'bg': gla_b_gate[lyr][None], 'g_gla': gla_g_onorm[lyr][None],
        'g_q': mla_g_qnorm[lyr][None], 'g_kv': mla_g_kvnorm[lyr][None], 'g_mla': mla_g_onorm[lyr][None],
        'w_uq': _arrange_w_uq(mla_w_uq[lyr]),
        'w_uk': w_uk.reshape(MLA_KV_RANK, MLA_HEADS * MLA_NOPE).astype(BF16),
        'w_uk_t': w_uk.transpose(1, 2, 0).astype(BF16),
        'w_uv': w_uv.reshape(MLA_KV_RANK, MLA_WIDTH).astype(BF16),
        'w_uv_h': w_uv.transpose(1, 0, 2).astype(BF16),
        'w_out_top': w_out[lyr][:GLA_WIDTH].astype(BF16),
        'w_out_bot': w_out[lyr][GLA_WIDTH:].astype(BF16),
    }

    pos_p = jnp.arange(lp, dtype=jnp.int32)
    pos_s = n_past + jnp.arange(ls, dtype=jnp.int32)
    s0_p = jnp.zeros((bp, GLA_HEADS, GLA_DK, GLA_DV), F32)
    x1_p, h2_p, ckv_p, kr_p, st_p = _mixer(x_prompt, mods_p, pos_p, s0_p, None, w, prompt=True)
    x1_s, h2_s, ckv_s, kr_s, st_s = _mixer(x_sample, mods_s, pos_s, state_gla[lyr],
                                           (cache_ckv, cache_krope, page_table), w, prompt=False)

    h2 = jnp.concatenate([h2_p, h2_s], axis=0)
    w_r = jnp.concatenate([moe_w_group[lyr], moe_w_expert[lyr],
                           jnp.zeros((d, LANES - N_GROUPS - N_EXPERTS), F32)], axis=1)
    b_r = jnp.concatenate([moe_b_group[lyr], moe_b_expert[lyr],
                           jnp.zeros((LANES - N_GROUPS - N_EXPERTS,), F32)])[None]
    ids, wts = _router(h2, w_r, b_r, 512)
    items, tok_s, dst_s = _moe_plan(ids[:, :TOP_K].reshape(-1))
    y_rows = _moe(items, tok_s, dst_s, h2, moe_w_gu[lyr], moe_w_down[lyr])
    y_pair = y_rows.reshape(h2.shape[0], TOP_K * d)

    y_p = _final(x1_p, mods_p[5], y_pair, wts, g_final[None], 1, 256, 0)
    y_s = _final(x1_s, mods_s[5], y_pair, wts, g_final[None], 256 // ls, ls, bp * lp // 256)
    return (y_p, y_s, ckv_p[None], kr_p[None], st_p[None], ckv_s[None], kr_s[None], st_s[None])
```

```python
import functools
import math

import jax
import jax.numpy as jnp
from jax import lax
from jax.experimental import pallas as pl
from jax.experimental.pallas import tpu as pltpu

F32 = jnp.float32
BF16 = jnp.bfloat16
HIGHEST = lax.Precision.HIGHEST

D_MODEL = 2048
PAGE_SIZE = 128
GLA_HEADS = 4
GLA_DK = 128
GLA_DV = 256
GLA_GATE_RANK = 16
GLA_TAU = 16.0
GLA_CHUNK = 64
GLA_QK = GLA_HEADS * GLA_DK
GLA_WIDTH = GLA_HEADS * GLA_DV
MLA_HEADS = 8
MLA_Q_RANK = 512
MLA_KV_RANK = 512
MLA_NOPE = 128
MLA_ROPE = 64
MLA_DV = 128
MLA_WIDTH = MLA_HEADS * MLA_DV
ROPE_BASE = 10000.0
ATTN_SCALE = (MLA_NOPE + MLA_ROPE) ** -0.5
N_GROUPS = 8
EXPERTS_PER_GROUP = 8
N_EXPERTS = N_GROUPS * EXPERTS_PER_GROUP
TOP_K = 2
EXPERT_FF = 1408
EPS = 1e-6

LANES = 128
QK_PAD = 2 * LANES
COL_GQ = 0
COL_GK = GLA_QK
COL_GV = 2 * GLA_QK
COL_GR = COL_GV + GLA_WIDTH
COL_CQ = COL_GR + GLA_WIDTH
COL_CKV = COL_CQ + MLA_Q_RANK
COL_KR = COL_CKV + MLA_KV_RANK
IN_COLS_PAD = COL_KR + LANES
GA_LANE = MLA_ROPE
IN_COL_TILE = IN_COLS_PAD // 3

MOE_ROWS = 512
MOE_KSTEPS = 4
MOE_KCHUNK = D_MODEL // MOE_KSTEPS
VMEM_LIMIT = 56 * 1024 * 1024


def _params(sem, vmem=VMEM_LIMIT):
    return pltpu.CompilerParams(dimension_semantics=sem, vmem_limit_bytes=vmem)


def _silu(x):
    return x * jax.nn.sigmoid(x)


def _rms(x, g):
    return x * lax.rsqrt(jnp.mean(x * x, axis=-1, keepdims=True) + EPS) * g


def _ada_kernel(c_ref, w_ref, b_ref, o_ref):
    a = _silu(c_ref[...]).astype(BF16)
    o_ref[...] = jnp.dot(a, w_ref[...].astype(BF16), preferred_element_type=F32) + b_ref[...]


def _ada(c, w, b):
    rows, d = c.shape
    n = w.shape[1]
    tn = 1024
    return pl.pallas_call(
        _ada_kernel,
        grid=(n // tn,),
        in_specs=[pl.BlockSpec((rows, d), lambda j: (0, 0)),
                  pl.BlockSpec((d, tn), lambda j: (0, j)),
                  pl.BlockSpec((1, tn), lambda j: (0, j))],
        out_specs=pl.BlockSpec((rows, tn), lambda j: (0, j)),
        out_shape=jax.ShapeDtypeStruct((rows, n), F32),
        compiler_params=_params(("arbitrary",)),
        name="ada_mod",
    )(c, w, b)


def _inproj_kernel(x_ref, g_ref, sc_ref, sh_ref, w_ref, o_ref, h_scr):
    @pl.when(pl.program_id(1) == 0)
    def _():
        h = _rms(x_ref[...], g_ref[...][None]) * (1.0 + sc_ref[...]) + sh_ref[...]
        h_scr[...] = h.reshape(h_scr.shape).astype(BF16)

    o_ref[...] = jnp.dot(h_scr[...], w_ref[...], preferred_element_type=F32)


def _inproj(x, g, sc, sh, w, bb, lt):
    bsz, seq, d = x.shape
    rows = bb * lt
    nl = seq // lt
    n = w.shape[1]
    return pl.pallas_call(
        _inproj_kernel,
        grid=(bsz * seq // rows, n // IN_COL_TILE),
        in_specs=[pl.BlockSpec((bb, lt, d), lambda i, j: (i // nl, i % nl, 0)),
                  pl.BlockSpec((1, d), lambda i, j: (0, 0)),
                  pl.BlockSpec((bb, 1, d), lambda i, j: (i // nl, 0, 0)),
                  pl.BlockSpec((bb, 1, d), lambda i, j: (i // nl, 0, 0)),
                  pl.BlockSpec((d, IN_COL_TILE), lambda i, j: (0, j))],
        out_specs=pl.BlockSpec((rows, IN_COL_TILE), lambda i, j: (i, j)),
        out_shape=jax.ShapeDtypeStruct((bsz * seq, n), F32),
        scratch_shapes=[pltpu.VMEM((rows, d), BF16)],
        compiler_params=_params(("arbitrary", "arbitrary")),
        name="in_proj",
    )(x, g, sc, sh, w)


def _gla_kernel(q_ref, k_ref, v_ref, gr_ref, ga_ref, wg_ref, bg_ref, gon_ref, s0_ref,
                o_ref, sout_ref, s_scr, *, n_chunks, n_valid):
    c = GLA_CHUNK
    l_step = pl.program_id(2)

    @pl.when(l_step == 0)
    def _():
        s_scr[...] = s0_ref[...]

    row = lax.broadcasted_iota(jnp.int32, (c, c), 0)
    col = lax.broadcasted_iota(jnp.int32, (c, c), 1)
    causal = col <= row
    tri = causal.astype(F32)
    chunks_per_batch = n_chunks // s_scr.shape[0]

    def load(ref, r0):
        x = ref[pl.ds(r0, n_valid), :]
        if n_valid < c:
            x = jnp.concatenate([x, jnp.zeros((c - n_valid, x.shape[1]), x.dtype)], axis=0)
        return x

    def chunk(idx):
        r0 = idx * n_valid
        bi = idx // chunks_per_batch
        q = load(q_ref, r0) * (GLA_DK ** -0.5)
        k = load(k_ref, r0)
        v = load(v_ref, r0).astype(BF16)
        z = jnp.dot(load(ga_ref, r0), wg_ref[...], precision=HIGHEST, preferred_element_type=F32) + bg_ref[...]
        log_a = (jnp.minimum(z, 0.0) - jnp.log1p(jnp.exp(-jnp.abs(z)))) / GLA_TAU
        if n_valid < c:
            log_a = jnp.where(lax.broadcasted_iota(jnp.int32, log_a.shape, 0) < n_valid, log_a, 0.0)
        b = jnp.dot(tri, log_a, precision=HIGHEST, preferred_element_type=F32)
        b_last = b[c - 1:c, :]
        q_dec = (q * jnp.exp(b)).astype(BF16)
        k_inv = (k * jnp.exp(-b)).astype(BF16)
        att = lax.dot_general(q_dec, k_inv, (((1,), (1,)), ((), ())), preferred_element_type=F32)
        att = jnp.where(causal, att, 0.0).astype(BF16)
        s_old = s_scr[bi]
        o = (jnp.dot(q_dec, s_old.astype(BF16), preferred_element_type=F32)
             + jnp.dot(att, v, preferred_element_type=F32))
        k_dec = (k * jnp.exp(b_last - b)).astype(BF16)
        kv = lax.dot_general(k_dec, v, (((0,), (0,)), ((), ())), preferred_element_type=F32)
        decay = jnp.exp(jnp.broadcast_to(b_last, (GLA_DK, GLA_DK)).T)
        s_scr[bi] = jnp.concatenate([decay] * (GLA_DV // GLA_DK), axis=1) * s_old + kv
        o = _rms(o[:n_valid], gon_ref[...])
        o_ref[pl.ds(r0, n_valid), :] = o * _silu(gr_ref[pl.ds(r0, n_valid), :])

    for idx in range(n_chunks):
        chunk(idx)

    @pl.when(l_step == pl.num_programs(2) - 1)
    def _():
        sout_ref[...] = s_scr[...]


def _gla(proj, s0, wg_pad, bg, gon, bsz, seq, bb, lt):
    n_valid = math.gcd(seq, GLA_CHUNK)
    rows = bb * lt
    nl = seq // lt
    rmap = lambda b, h, l: b * nl + l
    kern = functools.partial(_gla_kernel, n_chunks=rows // n_valid, n_valid=n_valid)
    st_spec = pl.BlockSpec((bb, None, GLA_DK, GLA_DV), lambda b, h, l: (b, h, 0, 0))
    return pl.pallas_call(
        kern,
        grid=(bsz // bb, GLA_HEADS, nl),
        in_specs=[pl.BlockSpec((rows, GLA_DK), lambda b, h, l: (rmap(b, h, l), COL_GQ // GLA_DK + h)),
                  pl.BlockSpec((rows, GLA_DK), lambda b, h, l: (rmap(b, h, l), COL_GK // GLA_DK + h)),
                  pl.BlockSpec((rows, GLA_DV), lambda b, h, l: (rmap(b, h, l), COL_GV // GLA_DV + h)),
                  pl.BlockSpec((rows, GLA_DV), lambda b, h, l: (rmap(b, h, l), COL_GR // GLA_DV + h)),
                  pl.BlockSpec((rows, LANES), lambda b, h, l: (rmap(b, h, l), COL_KR // LANES)),
                  pl.BlockSpec((LANES, GLA_DK), lambda b, h, l: (0, h)),
                  pl.BlockSpec((1, GLA_DK), lambda b, h, l: (0, h)),
                  pl.BlockSpec((1, GLA_DV), lambda b, h, l: (0, 0)),
                  st_spec],
        out_specs=[pl.BlockSpec((rows, GLA_DV), lambda b, h, l: (rmap(b, h, l), h)), st_spec],
        out_shape=[jax.ShapeDtypeStruct((bsz * seq, GLA_WIDTH), F32),
                   jax.ShapeDtypeStruct((bsz, GLA_HEADS, GLA_DK, GLA_DV), F32)],
        scratch_shapes=[pltpu.VMEM((bb, GLA_DK, GLA_DV), F32)],
        compiler_params=_params(("arbitrary", "arbitrary", "arbitrary")),
        name="gla",
    )(proj, proj, proj, proj, proj, wg_pad, bg, gon, s0)


def _rope(r, cos, sin_lo, sin_hi):
    return (r * cos + pltpu.roll(r, LANES - MLA_ROPE // 2, axis=1) * sin_lo
            + pltpu.roll(r, MLA_ROPE // 2, axis=1) * sin_hi)


def _mla_proj_kernel(cq_ref, ckv_ref, kr_ref, cos_ref, slo_ref, shi_ref, gq_ref, gkv_ref, wq_ref,
                     wk_ref, wv_ref, *out_refs, absorbed):
    cos, slo, shi = cos_ref[...], slo_ref[...], shi_ref[...]
    cqn = _rms(cq_ref[...], gq_ref[...]).astype(BF16)
    qf = jnp.dot(cqn, wq_ref[...], preferred_element_type=F32)
    ckv_n = _rms(ckv_ref[...], gkv_ref[...])
    k_rot = _rope(kr_ref[...], cos, slo, shi)
    if absorbed:
        qlat_ref, qr_ref, ckvn_ref, krope_ref = out_refs
    else:
        qcat_ref, kcat_ref, v_ref, ckvn_ref, krope_ref = out_refs
    ckvn_ref[...] = ckv_n
    krope_ref[...] = k_rot[:, :MLA_ROPE]
    for h in range(MLA_HEADS):
        q_nope = qf[:, h * QK_PAD:h * QK_PAD + LANES] * ATTN_SCALE
        q_rot = _rope(qf[:, h * QK_PAD + LANES:(h + 1) * QK_PAD], cos, slo, shi) * ATTN_SCALE
        if absorbed:
            qlat_ref[h] = jnp.dot(q_nope.astype(BF16), wk_ref[h], preferred_element_type=F32)
            qr_ref[h] = q_rot
        else:
            qcat_ref[:, h * QK_PAD:h * QK_PAD + LANES] = q_nope.astype(BF16)
            qcat_ref[:, h * QK_PAD + LANES:(h + 1) * QK_PAD] = q_rot.astype(BF16)
    if not absorbed:
        ckv_b = ckv_n.astype(BF16)
        k_nope = jnp.dot(ckv_b, wk_ref[...], preferred_element_type=F32)
        k_rot_b = k_rot.astype(BF16)
        for h in range(MLA_HEADS):
            kcat_ref[:, h * QK_PAD:h * QK_PAD + LANES] = k_nope[:, h * LANES:(h + 1) * LANES].astype(BF16)
            kcat_ref[:, h * QK_PAD + LANES:(h + 1) * QK_PAD] = k_rot_b
        v_ref[...] = jnp.dot(ckv_b, wv_ref[...], preferred_element_type=F32).astype(BF16)


def _mla_proj(proj, tabs, gq, gkv, wq, wk, wv, tm, absorbed):
    t = proj.shape[0]
    ntab = tabs[0].shape[0] // tm
    row = lambda c: pl.BlockSpec((tm, c), lambda i: (i, 0))
    full = lambda a: pl.BlockSpec(a.shape, lambda i: (0,) * a.ndim)
    tab = pl.BlockSpec((tm, LANES), lambda i: (i % ntab, 0))
    in_specs = [pl.BlockSpec((tm, MLA_Q_RANK), lambda i: (i, COL_CQ // MLA_Q_RANK)),
                pl.BlockSpec((tm, MLA_KV_RANK), lambda i: (i, COL_CKV // MLA_KV_RANK)),
                pl.BlockSpec((tm, LANES), lambda i: (i, COL_KR // LANES)),
                tab, tab, tab, full(gq), full(gkv), full(wq), full(wk), full(wv)]
    if absorbed:
        out_specs = [pl.BlockSpec((MLA_HEADS, tm, MLA_KV_RANK), lambda i: (0, i, 0)),
                     pl.BlockSpec((MLA_HEADS, tm, LANES), lambda i: (0, i, 0))]
        out_shape = [jax.ShapeDtypeStruct((MLA_HEADS, t, MLA_KV_RANK), F32),
                     jax.ShapeDtypeStruct((MLA_HEADS, t, LANES), F32)]
    else:
        out_specs = [row(MLA_HEADS * QK_PAD), row(MLA_HEADS * QK_PAD), row(MLA_WIDTH)]
        out_shape = [jax.ShapeDtypeStruct((t, MLA_HEADS * QK_PAD), BF16),
                     jax.ShapeDtypeStruct((t, MLA_HEADS * QK_PAD), BF16),
                     jax.ShapeDtypeStruct((t, MLA_WIDTH), BF16)]
    out_specs += [row(MLA_KV_RANK), row(MLA_ROPE)]
    out_shape += [jax.ShapeDtypeStruct((t, MLA_KV_RANK), F32), jax.ShapeDtypeStruct((t, MLA_ROPE), F32)]
    return pl.pallas_call(
        functools.partial(_mla_proj_kernel, absorbed=absorbed),
        grid=(t // tm,),
        in_specs=in_specs, out_specs=out_specs, out_shape=out_shape,
        compiler_params=_params(("arbitrary",)),
        name="mla_proj_absorbed" if absorbed else "mla_proj",
    )(proj, proj, proj, *tabs, gq, gkv, wq, wk, wv)


def _attn_kernel(q_ref, k_ref, v_ref, g_ref, o_ref, *, tq, tk):
    qi = pl.program_id(2)
    q = q_ref[...]
    q_pos = qi * tq + lax.broadcasted_iota(jnp.int32, (tq, tk), 0)
    k_off = lax.broadcasted_iota(jnp.int32, (tq, tk), 1)

    def body(j, carry):
        m, l, acc = carry
        k0 = pl.multiple_of(j * tk, tk)
        s = lax.dot_general(q, k_ref[pl.ds(k0, tk), :], (((1,), (1,)), ((), ())), preferred_element_type=F32)
        s = jnp.where(k0 + k_off <= q_pos, s, -jnp.inf)
        m_new = jnp.maximum(m, jnp.max(s, axis=-1, keepdims=True))
        alpha = jnp.exp(m - m_new)
        p = jnp.exp(s - m_new)
        l = alpha * l + jnp.sum(p, axis=-1, keepdims=True)
        acc = alpha * acc + jnp.dot(p.astype(BF16), v_ref[pl.ds(k0, tk), :], preferred_element_type=F32)
        return m_new, l, acc

    n_blocks = (qi * tq + tq + tk - 1) // tk
    init = (jnp.full((tq, 1), -jnp.inf, F32), jnp.zeros((tq, 1), F32), jnp.zeros((tq, MLA_DV), F32))
    _, l, acc = lax.fori_loop(0, n_blocks, body, init)
    o_ref[...] = _rms(acc / l, g_ref[...])


def _prompt_attention(q_cat, k_cat, v, g, bsz, seq, tq, tk):
    nq = seq // tq
    return pl.pallas_call(
        functools.partial(_attn_kernel, tq=tq, tk=tk),
        grid=(bsz, MLA_HEADS, nq),
        in_specs=[pl.BlockSpec((tq, QK_PAD), lambda b, h, i: (b * nq + i, h)),
                  pl.BlockSpec((seq, QK_PAD), lambda b, h, i: (b, h)),
                  pl.BlockSpec((seq, MLA_DV), lambda b, h, i: (b, h)),
                  pl.BlockSpec((1, MLA_DV), lambda b, h, i: (0, 0))],
        out_specs=pl.BlockSpec((tq, MLA_DV), lambda b, h, i: (b * nq + i, h)),
        out_shape=jax.ShapeDtypeStruct((bsz * seq, MLA_WIDTH), F32),
        compiler_params=_params(("arbitrary", "arbitrary", "arbitrary")),
        name="prompt_attention",
    )(q_cat, k_cat, v, g)


def _paged_attn_kernel(pt_ref, qlat_ref, qr_ref, cnew_ref, knew_ref, ckv_hbm, krt_hbm, o_ref,
                       ckv_buf, krt_buf, sems, m_scr, l_scr, acc_scr, *, pp, n_q):
    grp = pl.program_id(1)
    n_grp = pl.num_programs(1)
    step = pl.program_id(0) * n_grp + grp
    n_steps = pl.num_programs(0) * n_grp
    slot = step % 2
    rows = MLA_HEADS * n_q

    def page_copies(at_step, at_slot):
        out = []
        for j in range(pp):
            page = pt_ref[at_step * pp + j]
            out.append(pltpu.make_async_copy(ckv_hbm.at[0, page], ckv_buf.at[at_slot, j], sems.at[0, at_slot]))
            out.append(pltpu.make_async_copy(krt_hbm.at[0, page],
                                             krt_buf.at[at_slot, :, pl.ds(j * PAGE_SIZE, PAGE_SIZE)],
                                             sems.at[1, at_slot]))
        return out

    @pl.when(step == 0)
    def _():
        for cp in page_copies(step, slot):
            cp.start()

    @pl.when(step + 1 < n_steps)
    def _():
        for cp in page_copies(step + 1, 1 - slot):
            cp.start()

    @pl.when(grp == 0)
    def _():
        m_scr[...] = jnp.full(m_scr.shape, -jnp.inf, F32)
        l_scr[...] = jnp.zeros(l_scr.shape, F32)
        acc_scr[...] = jnp.zeros(acc_scr.shape, F32)

    q = qlat_ref[...].reshape(rows, MLA_KV_RANK).astype(BF16)
    qr = qr_ref[...].reshape(rows, LANES)[:, :MLA_ROPE].astype(BF16)

    def update(s, v):
        m = m_scr[...]
        m_new = jnp.maximum(m, jnp.max(s, axis=-1, keepdims=True))
        alpha = jnp.exp(m - m_new)
        p = jnp.exp(s - m_new)
        l_scr[...] = alpha * l_scr[...] + jnp.sum(p, axis=-1, keepdims=True)
        acc_scr[...] = acc_scr[...] * alpha + jnp.dot(p.astype(BF16), v, preferred_element_type=F32)
        m_scr[...] = m_new

    for cp in page_copies(step, slot):
        cp.wait()
    ckv = ckv_buf[slot].reshape(pp * PAGE_SIZE, MLA_KV_RANK).astype(BF16)
    s_past = (lax.dot_general(q, ckv, (((1,), (1,)), ((), ())), preferred_element_type=F32)
              + jnp.dot(qr, krt_buf[slot].astype(BF16), preferred_element_type=F32))
    update(s_past, ckv)

    @pl.when(grp == n_grp - 1)
    def _():
        pad = PAGE_SIZE - n_q
        c_new = jnp.concatenate([cnew_ref[...], jnp.zeros((pad, MLA_KV_RANK), F32)], axis=0).astype(BF16)
        k_new = jnp.concatenate([knew_ref[...], jnp.zeros((pad, MLA_ROPE), F32)], axis=0).astype(BF16)
        s = (lax.dot_general(q, c_new, (((1,), (1,)), ((), ())), preferred_element_type=F32)
             + lax.dot_general(qr, k_new, (((1,), (1,)), ((), ())), preferred_element_type=F32))
        q_idx = lax.broadcasted_iota(jnp.int32, s.shape, 0) % n_q
        k_idx = lax.broadcasted_iota(jnp.int32, s.shape, 1)
        update(jnp.where(k_idx <= q_idx, s, -jnp.inf), c_new)
        o_ref[...] = (acc_scr[...] / l_scr[...]).reshape(o_ref.shape)


def _paged_attention(q_lat, q_r, ckv_n, k_rope, cache_ckv, cache_krope_t, page_table, n_q, pp):
    bsz, n_pages = page_table.shape
    rows = MLA_HEADS * n_q
    grid_spec = pltpu.PrefetchScalarGridSpec(
        num_scalar_prefetch=1,
        grid=(bsz, n_pages // pp),
        in_specs=[pl.BlockSpec((MLA_HEADS, n_q, MLA_KV_RANK), lambda b, s, pt: (0, b, 0)),
                  pl.BlockSpec((MLA_HEADS, n_q, LANES), lambda b, s, pt: (0, b, 0)),
                  pl.BlockSpec((n_q, MLA_KV_RANK), lambda b, s, pt: (b, 0)),
                  pl.BlockSpec((n_q, MLA_ROPE), lambda b, s, pt: (b, 0)),
                  pl.BlockSpec(memory_space=pl.ANY),
                  pl.BlockSpec(memory_space=pl.ANY)],
        out_specs=pl.BlockSpec((MLA_HEADS, n_q, MLA_KV_RANK), lambda b, s, pt: (0, b, 0)),
        scratch_shapes=[pltpu.VMEM((2, pp, PAGE_SIZE, MLA_KV_RANK), F32),
                        pltpu.VMEM((2, MLA_ROPE, pp * PAGE_SIZE), F32),
                        pltpu.SemaphoreType.DMA((2, 2)),
                        pltpu.VMEM((rows, 1), F32), pltpu.VMEM((rows, 1), F32),
                        pltpu.VMEM((rows, MLA_KV_RANK), F32)],
    )
    return pl.pallas_call(
        functools.partial(_paged_attn_kernel, pp=pp, n_q=n_q),
        grid_spec=grid_spec,
        out_shape=jax.ShapeDtypeStruct((MLA_HEADS, bsz * n_q, MLA_KV_RANK), F32),
        compiler_params=_params(("arbitrary", "arbitrary")),
        name="paged_attention",
    )(page_table.reshape(-1), q_lat, q_r, ckv_n, k_rope, cache_ckv, cache_krope_t)


def _latent_out_kernel(o_ref, w_ref, g_ref, out_ref):
    o = jnp.dot(o_ref[...].astype(BF16), w_ref[...], preferred_element_type=F32)
    out_ref[...] = _rms(o, g_ref[...])


def _latent_out(o_lat, w_uv_h, g, tm):
    t = o_lat.shape[1]
    return pl.pallas_call(
        _latent_out_kernel,
        grid=(MLA_HEADS, t // tm),
        in_specs=[pl.BlockSpec((None, tm, MLA_KV_RANK), lambda h, i: (h, i, 0)),
                  pl.BlockSpec((None, MLA_KV_RANK, MLA_DV), lambda h, i: (h, 0, 0)),
                  pl.BlockSpec((1, MLA_DV), lambda h, i: (0, 0))],
        out_specs=pl.BlockSpec((tm, MLA_DV), lambda h, i: (i, h)),
        out_shape=jax.ShapeDtypeStruct((t, MLA_WIDTH), F32),
        compiler_params=_params(("arbitrary", "arbitrary")),
        name="latent_out",
    )(o_lat, w_uv_h, g)


def _outproj_kernel(og_ref, om_ref, wg_ref, wm_ref, x_ref, gt_ref, sc_ref, sh_ref, g_ref, x1_ref, h2_ref):
    y = (jnp.dot(og_ref[...].astype(BF16), wg_ref[...], preferred_element_type=F32)
         + jnp.dot(om_ref[...].astype(BF16), wm_ref[...], preferred_element_type=F32))
    x1 = x_ref[...] + gt_ref[...] * y.reshape(x_ref.shape)
    x1_ref[...] = x1
    h2 = _rms(x1, g_ref[...][None]) * (1.0 + sc_ref[...]) + sh_ref[...]
    h2_ref[...] = h2.reshape(h2_ref.shape)


def _outproj(o_gla, o_mla, w_top, w_bot, x, gt, sc, sh, g, bb, lt):
    bsz, seq, d = x.shape
    rows = bb * lt
    nl = seq // lt
    xs = pl.BlockSpec((bb, lt, d), lambda i: (i // nl, i % nl, 0))
    ms = pl.BlockSpec((bb, 1, d), lambda i: (i // nl, 0, 0))
    return pl.pallas_call(
        _outproj_kernel,
        grid=(bsz * seq // rows,),
        in_specs=[pl.BlockSpec((rows, GLA_WIDTH), lambda i: (i, 0)),
                  pl.BlockSpec((rows, MLA_WIDTH), lambda i: (i, 0)),
                  pl.BlockSpec(w_top.shape, lambda i: (0, 0)),
                  pl.BlockSpec(w_bot.shape, lambda i: (0, 0)),
                  xs, ms, ms, ms, pl.BlockSpec((1, d), lambda i: (0, 0))],
        out_specs=[xs, pl.BlockSpec((rows, d), lambda i: (i, 0))],
        out_shape=[jax.ShapeDtypeStruct(x.shape, F32), jax.ShapeDtypeStruct((bsz * seq, d), F32)],
        compiler_params=_params(("arbitrary",)),
        name="out_proj",
    )(o_gla, o_mla, w_top, w_bot, x, gt, sc, sh, g)


def _router_kernel(h_ref, w_ref, b_ref, id_ref, wt_ref):
    logit = jnp.dot(h_ref[...], w_ref[...], precision=HIGHEST, preferred_element_type=F32) + b_ref[...]
    lane = lax.broadcasted_iota(jnp.int32, logit.shape, 1)
    neg = -jnp.inf
    big = jnp.int32(LANES)

    def first_max(vals):
        top = jnp.max(vals, axis=-1, keepdims=True)
        return top, jnp.min(jnp.where(vals == top, lane, big), axis=-1, keepdims=True)

    g_vals = jnp.where(lane < N_GROUPS, logit, neg)
    g_max, g_idx = first_max(g_vals)
    g_w = 1.0 / jnp.sum(jnp.exp(g_vals - g_max), axis=-1, keepdims=True)
    e_lane = lane - N_GROUPS
    in_group = (e_lane >= g_idx * EXPERTS_PER_GROUP) & (e_lane < (g_idx + 1) * EXPERTS_PER_GROUP)
    e_vals = jnp.where(in_group, logit, neg)
    v1, i1 = first_max(e_vals)
    v2, i2 = first_max(jnp.where(lane == i1, neg, e_vals))
    e2 = jnp.exp(v2 - v1)
    w1 = g_w / (1.0 + e2)
    w2 = g_w * e2 / (1.0 + e2)
    id_ref[...] = jnp.where(lane == 0, i1 - N_GROUPS, jnp.where(lane == 1, i2 - N_GROUPS, 0))
    wt_ref[...] = jnp.where(lane == 0, w1, jnp.where(lane == 1, w2, 0.0))


def _router(h2, w_r, b_r, tm):
    t, d = h2.shape
    o = pl.BlockSpec((tm, LANES), lambda i: (i, 0))
    return pl.pallas_call(
        _router_kernel,
        grid=(t // tm,),
        in_specs=[pl.BlockSpec((tm, d), lambda i: (i, 0)),
                  pl.BlockSpec(w_r.shape, lambda i: (0, 0)),
                  pl.BlockSpec((1, LANES), lambda i: (0, 0))],
        out_specs=[o, o],
        out_shape=[jax.ShapeDtypeStruct((t, LANES), jnp.int32), jax.ShapeDtypeStruct((t, LANES), F32)],
        compiler_params=_params(("arbitrary",)),
        name="router",
    )(h2, w_r, b_r)


def _moe_kernel(ie_ref, ir_ref, in_ref, tok_ref, dst_ref, h_hbm, wgu_ref, wd_ref, y_hbm,
                x_scr, gu_scr, y_scr, gsem, ssem):
    item = pl.program_id(0)
    step = pl.program_id(1)
    n = in_ref[item]
    row0 = ir_ref[item]

    def row_copy(r):
        return pltpu.make_async_copy(h_hbm.at[pl.ds(tok_ref[row0 + r], 1)], x_scr.at[pl.ds(r, 1)], gsem)

    def out_copy(r):
        return pltpu.make_async_copy(y_scr.at[pl.ds(r, 1)], y_hbm.at[pl.ds(dst_ref[row0 + r], 1)], ssem)

    @pl.when((step == 0) & (n > 0))
    def _():
        x_scr[...] = jnp.zeros(x_scr.shape, F32)
        lax.fori_loop(0, n, lambda r, c: (row_copy(r).start(), c)[1], 0)
        lax.fori_loop(0, n, lambda r, c: (row_copy(r).wait(), c)[1], 0)

    for kk in range(MOE_KSTEPS):
        @pl.when((step == kk) & (n > 0))
        def _(kk=kk):
            xk = x_scr[:, kk * MOE_KCHUNK:(kk + 1) * MOE_KCHUNK].astype(BF16)
            part = jnp.dot(xk, wgu_ref[...].astype(BF16), preferred_element_type=F32)
            if kk == 0:
                gu_scr[...] = part
            else:
                gu_scr[...] += part

    @pl.when((step == MOE_KSTEPS) & (n > 0))
    def _():
        act = (_silu(gu_scr[:, :EXPERT_FF]) * gu_scr[:, EXPERT_FF:]).astype(BF16)
        y_scr[...] = jnp.dot(act, wd_ref[...].astype(BF16), preferred_element_type=F32)
        lax.fori_loop(0, n, lambda r, c: (out_copy(r).start(), c)[1], 0)
        lax.fori_loop(0, n, lambda r, c: (out_copy(r).wait(), c)[1], 0)


def _moe(items, tok_s, dst_s, h2, w_gu, w_down):
    item_e, item_row0, item_n = items
    n_items = item_e.shape[0]
    t, d = h2.shape
    last_k = MOE_KSTEPS - 1

    def gu_map(i, s, ie, ir, inn, tok, dst):
        return (ie[i], jnp.where(inn[i] > 0, jnp.minimum(s, last_k), last_k), 0)

    grid_spec = pltpu.PrefetchScalarGridSpec(
        num_scalar_prefetch=5,
        grid=(n_items, MOE_KSTEPS + 1),
        in_specs=[pl.BlockSpec(memory_space=pl.ANY),
                  pl.BlockSpec((None, MOE_KCHUNK, 2 * EXPERT_FF), gu_map),
                  pl.BlockSpec((None, EXPERT_FF, d), lambda i, s, ie, ir, inn, tok, dst: (ie[i], 0, 0))],
        out_specs=pl.BlockSpec(memory_space=pl.ANY),
        scratch_shapes=[pltpu.VMEM((MOE_ROWS, d), F32),
                        pltpu.VMEM((MOE_ROWS, 2 * EXPERT_FF), F32),
                        pltpu.VMEM((MOE_ROWS, d), F32),
                        pltpu.SemaphoreType.DMA(()),
                        pltpu.SemaphoreType.DMA(())],
    )
    return pl.pallas_call(
        _moe_kernel,
        grid_spec=grid_spec,
        out_shape=jax.ShapeDtypeStruct((t * TOP_K, d), F32),
        compiler_params=_params(("arbitrary", "arbitrary")),
        name="moe_experts",
    )(item_e, item_row0, item_n, tok_s, dst_s, h2, w_gu, w_down)


def _moe_plan(e_id):
    n_as = e_id.shape[0]
    n_items = N_EXPERTS + n_as // MOE_ROWS
    order = jnp.argsort(e_id).astype(jnp.int32)
    cnt = jnp.zeros((N_EXPERTS,), jnp.int32).at[e_id].add(1)
    ustart = jnp.cumsum(cnt) - cnt
    per_e = (cnt + MOE_ROWS - 1) // MOE_ROWS
    item_end = jnp.cumsum(per_e)
    total = item_end[-1]
    idx = jnp.arange(n_items, dtype=jnp.int32)
    live = idx < total
    e_of = jnp.searchsorted(item_end, jnp.minimum(idx, total - 1), side='right').astype(jnp.int32)
    local = jnp.minimum(idx, total - 1) - (item_end[e_of] - per_e[e_of])
    row0 = ustart[e_of] + local * MOE_ROWS
    n_rows = jnp.where(live, jnp.minimum(MOE_ROWS, cnt[e_of] - local * MOE_ROWS), 0)
    tok_s = order // TOP_K
    dst_s = (order % TOP_K) * (n_as // TOP_K) + tok_s
    return (e_of, row0.astype(jnp.int32), n_rows.astype(jnp.int32)), tok_s, dst_s


def _final_kernel(x_ref, gt_ref, y0_ref, y1_ref, wt_ref, g_ref, o_ref):
    wt = wt_ref[...]
    y = wt[:, 0:1] * y0_ref[...] + wt[:, 1:2] * y1_ref[...]
    x2 = x_ref[...] + gt_ref[...] * y.reshape(x_ref.shape)
    o_ref[...] = _rms(x2, g_ref[...][None])


def _final(x1, gt, y_rows, wt, g, bb, lt, row_block0):
    bsz, seq, d = x1.shape
    rows = bb * lt
    nl = seq // lt
    choice_blocks = y_rows.shape[0] // TOP_K // rows
    xs = pl.BlockSpec((bb, lt, d), lambda i: (i // nl, i % nl, 0))
    return pl.pallas_call(
        _final_kernel,
        grid=(bsz * seq // rows,),
        in_specs=[xs, pl.BlockSpec((bb, 1, d), lambda i: (i // nl, 0, 0)),
                  pl.BlockSpec((rows, d), lambda i: (row_block0 + i, 0)),
                  pl.BlockSpec((rows, d), lambda i: (choice_blocks + row_block0 + i, 0)),
                  pl.BlockSpec((rows, LANES), lambda i: (i, 0)),
                  pl.BlockSpec((1, d), lambda i: (0, 0))],
        out_specs=xs,
        out_shape=jax.ShapeDtypeStruct(x1.shape, F32),
        compiler_params=_params(("arbitrary",)),
        name="moe_combine_final_norm",
    )(x1, gt, y_rows, y_rows, wt, g)


def _rope_tables(pos, rows):
    half = MLA_ROPE // 2
    inv_freq = ROPE_BASE ** (-jnp.arange(half, dtype=F32) / half)
    ang = pos.astype(F32)[:, None] * inv_freq[None, :]
    cos, sin = jnp.cos(ang), jnp.sin(ang)
    zero = jnp.zeros_like(cos)
    pad = jnp.zeros((pos.shape[0], LANES - MLA_ROPE), F32)
    tabs = (jnp.concatenate([cos, cos, pad], axis=1),
            jnp.concatenate([-sin, zero, pad], axis=1),
            jnp.concatenate([zero, sin, pad], axis=1))
    reps = rows // pos.shape[0]
    return tuple(jnp.tile(t, (reps, 1)) for t in tabs)


def _arrange_w_in(w_in):
    d = w_in.shape[0]
    off_gv = 2 * GLA_QK
    off_ga = off_gv + GLA_WIDTH
    off_gr = off_ga + GLA_GATE_RANK
    off_cq = off_gr + GLA_WIDTH
    off_ckv = off_cq + MLA_Q_RANK
    off_kr = off_ckv + MLA_KV_RANK
    pad = jnp.zeros((d, LANES - MLA_ROPE - GLA_GATE_RANK), w_in.dtype)
    return jnp.concatenate([w_in[:, :off_ga], w_in[:, off_gr:off_kr + MLA_ROPE], w_in[:, off_ga:off_gr], pad],
                           axis=1).astype(BF16)


def _arrange_w_uq(w_uq):
    r = w_uq.shape[0]
    pad = jnp.zeros((r, MLA_HEADS, QK_PAD - MLA_NOPE - MLA_ROPE), w_uq.dtype)
    return jnp.concatenate([w_uq, pad], axis=2).reshape(r, MLA_HEADS * QK_PAD).astype(BF16)


def _mixer(x, mods, pos, s0, cache, w, prompt):
    bsz, seq, d = x.shape
    t = bsz * seq
    sh1, sc1, gt1, sh2, sc2, _ = mods
    bb, lt = (1, 512) if prompt else (512 // seq, seq)
    proj = _inproj(x, w['g_mix'], sc1, sh1, w['w_in'], bb, lt)
    gbb, glt = (1, 512) if prompt else (8, seq)
    o_gla, s_fin = _gla(proj, s0, w['wg_pad'], w['bg'], w['g_gla'], bsz, seq, gbb, glt)
    tm = 256
    if prompt:
        tabs = _rope_tables(pos, seq)
        q_cat, k_cat, v, ckv_n, k_rope = _mla_proj(proj, tabs, w['g_q'], w['g_kv'], w['w_uq'], w['w_uk'],
                                                    w['w_uv'], tm, absorbed=False)
        o_mla = _prompt_attention(q_cat, k_cat, v, w['g_mla'], bsz, seq, 512, 512)
    else:
        cache_ckv, cache_krope, page_table = cache
        tabs = _rope_tables(pos, tm)
        q_lat, q_r, ckv_n, k_rope = _mla_proj(proj, tabs, w['g_q'], w['g_kv'], w['w_uq'], w['w_uk_t'],
                                              w['w_uv'], tm, absorbed=True)
        o_lat = _paged_attention(q_lat, q_r, ckv_n, k_rope, cache_ckv, jnp.swapaxes(cache_krope, 2, 3),
                                 page_table, seq, 32)
        o_mla = _latent_out(o_lat, w['w_uv_h'], w['g_mla'], tm)
    obb, olt = (1, 256) if prompt else (256 // seq, seq)
    x1, h2 = _outproj(o_gla, o_mla, w['w_out_top'], w['w_out_bot'], x, gt1, sc2, sh2, w['g_ffn'], obb, olt)
    return x1, h2, ckv_n.reshape(bsz, seq, MLA_KV_RANK), k_rope.reshape(bsz, seq, MLA_ROPE), s_fin


def kernel(x_prompt, x_sample, c_prompt, c_sample, cache_ckv, cache_krope, state_gla, page_table, w_ada, b_ada, g_mix, g_ffn, w_in, gla_w_gate, gla_b_gate, gla_g_onorm, mla_g_qnorm, mla_w_uq, mla_g_kvnorm, mla_w_uk, mla_w_uv, mla_g_onorm, w_out, moe_w_group, moe_b_group, moe_w_expert, moe_b_expert, moe_w_gu, moe_w_down, g_final):
    depth = w_ada.shape[0]
    assert depth == 1
    bp, lp, d = x_prompt.shape
    bs, ls, _ = x_sample.shape
    n_past = page_table.shape[1] * PAGE_SIZE
    lyr = 0

    n_c = bp + bs
    c_rows = -(-n_c // 8) * 8
    c_all = jnp.concatenate([c_prompt, c_sample, jnp.zeros((c_rows - n_c, d), F32)], axis=0)
    mod = _ada(c_all, w_ada[lyr], b_ada[lyr][None])
    mods_p = tuple(m[:bp, None, :] for m in jnp.split(mod, 6, axis=-1))
    mods_s = tuple(m[bp:n_c, None, :] for m in jnp.split(mod, 6, axis=-1))

    wg_pad = jnp.zeros((LANES, GLA_QK), F32).at[GA_LANE:GA_LANE + GLA_GATE_RANK].set(gla_w_gate[lyr])
    w_uk = mla_w_uk[lyr]
    w_uv = mla_w_uv[lyr]
    w = {
        'g_mix': g_mix[lyr][None], 'g_ffn': g_ffn[lyr][None],
        'w_in': _arrange_w_in(w_in[lyr]),
        'wg_pad': wg_pad, 'bg': gla_b_gate[lyr][None], 'g_gla': gla_g_onorm[lyr][None],
        'g_q': mla_g_qnorm[lyr][None], 'g_kv': mla_g_kvnorm[lyr][None], 'g_mla': mla_g_onorm[lyr][None],
        'w_uq': _arrange_w_uq(mla_w_uq[lyr]),
        'w_uk': w_uk.reshape(MLA_KV_RANK, MLA_HEADS * MLA_NOPE).astype(BF16),
        'w_uk_t': w_uk.transpose(1, 2, 0).astype(BF16),
        'w_uv': w_uv.reshape(MLA_KV_RANK, MLA_WIDTH).astype(BF16),
        'w_uv_h': w_uv.transpose(1, 0, 2).astype(BF16),
        'w_out_top': w_out[lyr][:GLA_WIDTH].astype(BF16),
        'w_out_bot': w_out[lyr][GLA_WIDTH:].astype(BF16),
    }

    pos_p = jnp.arange(lp, dtype=jnp.int32)
    pos_s = n_past + jnp.arange(ls, dtype=jnp.int32)
    s0_p = jnp.zeros((bp, GLA_HEADS, GLA_DK, GLA_DV), F32)
    x1_p, h2_p, ckv_p, kr_p, st_p = _mixer(x_prompt, mods_p, pos_p, s0_p, None, w, prompt=True)
    x1_s, h2_s, ckv_s, kr_s, st_s = _mixer(x_sample, mods_s, pos_s, state_gla[lyr],
                                           (cache_ckv, cache_krope, page_table), w, prompt=False)

    w_r = jnp.concatenate([moe_w_group[lyr], moe_w_expert[lyr],
                           jnp.zeros((d, LANES - N_GROUPS - N_EXPERTS), F32)], axis=1)
    b_r = jnp.concatenate([moe_b_group[lyr], moe_b_expert[lyr],
                           jnp.zeros((LANES - N_GROUPS - N_EXPERTS,), F32)])[None]
    h2 = jnp.concatenate([h2_p, h2_s], axis=0)
    ids, wts = _router(h2, w_r, b_r, 512)
    items, tok_s, dst_s = _moe_plan(ids[:, :TOP_K].reshape(-1))
    y_rows = _moe(items, tok_s, dst_s, h2, moe_w_gu[lyr], moe_w_down[lyr])

    y_p = _final(x1_p, mods_p[5], y_rows, wts[:bp * lp], g_final[None], 1, 256, 0)
    y_s = _final(x1_s, mods_s[5], y_rows, wts[bp * lp:], g_final[None], 256 // ls, ls, bp * lp // 256)
    return (y_p, y_s, ckv_p[None], kr_p[None], st_p[None], ckv_s[None], kr_s[None], st_s[None])
```

```python
import functools
import math

import jax
import jax.numpy as jnp
from jax import lax
from jax.experimental import pallas as pl
from jax.experimental.pallas import tpu as pltpu

F32 = jnp.float32
BF16 = jnp.bfloat16
HIGHEST = lax.Precision.HIGHEST

D_MODEL = 2048
PAGE_SIZE = 128
GLA_HEADS = 4
GLA_DK = 128
GLA_DV = 256
GLA_GATE_RANK = 16
GLA_TAU = 16.0
GLA_CHUNK = 64
GLA_QK = GLA_HEADS * GLA_DK
GLA_WIDTH = GLA_HEADS * GLA_DV
MLA_HEADS = 8
MLA_Q_RANK = 512
MLA_KV_RANK = 512
MLA_NOPE = 128
MLA_ROPE = 64
MLA_DV = 128
MLA_WIDTH = MLA_HEADS * MLA_DV
ROPE_BASE = 10000.0
ATTN_SCALE = (MLA_NOPE + MLA_ROPE) ** -0.5
N_GROUPS = 8
EXPERTS_PER_GROUP = 8
N_EXPERTS = N_GROUPS * EXPERTS_PER_GROUP
TOP_K = 2
EXPERT_FF = 1408
EPS = 1e-6

LANES = 128
QK_PAD = 2 * LANES
COL_GQ = 0
COL_GK = GLA_QK
COL_GV = 2 * GLA_QK
COL_GR = COL_GV + GLA_WIDTH
COL_CQ = COL_GR + GLA_WIDTH
COL_CKV = COL_CQ + MLA_Q_RANK
COL_KR = COL_CKV + MLA_KV_RANK
IN_COLS_PAD = COL_KR + LANES
GA_LANE = MLA_ROPE
IN_COL_TILE = IN_COLS_PAD // 3

MXU_COLS = 256
MOE_ROWS = 768
MOE_SUB = 256
MOE_HALF = MXU_COLS
MOE_COLS = 2 * MXU_COLS
MOE_GU_CHUNKS = -(-EXPERT_FF // MOE_HALF)
MOE_DN_CHUNKS = D_MODEL // MOE_COLS
MOE_CHUNKS = MOE_GU_CHUNKS + MOE_DN_CHUNKS
MOE_RING = 4
VMEM_LIMIT = 56 * 1024 * 1024


def _params(sem, vmem=VMEM_LIMIT):
    return pltpu.CompilerParams(dimension_semantics=sem, vmem_limit_bytes=vmem)


def _silu(x):
    return x * jax.nn.sigmoid(x)


def _rms(x, g):
    return x * lax.rsqrt(jnp.mean(x * x, axis=-1, keepdims=True) + EPS) * g


def _ada_kernel(c_ref, w_ref, b_ref, o_ref):
    a = _silu(c_ref[...]).astype(BF16)
    o_ref[...] = jnp.dot(a, w_ref[...].astype(BF16), preferred_element_type=F32) + b_ref[...]


def _ada(c, w, b):
    rows, d = c.shape
    n = w.shape[1]
    tn = 1024
    return pl.pallas_call(
        _ada_kernel,
        grid=(n // tn,),
        in_specs=[pl.BlockSpec((rows, d), lambda j: (0, 0)),
                  pl.BlockSpec((d, tn), lambda j: (0, j)),
                  pl.BlockSpec((1, tn), lambda j: (0, j))],
        out_specs=pl.BlockSpec((rows, tn), lambda j: (0, j)),
        out_shape=jax.ShapeDtypeStruct((rows, n), F32),
        compiler_params=_params(("arbitrary",)),
        name="ada_mod",
    )(c, w, b)


def _inproj_kernel(x_ref, g_ref, sc_ref, sh_ref, w_ref, o_ref, h_scr):
    @pl.when(pl.program_id(1) == 0)
    def _():
        h = _rms(x_ref[...], g_ref[...][None]) * (1.0 + sc_ref[...]) + sh_ref[...]
        h_scr[...] = h.reshape(h_scr.shape).astype(BF16)

    o_ref[...] = jnp.dot(h_scr[...], w_ref[...], preferred_element_type=F32)


def _inproj(x, g, sc, sh, w, bb, lt):
    bsz, seq, d = x.shape
    rows = bb * lt
    nl = seq // lt
    n = w.shape[1]
    return pl.pallas_call(
        _inproj_kernel,
        grid=(bsz * seq // rows, n // IN_COL_TILE),
        in_specs=[pl.BlockSpec((bb, lt, d), lambda i, j: (i // nl, i % nl, 0)),
                  pl.BlockSpec((1, d), lambda i, j: (0, 0)),
                  pl.BlockSpec((bb, 1, d), lambda i, j: (i // nl, 0, 0)),
                  pl.BlockSpec((bb, 1, d), lambda i, j: (i // nl, 0, 0)),
                  pl.BlockSpec((d, IN_COL_TILE), lambda i, j: (0, j))],
        out_specs=pl.BlockSpec((rows, IN_COL_TILE), lambda i, j: (i, j)),
        out_shape=jax.ShapeDtypeStruct((bsz * seq, n), F32),
        scratch_shapes=[pltpu.VMEM((rows, d), BF16)],
        compiler_params=_params(("arbitrary", "arbitrary")),
        name="in_proj",
    )(x, g, sc, sh, w)


def _split_bf16(x, pieces):
    out = []
    for _ in range(pieces):
        part = x.astype(BF16)
        out.append(part)
        x = x - part.astype(F32)
    return out


def _gla_kernel(q_ref, k_ref, v_ref, gr_ref, ga_ref, wg_ref, bg_ref, gon_ref, tri_b_ref, tri_f_ref, s0_ref,
                o_ref, sout_ref, s_scr, *, n_chunks, n_valid):
    c = GLA_CHUNK
    l_step = pl.program_id(2)

    @pl.when(l_step == 0)
    def _():
        s_scr[...] = s0_ref[...]

    chunks_per_batch = n_chunks // s_scr.shape[0]

    def load(ref):
        x = ref[...]
        if n_valid == c:
            return x
        zeros = jnp.zeros((c - n_valid, x.shape[1]), x.dtype)
        return jnp.concatenate([part for i in range(n_chunks)
                                for part in (x[i * n_valid:(i + 1) * n_valid], zeros)], axis=0)

    q = load(q_ref) * (GLA_DK ** -0.5)
    k = load(k_ref)
    v = load(v_ref).astype(BF16)
    a_hi, a_lo = _split_bf16(load(ga_ref), 2)
    w_hi, w_lo = _split_bf16(wg_ref[...], 2)
    z = jnp.dot(jnp.concatenate([a_hi, a_lo, a_hi], axis=1), jnp.concatenate([w_hi, w_hi, w_lo], axis=0),
                preferred_element_type=F32) + bg_ref[...]
    log_a = (jnp.minimum(z, 0.0) - jnp.log1p(jnp.exp(-jnp.abs(z)))) / GLA_TAU
    if n_valid < c:
        log_a = jnp.where(lax.broadcasted_iota(jnp.int32, log_a.shape, 0) % c < n_valid, log_a, 0.0)
    cum = jnp.dot(tri_b_ref[...], jnp.concatenate(_split_bf16(log_a, 3), axis=1), preferred_element_type=F32)
    b = cum[:, :GLA_DK] + cum[:, GLA_DK:2 * GLA_DK] + cum[:, 2 * GLA_DK:]
    q_dec = (q * jnp.exp(b)).astype(BF16)
    k_inv = (k * jnp.exp(-b)).astype(BF16)
    att = lax.dot_general(q_dec, k_inv, (((1,), (1,)), ((), ())), preferred_element_type=F32)
    att = jnp.where(tri_f_ref[...] > 0.0, att, 0.0).astype(BF16)
    o_intra = jnp.dot(att, v, preferred_element_type=F32)

    states = {}
    outs = []
    for idx in range(n_chunks):
        bi = idx // chunks_per_batch
        rows = slice(idx * c, (idx + 1) * c)
        s_old = states[bi] if bi in states else s_scr[bi]
        b_last = b[(idx + 1) * c - 1:(idx + 1) * c, :]
        o = o_intra[rows] + jnp.dot(q_dec[rows], s_old.astype(BF16), preferred_element_type=F32)
        k_dec = (k[rows] * jnp.exp(b_last - b[rows])).astype(BF16)
        kv = lax.dot_general(k_dec, v[rows], (((0,), (0,)), ((), ())), preferred_element_type=F32)
        decay = jnp.exp(jnp.broadcast_to(b_last, (GLA_DK, GLA_DK)).T)
        states[bi] = jnp.concatenate([decay] * (GLA_DV // GLA_DK), axis=1) * s_old + kv
        outs.append(o[:n_valid])
    for bi, s_new in states.items():
        s_scr[bi] = s_new
    o_all = _rms(jnp.concatenate(outs, axis=0), gon_ref[...])
    o_ref[...] = o_all * _silu(gr_ref[...])

    @pl.when(l_step == pl.num_programs(2) - 1)
    def _():
        sout_ref[...] = s_scr[...]


def _gla(proj, s0, wg_pad, bg, gon, bsz, seq, bb, lt):
    n_valid = math.gcd(seq, GLA_CHUNK)
    rows = bb * lt
    nl = seq // lt
    rmap = lambda b, h, l: b * nl + l
    n_chunks = rows // n_valid
    kern = functools.partial(_gla_kernel, n_chunks=n_chunks, n_valid=n_valid)
    st_spec = pl.BlockSpec((bb, None, GLA_DK, GLA_DV), lambda b, h, l: (b, h, 0, 0))
    pos = jnp.arange(n_chunks * GLA_CHUNK)
    tri = (pos[:, None] // GLA_CHUNK == pos[None, :] // GLA_CHUNK) & (pos[None, :] <= pos[:, None])
    tri_spec = pl.BlockSpec(tri.shape, lambda b, h, l: (0, 0))
    return pl.pallas_call(
        kern,
        grid=(bsz // bb, GLA_HEADS, nl),
        in_specs=[pl.BlockSpec((rows, GLA_DK), lambda b, h, l: (rmap(b, h, l), COL_GQ // GLA_DK + h)),
                  pl.BlockSpec((rows, GLA_DK), lambda b, h, l: (rmap(b, h, l), COL_GK // GLA_DK + h)),
                  pl.BlockSpec((rows, GLA_DV), lambda b, h, l: (rmap(b, h, l), COL_GV // GLA_DV + h)),
                  pl.BlockSpec((rows, GLA_DV), lambda b, h, l: (rmap(b, h, l), COL_GR // GLA_DV + h)),
                  pl.BlockSpec((rows, LANES), lambda b, h, l: (rmap(b, h, l), COL_KR // LANES)),
                  pl.BlockSpec((LANES, GLA_DK), lambda b, h, l: (0, h)),
                  pl.BlockSpec((1, GLA_DK), lambda b, h, l: (0, h)),
                  pl.BlockSpec((1, GLA_DV), lambda b, h, l: (0, 0)),
                  tri_spec, tri_spec, st_spec],
        out_specs=[pl.BlockSpec((rows, GLA_DV), lambda b, h, l: (rmap(b, h, l), h)), st_spec],
        out_shape=[jax.ShapeDtypeStruct((bsz * seq, GLA_WIDTH), F32),
                   jax.ShapeDtypeStruct((bsz, GLA_HEADS, GLA_DK, GLA_DV), F32)],
        scratch_shapes=[pltpu.VMEM((bb, GLA_DK, GLA_DV), F32)],
        compiler_params=_params(("arbitrary", "arbitrary", "arbitrary")),
        name="gla",
    )(proj, proj, proj, proj, proj, wg_pad, bg, gon, tri.astype(BF16), tri.astype(F32), s0)


def _rope(r, cos, sin_lo, sin_hi):
    return (r * cos + pltpu.roll(r, LANES - MLA_ROPE // 2, axis=1) * sin_lo
            + pltpu.roll(r, MLA_ROPE // 2, axis=1) * sin_hi)


def _mla_proj_kernel(cq_ref, ckv_ref, kr_ref, cos_ref, slo_ref, shi_ref, gq_ref, gkv_ref, wq_ref,
                     wk_ref, wv_ref, *out_refs, absorbed):
    cos, slo, shi = cos_ref[...], slo_ref[...], shi_ref[...]
    cqn = _rms(cq_ref[...], gq_ref[...]).astype(BF16)
    qf = jnp.dot(cqn, wq_ref[...], preferred_element_type=F32)
    ckv_n = _rms(ckv_ref[...], gkv_ref[...])
    k_rot = _rope(kr_ref[...], cos, slo, shi)
    if absorbed:
        qlat_ref, qr_ref, ckvn_ref, krope_ref = out_refs
    else:
        qcat_ref, kcat_ref, v_ref, ckvn_ref, krope_ref = out_refs
    ckvn_ref[...] = ckv_n
    krope_ref[...] = k_rot[:, :MLA_ROPE]
    for h in range(MLA_HEADS):
        q_nope = qf[:, h * QK_PAD:h * QK_PAD + LANES] * ATTN_SCALE
        q_rot = _rope(qf[:, h * QK_PAD + LANES:(h + 1) * QK_PAD], cos, slo, shi) * ATTN_SCALE
        if absorbed:
            qlat_ref[h] = jnp.dot(q_nope.astype(BF16), wk_ref[h], preferred_element_type=F32)
            qr_ref[h] = q_rot
        else:
            qcat_ref[:, h * QK_PAD:h * QK_PAD + LANES] = q_nope.astype(BF16)
            qcat_ref[:, h * QK_PAD + LANES:(h + 1) * QK_PAD] = q_rot.astype(BF16)
    if not absorbed:
        ckv_b = ckv_n.astype(BF16)
        k_nope = jnp.dot(ckv_b, wk_ref[...], preferred_element_type=F32)
        k_rot_b = k_rot.astype(BF16)
        for h in range(MLA_HEADS):
            kcat_ref[:, h * QK_PAD:h * QK_PAD + LANES] = k_nope[:, h * LANES:(h + 1) * LANES].astype(BF16)
            kcat_ref[:, h * QK_PAD + LANES:(h + 1) * QK_PAD] = k_rot_b
        v_ref[...] = jnp.dot(ckv_b, wv_ref[...], preferred_element_type=F32).astype(BF16)


def _mla_proj(proj, tabs, gq, gkv, wq, wk, wv, tm, absorbed):
    t = proj.shape[0]
    ntab = tabs[0].shape[0] // tm
    row = lambda c: pl.BlockSpec((tm, c), lambda i: (i, 0))
    full = lambda a: pl.BlockSpec(a.shape, lambda i: (0,) * a.ndim)
    tab = pl.BlockSpec((tm, LANES), lambda i: (i % ntab, 0))
    in_specs = [pl.BlockSpec((tm, MLA_Q_RANK), lambda i: (i, COL_CQ // MLA_Q_RANK)),
                pl.BlockSpec((tm, MLA_KV_RANK), lambda i: (i, COL_CKV // MLA_KV_RANK)),
                pl.BlockSpec((tm, LANES), lambda i: (i, COL_KR // LANES)),
                tab, tab, tab, full(gq), full(gkv), full(wq), full(wk), full(wv)]
    if absorbed:
        out_specs = [pl.BlockSpec((MLA_HEADS, tm, MLA_KV_RANK), lambda i: (0, i, 0)),
                     pl.BlockSpec((MLA_HEADS, tm, LANES), lambda i: (0, i, 0))]
        out_shape = [jax.ShapeDtypeStruct((MLA_HEADS, t, MLA_KV_RANK), F32),
                     jax.ShapeDtypeStruct((MLA_HEADS, t, LANES), F32)]
    else:
        out_specs = [row(MLA_HEADS * QK_PAD), row(MLA_HEADS * QK_PAD), row(MLA_WIDTH)]
        out_shape = [jax.ShapeDtypeStruct((t, MLA_HEADS * QK_PAD), BF16),
                     jax.ShapeDtypeStruct((t, MLA_HEADS * QK_PAD), BF16),
                     jax.ShapeDtypeStruct((t, MLA_WIDTH), BF16)]
    out_specs += [row(MLA_KV_RANK), row(MLA_ROPE)]
    out_shape += [jax.ShapeDtypeStruct((t, MLA_KV_RANK), F32), jax.ShapeDtypeStruct((t, MLA_ROPE), F32)]
    return pl.pallas_call(
        functools.partial(_mla_proj_kernel, absorbed=absorbed),
        grid=(t // tm,),
        in_specs=in_specs, out_specs=out_specs, out_shape=out_shape,
        compiler_params=_params(("arbitrary",)),
        name="mla_proj_absorbed" if absorbed else "mla_proj",
    )(proj, proj, proj, *tabs, gq, gkv, wq, wk, wv)


def _attn_kernel(q_ref, k_ref, v_ref, g_ref, o_ref, *, tq, tk):
    assert tq == tk
    qi = pl.program_id(2)
    q = q_ref[...]

    def block(j, carry, on_diagonal):
        m, l, acc = carry
        k0 = pl.multiple_of(j * tk, tk)
        s = lax.dot_general(q, k_ref[pl.ds(k0, tk), :], (((1,), (1,)), ((), ())), preferred_element_type=F32)
        if on_diagonal:
            row = lax.broadcasted_iota(jnp.int32, (tq, tk), 0)
            col = lax.broadcasted_iota(jnp.int32, (tq, tk), 1)
            s = jnp.where(col <= row, s, -jnp.inf)
        m_new = jnp.maximum(m, jnp.max(s, axis=-1, keepdims=True))
        alpha = jnp.exp(m - m_new)
        p = jnp.exp(s - m_new)
        l = alpha * l + jnp.sum(p, axis=-1, keepdims=True)
        acc = alpha * acc + jnp.dot(p.astype(BF16), v_ref[pl.ds(k0, tk), :], preferred_element_type=F32)
        return m_new, l, acc

    init = (jnp.full((tq, 1), -jnp.inf, F32), jnp.zeros((tq, 1), F32), jnp.zeros((tq, MLA_DV), F32))
    carry = lax.fori_loop(0, qi, lambda j, c: block(j, c, False), init)
    _, l, acc = block(qi, carry, True)
    o_ref[...] = _rms(acc / l, g_ref[...])


def _prompt_attention(q_cat, k_cat, v, g, bsz, seq, tq, tk):
    nq = seq // tq
    return pl.pallas_call(
        functools.partial(_attn_kernel, tq=tq, tk=tk),
        grid=(bsz, MLA_HEADS, nq),
        in_specs=[pl.BlockSpec((tq, QK_PAD), lambda b, h, i: (b * nq + i, h)),
                  pl.BlockSpec((seq, QK_PAD), lambda b, h, i: (b, h)),
                  pl.BlockSpec((seq, MLA_DV), lambda b, h, i: (b, h)),
                  pl.BlockSpec((1, MLA_DV), lambda b, h, i: (0, 0))],
        out_specs=pl.BlockSpec((tq, MLA_DV), lambda b, h, i: (b * nq + i, h)),
        out_shape=jax.ShapeDtypeStruct((bsz * seq, MLA_WIDTH), F32),
        compiler_params=_params(("arbitrary", "arbitrary", "arbitrary")),
        name="prompt_attention",
    )(q_cat, k_cat, v, g)


def _paged_attn_kernel(pt_ref, qlat_ref, qr_ref, cnew_ref, knew_ref, ckv_hbm, krt_hbm, o_ref,
                       ckv_buf, krt_buf, sems, m_scr, l_scr, acc_scr, *, pp, n_q):
    grp = pl.program_id(1)
    n_grp = pl.num_programs(1)
    step = pl.program_id(0) * n_grp + grp
    n_steps = pl.num_programs(0) * n_grp
    slot = step % 2
    rows = MLA_HEADS * n_q

    def page_copies(at_step, at_slot):
        out = []
        for j in range(pp):
            page = pt_ref[at_step * pp + j]
            out.append(pltpu.make_async_copy(ckv_hbm.at[0, page], ckv_buf.at[at_slot, j], sems.at[0, at_slot]))
            out.append(pltpu.make_async_copy(krt_hbm.at[0, page],
                                             krt_buf.at[at_slot, :, pl.ds(j * PAGE_SIZE, PAGE_SIZE)],
                                             sems.at[1, at_slot]))
        return out

    @pl.when(step == 0)
    def _():
        for cp in page_copies(step, slot):
            cp.start()

    @pl.when(step + 1 < n_steps)
    def _():
        for cp in page_copies(step + 1, 1 - slot):
            cp.start()

    @pl.when(grp == 0)
    def _():
        m_scr[...] = jnp.full(m_scr.shape, -jnp.inf, F32)
        l_scr[...] = jnp.zeros(l_scr.shape, F32)
        acc_scr[...] = jnp.zeros(acc_scr.shape, F32)

    q = qlat_ref[...].reshape(rows, MLA_KV_RANK).astype(BF16)
    qr = qr_ref[...].reshape(rows, LANES)[:, :MLA_ROPE].astype(BF16)

    def update(s, v):
        m = m_scr[...]
        m_new = jnp.maximum(m, jnp.max(s, axis=-1, keepdims=True))
        alpha = jnp.exp(m - m_new)
        p = jnp.exp(s - m_new)
        l_scr[...] = alpha * l_scr[...] + jnp.sum(p, axis=-1, keepdims=True)
        acc_scr[...] = acc_scr[...] * alpha + jnp.dot(p.astype(BF16), v, preferred_element_type=F32)
        m_scr[...] = m_new

    for cp in page_copies(step, slot):
        cp.wait()
    ckv = ckv_buf[slot].reshape(pp * PAGE_SIZE, MLA_KV_RANK).astype(BF16)
    s_past = (lax.dot_general(q, ckv, (((1,), (1,)), ((), ())), preferred_element_type=F32)
              + jnp.dot(qr, krt_buf[slot].astype(BF16), preferred_element_type=F32))
    update(s_past, ckv)

    @pl.when(grp == n_grp - 1)
    def _():
        pad = PAGE_SIZE - n_q
        c_new = jnp.concatenate([cnew_ref[...], jnp.zeros((pad, MLA_KV_RANK), F32)], axis=0).astype(BF16)
        k_new = jnp.concatenate([knew_ref[...], jnp.zeros((pad, MLA_ROPE), F32)], axis=0).astype(BF16)
        s = (lax.dot_general(q, c_new, (((1,), (1,)), ((), ())), preferred_element_type=F32)
             + lax.dot_general(qr, k_new, (((1,), (1,)), ((), ())), preferred_element_type=F32))
        q_idx = lax.broadcasted_iota(jnp.int32, s.shape, 0) % n_q
        k_idx = lax.broadcasted_iota(jnp.int32, s.shape, 1)
        update(jnp.where(k_idx <= q_idx, s, -jnp.inf), c_new)
        o_ref[...] = (acc_scr[...] / l_scr[...]).reshape(o_ref.shape)


def _paged_attention(q_lat, q_r, ckv_n, k_rope, cache_ckv, cache_krope_t, page_table, n_q, pp):
    bsz, n_pages = page_table.shape
    rows = MLA_HEADS * n_q
    grid_spec = pltpu.PrefetchScalarGridSpec(
        num_scalar_prefetch=1,
        grid=(bsz, n_pages // pp),
        in_specs=[pl.BlockSpec((MLA_HEADS, n_q, MLA_KV_RANK), lambda b, s, pt: (0, b, 0)),
                  pl.BlockSpec((MLA_HEADS, n_q, LANES), lambda b, s, pt: (0, b, 0)),
                  pl.BlockSpec((n_q, MLA_KV_RANK), lambda b, s, pt: (b, 0)),
                  pl.BlockSpec((n_q, MLA_ROPE), lambda b, s, pt: (b, 0)),
                  pl.BlockSpec(memory_space=pl.ANY),
                  pl.BlockSpec(memory_space=pl.ANY)],
        out_specs=pl.BlockSpec((MLA_HEADS, n_q, MLA_KV_RANK), lambda b, s, pt: (0, b, 0)),
        scratch_shapes=[pltpu.VMEM((2, pp, PAGE_SIZE, MLA_KV_RANK), F32),
                        pltpu.VMEM((2, MLA_ROPE, pp * PAGE_SIZE), F32),
                        pltpu.SemaphoreType.DMA((2, 2)),
                        pltpu.VMEM((rows, 1), F32), pltpu.VMEM((rows, 1), F32),
                        pltpu.VMEM((rows, MLA_KV_RANK), F32)],
    )
    return pl.pallas_call(
        functools.partial(_paged_attn_kernel, pp=pp, n_q=n_q),
        grid_spec=grid_spec,
        out_shape=jax.ShapeDtypeStruct((MLA_HEADS, bsz * n_q, MLA_KV_RANK), F32),
        compiler_params=_params(("arbitrary", "arbitrary")),
        name="paged_attention",
    )(page_table.reshape(-1), q_lat, q_r, ckv_n, k_rope, cache_ckv, cache_krope_t)


def _latent_out_kernel(o_ref, w_ref, g_ref, out_ref):
    o = jnp.dot(o_ref[...].astype(BF16), w_ref[...], preferred_element_type=F32)
    out_ref[...] = _rms(o, g_ref[...])


def _latent_out(o_lat, w_uv_h, g, tm):
    t = o_lat.shape[1]
    return pl.pallas_call(
        _latent_out_kernel,
        grid=(MLA_HEADS, t // tm),
        in_specs=[pl.BlockSpec((None, tm, MLA_KV_RANK), lambda h, i: (h, i, 0)),
                  pl.BlockSpec((None, MLA_KV_RANK, MLA_DV), lambda h, i: (h, 0, 0)),
                  pl.BlockSpec((1, MLA_DV), lambda h, i: (0, 0))],
        out_specs=pl.BlockSpec((tm, MLA_DV), lambda h, i: (i, h)),
        out_shape=jax.ShapeDtypeStruct((t, MLA_WIDTH), F32),
        compiler_params=_params(("arbitrary", "arbitrary")),
        name="latent_out",
    )(o_lat, w_uv_h, g)


def _outproj_kernel(og_ref, om_ref, wg_ref, wm_ref, x_ref, gt_ref, sc_ref, sh_ref, g_ref, x1_ref, h2_ref):
    y = (jnp.dot(og_ref[...].astype(BF16), wg_ref[...], preferred_element_type=F32)
         + jnp.dot(om_ref[...].astype(BF16), wm_ref[...], preferred_element_type=F32))
    x1 = x_ref[...] + gt_ref[...] * y.reshape(x_ref.shape)
    x1_ref[...] = x1
    h2 = _rms(x1, g_ref[...][None]) * (1.0 + sc_ref[...]) + sh_ref[...]
    h2_ref[...] = h2.reshape(h2_ref.shape)


def _outproj(o_gla, o_mla, w_top, w_bot, x, gt, sc, sh, g, bb, lt):
    bsz, seq, d = x.shape
    rows = bb * lt
    nl = seq // lt
    xs = pl.BlockSpec((bb, lt, d), lambda i: (i // nl, i % nl, 0))
    ms = pl.BlockSpec((bb, 1, d), lambda i: (i // nl, 0, 0))
    return pl.pallas_call(
        _outproj_kernel,
        grid=(bsz * seq // rows,),
        in_specs=[pl.BlockSpec((rows, GLA_WIDTH), lambda i: (i, 0)),
                  pl.BlockSpec((rows, MLA_WIDTH), lambda i: (i, 0)),
                  pl.BlockSpec(w_top.shape, lambda i: (0, 0)),
                  pl.BlockSpec(w_bot.shape, lambda i: (0, 0)),
                  xs, ms, ms, ms, pl.BlockSpec((1, d), lambda i: (0, 0))],
        out_specs=[xs, pl.BlockSpec((rows, d), lambda i: (i, 0))],
        out_shape=[jax.ShapeDtypeStruct(x.shape, F32), jax.ShapeDtypeStruct((bsz * seq, d), F32)],
        compiler_params=_params(("arbitrary",)),
        name="out_proj",
    )(o_gla, o_mla, w_top, w_bot, x, gt, sc, sh, g)


def _router_kernel(h_ref, w_ref, b_ref, id_ref, wt_ref):
    logit = jnp.dot(h_ref[...], w_ref[...], precision=HIGHEST, preferred_element_type=F32) + b_ref[...]
    lane = lax.broadcasted_iota(jnp.int32, logit.shape, 1)
    neg = -jnp.inf
    big = jnp.int32(LANES)

    def first_max(vals):
        top = jnp.max(vals, axis=-1, keepdims=True)
        return top, jnp.min(jnp.where(vals == top, lane, big), axis=-1, keepdims=True)

    g_vals = jnp.where(lane < N_GROUPS, logit, neg)
    g_max, g_idx = first_max(g_vals)
    g_w = 1.0 / jnp.sum(jnp.exp(g_vals - g_max), axis=-1, keepdims=True)
    e_lane = lane - N_GROUPS
    in_group = (e_lane >= g_idx * EXPERTS_PER_GROUP) & (e_lane < (g_idx + 1) * EXPERTS_PER_GROUP)
    e_vals = jnp.where(in_group, logit, neg)
    v1, i1 = first_max(e_vals)
    v2, i2 = first_max(jnp.where(lane == i1, neg, e_vals))
    e2 = jnp.exp(v2 - v1)
    w1 = g_w / (1.0 + e2)
    w2 = g_w * e2 / (1.0 + e2)
    id_ref[...] = jnp.where(lane == 0, i1 - N_GROUPS, jnp.where(lane == 1, i2 - N_GROUPS, 0))
    wt_ref[...] = jnp.where(lane == 0, w1, jnp.where(lane == 1, w2, 0.0))


def _router(h2, w_r, b_r, tm):
    t, d = h2.shape
    o = pl.BlockSpec((tm, LANES), lambda i: (i, 0))
    return pl.pallas_call(
        _router_kernel,
        grid=(t // tm,),
        in_specs=[pl.BlockSpec((tm, d), lambda i: (i, 0)),
                  pl.BlockSpec(w_r.shape, lambda i: (0, 0)),
                  pl.BlockSpec((1, LANES), lambda i: (0, 0))],
        out_specs=[o, o],
        out_shape=[jax.ShapeDtypeStruct((t, LANES), jnp.int32), jax.ShapeDtypeStruct((t, LANES), F32)],
        compiler_params=_params(("arbitrary",)),
        name="router",
    )(h2, w_r, b_r)


def _moe_kernel(ie_ref, ir_ref, in_ref, im_ref, nl_ref, src_ref, dst_ref,
                hp_hbm, hs_hbm, wgu_hbm, wd_hbm, y_hbm,
                x_buf, xb_scr, wb_scr, act_scr, y_buf, ring, gsem, ssem, wsem):
    n_live = nl_ref[0]
    last_item = ie_ref.shape[0] - 1

    def chunk_width(j):
        return min(MOE_HALF, EXPERT_FF - j * MOE_HALF)

    def weight_copies(e, j, slot):
        sem = wsem.at[slot]
        if j < MOE_GU_CHUNKS:
            w = chunk_width(j)
            return [pltpu.make_async_copy(wgu_hbm.at[e, :, pl.ds(off + j * MOE_HALF, w)],
                                          ring.at[slot, :, pl.ds(half * w, w)], sem)
                    for half, off in enumerate((0, EXPERT_FF))]
        c = j - MOE_GU_CHUNKS
        return [pltpu.make_async_copy(wd_hbm.at[e, :, pl.ds(c * MOE_COLS, MOE_COLS)],
                                      ring.at[slot, pl.ds(0, EXPERT_FF)], sem)]

    def gather_copy(h_hbm, src_row, p, r, k=1):
        return pltpu.make_async_copy(h_hbm.at[pl.ds(src_row, k)], x_buf.at[p, pl.ds(r, k)], gsem.at[p])

    def scatter_copy(r, dst_row, k=1):
        return pltpu.make_async_copy(y_buf.at[pl.ds(r, k)], y_hbm.at[pl.ds(dst_row, k)], ssem)

    def for_rows(lo, hi, fn, unroll=4):
        groups = (hi - lo) // unroll

        def group(g, c):
            for u in range(unroll):
                fn(lo + g * unroll + u)
            return c

        def single(r, c):
            fn(r)
            return c

        lax.fori_loop(0, groups, group, 0)
        lax.fori_loop(lo + groups * unroll, hi, single, 0)

    def wait_rows(n, copy_of_rows):
        k = 1
        while k <= MOE_ROWS:
            @pl.when((n & k) != 0)
            def _(k=k):
                copy_of_rows(k).wait()
            k *= 2

    def start_gather(i, p):
        row0, n, m = ir_ref[i], in_ref[i], im_ref[i]
        for_rows(0, m, lambda r: gather_copy(hp_hbm, src_ref[row0 + r], p, r).start())
        for_rows(m, n, lambda r: gather_copy(hs_hbm, src_ref[row0 + r], p, r).start())

    x_buf[...] = jnp.zeros(x_buf.shape, F32)
    start_gather(0, 0)
    for j in range(MOE_RING):
        for cp in weight_copies(ie_ref[0], j, j):
            cp.start()

    def item_body(i, carry):
        p = i % 2
        e, row0, n = ie_ref[i], ir_ref[i], in_ref[i]
        n_sub = (n + MOE_SUB - 1) // MOE_SUB
        has_next = i + 1 < n_live
        e_next = ie_ref[jnp.minimum(i + 1, last_item)]
        base = (i * MOE_CHUNKS) % MOE_RING

        wait_rows(n, lambda k: gather_copy(hp_hbm, 0, p, 0, k))

        @pl.when(has_next)
        def _():
            start_gather(i + 1, 1 - p)

        def cast_rows(sb, c):
            r0 = pl.multiple_of(sb * MOE_SUB, MOE_SUB)
            xb_scr[pl.ds(r0, MOE_SUB), :] = x_buf[p, pl.ds(r0, MOE_SUB), :].astype(BF16)
            return c
        lax.fori_loop(0, n_sub, cast_rows, 0)

        for j in range(MOE_CHUNKS):
            slot = (base + j) % MOE_RING
            if j == MOE_GU_CHUNKS:
                @pl.when(i > 0)
                def _():
                    n_prev = in_ref[jnp.maximum(i - 1, 0)]
                    wait_rows(n_prev, lambda k: scatter_copy(0, 0, k))
            for cp in weight_copies(e, j, slot):
                cp.wait()
            if j < MOE_GU_CHUNKS:
                cols = 2 * chunk_width(j)
                wb_scr[:, :cols] = ring[slot, :, pl.ds(0, cols)].astype(BF16)
            else:
                wb_scr[:EXPERT_FF, :] = ring[slot, pl.ds(0, EXPERT_FF), :].astype(BF16)
            j_ahead = j + MOE_RING
            if j_ahead < MOE_CHUNKS:
                for cp in weight_copies(e, j_ahead, slot):
                    cp.start()
            else:
                @pl.when(has_next)
                def _(j_ahead=j_ahead, slot=slot):
                    for cp in weight_copies(e_next, j_ahead - MOE_CHUNKS, slot):
                        cp.start()

            if j < MOE_GU_CHUNKS:
                w = chunk_width(j)

                def gate_up(sb, c, j=j, w=w):
                    r0 = pl.multiple_of(sb * MOE_SUB, MOE_SUB)
                    res = jnp.dot(xb_scr[pl.ds(r0, MOE_SUB), :], wb_scr[:, :2 * w], preferred_element_type=F32)
                    act = _silu(res[:, :w]) * res[:, w:]
                    act_scr[pl.ds(r0, MOE_SUB), j * MOE_HALF:j * MOE_HALF + w] = act.astype(BF16)
                    return c
                lax.fori_loop(0, n_sub, gate_up, 0)
            else:
                col0 = (j - MOE_GU_CHUNKS) * MOE_COLS

                def down(sb, c, col0=col0):
                    r0 = pl.multiple_of(sb * MOE_SUB, MOE_SUB)
                    y_buf[pl.ds(r0, MOE_SUB), col0:col0 + MOE_COLS] = jnp.dot(
                        act_scr[pl.ds(r0, MOE_SUB), :], wb_scr[:EXPERT_FF, :], preferred_element_type=F32)
                    return c
                lax.fori_loop(0, n_sub, down, 0)

        for_rows(0, n, lambda r: scatter_copy(r, dst_ref[row0 + r]).start())
        return carry

    lax.fori_loop(0, n_live, item_body, 0)
    wait_rows(in_ref[jnp.maximum(n_live - 1, 0)], lambda k: scatter_copy(0, 0, k))


def _moe(plan, h2_p, h2_s, w_gu, w_down):
    d = h2_p.shape[1]
    t_all = h2_p.shape[0] + h2_s.shape[0]
    hbm = pl.BlockSpec(memory_space=pl.ANY)
    grid_spec = pltpu.PrefetchScalarGridSpec(
        num_scalar_prefetch=len(plan),
        grid=(1,),
        in_specs=[hbm, hbm, hbm, hbm],
        out_specs=hbm,
        scratch_shapes=[pltpu.VMEM((2, MOE_ROWS, d), F32),
                        pltpu.VMEM((MOE_ROWS, d), BF16),
                        pltpu.VMEM((d, MOE_COLS), BF16),
                        pltpu.VMEM((MOE_ROWS, EXPERT_FF), BF16),
                        pltpu.VMEM((MOE_ROWS, d), F32),
                        pltpu.VMEM((MOE_RING, d, MOE_COLS), F32),
                        pltpu.SemaphoreType.DMA((2,)),
                        pltpu.SemaphoreType.DMA(()),
                        pltpu.SemaphoreType.DMA((MOE_RING,))],
    )
    return pl.pallas_call(
        _moe_kernel,
        grid_spec=grid_spec,
        out_shape=jax.ShapeDtypeStruct((t_all * TOP_K, d), F32),
        compiler_params=_params(("arbitrary",)),
        name="moe_experts",
    )(*plan, h2_p, h2_s, w_gu, w_down)


def _moe_plan(e_id, t_prompt):
    n_as = e_id.shape[0]
    n_items = N_EXPERTS + n_as // MOE_ROWS
    order = jnp.argsort(e_id).astype(jnp.int32)
    cnt = jnp.zeros((N_EXPERTS,), jnp.int32).at[e_id].add(1)
    cnt_p = jnp.zeros((N_EXPERTS,), jnp.int32).at[e_id[:t_prompt * TOP_K]].add(1)
    ustart = jnp.cumsum(cnt) - cnt
    per_e = (cnt + MOE_ROWS - 1) // MOE_ROWS
    item_end = jnp.cumsum(per_e)
    total = item_end[-1]
    idx = jnp.arange(n_items, dtype=jnp.int32)
    live = idx < total
    e_of = jnp.searchsorted(item_end, jnp.minimum(idx, total - 1), side='right').astype(jnp.int32)
    local = jnp.minimum(idx, total - 1) - (item_end[e_of] - per_e[e_of])
    row0 = ustart[e_of] + local * MOE_ROWS
    n_rows = jnp.where(live, jnp.minimum(MOE_ROWS, cnt[e_of] - local * MOE_ROWS), 0)
    m_rows = jnp.clip(cnt_p[e_of] - local * MOE_ROWS, 0, n_rows)
    tok_s = order // TOP_K
    src_s = jnp.where(tok_s < t_prompt, tok_s, tok_s - t_prompt)
    dst_s = (order % TOP_K) * (n_as // TOP_K) + tok_s
    i32 = lambda a: a.astype(jnp.int32)
    return (i32(e_of), i32(row0), i32(n_rows), i32(m_rows), i32(total)[None], i32(src_s), i32(dst_s))


def _final_kernel(x_ref, gt_ref, y0_ref, y1_ref, wt_ref, g_ref, o_ref):
    wt = wt_ref[...]
    y = wt[:, 0:1] * y0_ref[...] + wt[:, 1:2] * y1_ref[...]
    x2 = x_ref[...] + gt_ref[...] * y.reshape(x_ref.shape)
    o_ref[...] = _rms(x2, g_ref[...][None])


def _final(x1, gt, y_rows, wt, g, bb, lt, row_block0):
    bsz, seq, d = x1.shape
    rows = bb * lt
    nl = seq // lt
    choice_blocks = y_rows.shape[0] // TOP_K // rows
    xs = pl.BlockSpec((bb, lt, d), lambda i: (i // nl, i % nl, 0))
    return pl.pallas_call(
        _final_kernel,
        grid=(bsz * seq // rows,),
        in_specs=[xs, pl.BlockSpec((bb, 1, d), lambda i: (i // nl, 0, 0)),
                  pl.BlockSpec((rows, d), lambda i: (row_block0 + i, 0)),
                  pl.BlockSpec((rows, d), lambda i: (choice_blocks + row_block0 + i, 0)),
                  pl.BlockSpec((rows, LANES), lambda i: (i, 0)),
                  pl.BlockSpec((1, d), lambda i: (0, 0))],
        out_specs=xs,
        out_shape=jax.ShapeDtypeStruct(x1.shape, F32),
        compiler_params=_params(("arbitrary",)),
        name="moe_combine_final_norm",
    )(x1, gt, y_rows, y_rows, wt, g)


def _rope_tables(pos, rows):
    half = MLA_ROPE // 2
    inv_freq = ROPE_BASE ** (-jnp.arange(half, dtype=F32) / half)
    ang = pos.astype(F32)[:, None] * inv_freq[None, :]
    cos, sin = jnp.cos(ang), jnp.sin(ang)
    zero = jnp.zeros_like(cos)
    pad = jnp.zeros((pos.shape[0], LANES - MLA_ROPE), F32)
    tabs = (jnp.concatenate([cos, cos, pad], axis=1),
            jnp.concatenate([-sin, zero, pad], axis=1),
            jnp.concatenate([zero, sin, pad], axis=1))
    reps = rows // pos.shape[0]
    return tuple(jnp.tile(t, (reps, 1)) for t in tabs)


def _arrange_w_in(w_in):
    d = w_in.shape[0]
    off_gv = 2 * GLA_QK
    off_ga = off_gv + GLA_WIDTH
    off_gr = off_ga + GLA_GATE_RANK
    off_cq = off_gr + GLA_WIDTH
    off_ckv = off_cq + MLA_Q_RANK
    off_kr = off_ckv + MLA_KV_RANK
    pad = jnp.zeros((d, LANES - MLA_ROPE - GLA_GATE_RANK), w_in.dtype)
    return jnp.concatenate([w_in[:, :off_ga], w_in[:, off_gr:off_kr + MLA_ROPE], w_in[:, off_ga:off_gr], pad],
                           axis=1).astype(BF16)


def _arrange_w_uq(w_uq):
    r = w_uq.shape[0]
    pad = jnp.zeros((r, MLA_HEADS, QK_PAD - MLA_NOPE - MLA_ROPE), w_uq.dtype)
    return jnp.concatenate([w_uq, pad], axis=2).reshape(r, MLA_HEADS * QK_PAD).astype(BF16)


def _mixer(x, mods, pos, s0, cache, w, prompt):
    bsz, seq, d = x.shape
    t = bsz * seq
    sh1, sc1, gt1, sh2, sc2, _ = mods
    bb, lt = (1, 512) if prompt else (512 // seq, seq)
    proj = _inproj(x, w['g_mix'], sc1, sh1, w['w_in'], bb, lt)
    gbb, glt = (1, 512) if prompt else (8, seq)
    o_gla, s_fin = _gla(proj, s0, w['wg_pad'], w['bg'], w['g_gla'], bsz, seq, gbb, glt)
    tm = 256
    if prompt:
        tabs = _rope_tables(pos, seq)
        q_cat, k_cat, v, ckv_n, k_rope = _mla_proj(proj, tabs, w['g_q'], w['g_kv'], w['w_uq'], w['w_uk'],
                                                    w['w_uv'], tm, absorbed=False)
        o_mla = _prompt_attention(q_cat, k_cat, v, w['g_mla'], bsz, seq, 512, 512)
    else:
        cache_ckv, cache_krope, page_table = cache
        tabs = _rope_tables(pos, tm)
        q_lat, q_r, ckv_n, k_rope = _mla_proj(proj, tabs, w['g_q'], w['g_kv'], w['w_uq'], w['w_uk_t'],
                                              w['w_uv'], tm, absorbed=True)
        o_lat = _paged_attention(q_lat, q_r, ckv_n, k_rope, cache_ckv, jnp.swapaxes(cache_krope, 2, 3),
                                 page_table, seq, 32)
        o_mla = _latent_out(o_lat, w['w_uv_h'], w['g_mla'], tm)
    obb, olt = (1, 256) if prompt else (256 // seq, seq)
    x1, h2 = _outproj(o_gla, o_mla, w['w_out_top'], w['w_out_bot'], x, gt1, sc2, sh2, w['g_ffn'], obb, olt)
    return x1, h2, ckv_n.reshape(bsz, seq, MLA_KV_RANK), k_rope.reshape(bsz, seq, MLA_ROPE), s_fin


def kernel(x_prompt, x_sample, c_prompt, c_sample, cache_ckv, cache_krope, state_gla, page_table, w_ada, b_ada, g_mix, g_ffn, w_in, gla_w_gate, gla_b_gate, gla_g_onorm, mla_g_qnorm, mla_w_uq, mla_g_kvnorm, mla_w_uk, mla_w_uv, mla_g_onorm, w_out, moe_w_group, moe_b_group, moe_w_expert, moe_b_expert, moe_w_gu, moe_w_down, g_final):
    depth = w_ada.shape[0]
    assert depth == 1
    bp, lp, d = x_prompt.shape
    bs, ls, _ = x_sample.shape
    n_past = page_table.shape[1] * PAGE_SIZE
    lyr = 0

    n_c = bp + bs
    c_rows = -(-n_c // 8) * 8
    c_all = jnp.concatenate([c_prompt, c_sample, jnp.zeros((c_rows - n_c, d), F32)], axis=0)
    mod = _ada(c_all, w_ada[lyr], b_ada[lyr][None])
    mods_p = tuple(m[:bp, None, :] for m in jnp.split(mod, 6, axis=-1))
    mods_s = tuple(m[bp:n_c, None, :] for m in jnp.split(mod, 6, axis=-1))

    wg_pad = jnp.zeros((LANES, GLA_QK), F32).at[GA_LANE:GA_LANE + GLA_GATE_RANK].set(gla_w_gate[lyr])
    w_uk = mla_w_uk[lyr]
    w_uv = mla_w_uv[lyr]
    w = {
        'g_mix': g_mix[lyr][None], 'g_ffn': g_ffn[lyr][None],
        'w_in': _arrange_w_in(w_in[lyr]),
        'wg_pad': wg_pad, 'bg': gla_b_gate[lyr][None], 'g_gla': gla_g_onorm[lyr][None],
        'g_q': mla_g_qnorm[lyr][None], 'g_kv': mla_g_kvnorm[lyr][None], 'g_mla': mla_g_onorm[lyr][None],
        'w_uq': _arrange_w_uq(mla_w_uq[lyr]),
        'w_uk': w_uk.reshape(MLA_KV_RANK, MLA_HEADS * MLA_NOPE).astype(BF16),
        'w_uk_t': w_uk.transpose(1, 2, 0).astype(BF16),
        'w_uv': w_uv.reshape(MLA_KV_RANK, MLA_WIDTH).astype(BF16),
        'w_uv_h': w_uv.transpose(1, 0, 2).astype(BF16),
        'w_out_top': w_out[lyr][:GLA_WIDTH].astype(BF16),
        'w_out_bot': w_out[lyr][GLA_WIDTH:].astype(BF16),
    }

    pos_p = jnp.arange(lp, dtype=jnp.int32)
    pos_s = n_past + jnp.arange(ls, dtype=jnp.int32)
    s0_p = jnp.zeros((bp, GLA_HEADS, GLA_DK, GLA_DV), F32)
    x1_p, h2_p, ckv_p, kr_p, st_p = _mixer(x_prompt, mods_p, pos_p, s0_p, None, w, prompt=True)
    x1_s, h2_s, ckv_s, kr_s, st_s = _mixer(x_sample, mods_s, pos_s, state_gla[lyr],
                                           (cache_ckv, cache_krope, page_table), w, prompt=False)

    w_r = jnp.concatenate([moe_w_group[lyr], moe_w_expert[lyr],
                           jnp.zeros((d, LANES - N_GROUPS - N_EXPERTS), F32)], axis=1)
    b_r = jnp.concatenate([moe_b_group[lyr], moe_b_expert[lyr],
                           jnp.zeros((LANES - N_GROUPS - N_EXPERTS,), F32)])[None]
    ids_p, wts_p = _router(h2_p, w_r, b_r, 512)
    ids_s, wts_s = _router(h2_s, w_r, b_r, 512)
    e_id = jnp.concatenate([ids_p[:, :TOP_K], ids_s[:, :TOP_K]], axis=0).reshape(-1)
    y_rows = _moe(_moe_plan(e_id, bp * lp), h2_p, h2_s, moe_w_gu[lyr], moe_w_down[lyr])

    y_p = _final(x1_p, mods_p[5], y_rows, wts_p, g_final[None], 1, 256, 0)
    y_s = _final(x1_s, mods_s[5], y_rows, wts_s, g_final[None], 256 // ls, ls, bp * lp // 256)
    return (y_p, y_s, ckv_p[None], kr_p[None], st_p[None], ckv_s[None], kr_s[None], st_s[None])
```

```python
import functools
import math

import jax
import jax.numpy as jnp
from jax import lax
from jax.experimental import pallas as pl
from jax.experimental.pallas import tpu as pltpu

F32 = jnp.float32
BF16 = jnp.bfloat16

D_MODEL = 2048
PAGE_SIZE = 128
GLA_HEADS = 4
GLA_DK = 128
GLA_DV = 256
GLA_GATE_RANK = 16
GLA_TAU = 16.0
GLA_CHUNK = 64
GLA_QK = GLA_HEADS * GLA_DK
GLA_WIDTH = GLA_HEADS * GLA_DV
MLA_HEADS = 8
MLA_Q_RANK = 512
MLA_KV_RANK = 512
MLA_NOPE = 128
MLA_ROPE = 64
MLA_DV = 128
MLA_WIDTH = MLA_HEADS * MLA_DV
ROPE_BASE = 10000.0
ATTN_SCALE = (MLA_NOPE + MLA_ROPE) ** -0.5
N_GROUPS = 8
EXPERTS_PER_GROUP = 8
N_EXPERTS = N_GROUPS * EXPERTS_PER_GROUP
TOP_K = 2
EXPERT_FF = 1408
EPS = 1e-6

LANES = 128
QK_PAD = 2 * LANES
COL_GQ = 0
COL_GK = GLA_QK
COL_GV = 2 * GLA_QK
COL_GR = COL_GV + GLA_WIDTH
COL_CQ = COL_GR + GLA_WIDTH
COL_CKV = COL_CQ + MLA_Q_RANK
COL_KR = COL_CKV + MLA_KV_RANK
IN_COLS_PAD = COL_KR + LANES
GA_LANE = MLA_ROPE
IN_COL_TILE = IN_COLS_PAD // 3

MXU_COLS = 256
DMA_THREADS = 2
MOE_ROWS = 768
MOE_SUB = 256
MOE_HALF = MXU_COLS
MOE_COLS = 2 * MXU_COLS
MOE_GU_CHUNKS = -(-EXPERT_FF // MOE_HALF)
MOE_DN_CHUNKS = D_MODEL // MOE_COLS
MOE_CHUNKS = MOE_GU_CHUNKS + MOE_DN_CHUNKS
MOE_RING = 4
VMEM_LIMIT = 56 * 1024 * 1024


def _params(sem, vmem=VMEM_LIMIT):
    return pltpu.CompilerParams(dimension_semantics=sem, vmem_limit_bytes=vmem)


def _silu(x):
    return x * jax.nn.sigmoid(x)


def _rms(x, g):
    return x * lax.rsqrt(jnp.mean(x * x, axis=-1, keepdims=True) + EPS) * g


def _ada_kernel(c_ref, w_ref, b_ref, o_ref):
    a = _silu(c_ref[...]).astype(BF16)
    o_ref[...] = jnp.dot(a, w_ref[...].astype(BF16), preferred_element_type=F32) + b_ref[...]


def _ada(c, w, b):
    rows, d = c.shape
    n = w.shape[1]
    tn = 1024
    return pl.pallas_call(
        _ada_kernel,
        grid=(n // tn,),
        in_specs=[pl.BlockSpec((rows, d), lambda j: (0, 0)),
                  pl.BlockSpec((d, tn), lambda j: (0, j)),
                  pl.BlockSpec((1, tn), lambda j: (0, j))],
        out_specs=pl.BlockSpec((rows, tn), lambda j: (0, j)),
        out_shape=jax.ShapeDtypeStruct((rows, n), F32),
        compiler_params=_params(("arbitrary",)),
        name="ada_mod",
    )(c, w, b)


def _inproj_kernel(x_ref, g_ref, sc_ref, sh_ref, w_hbm, o_ref, w_scr, h_scr, sem):
    @pl.when(pl.program_id(0) == 0)
    def _():
        cp = pltpu.make_async_copy(w_hbm, w_scr, sem)
        cp.start()
        cp.wait()

    h = _rms(x_ref[...], g_ref[...][None]) * (1.0 + sc_ref[...]) + sh_ref[...]
    h_scr[...] = h.reshape(h_scr.shape).astype(BF16)
    for j in range(w_scr.shape[1] // IN_COL_TILE):
        cols = slice(j * IN_COL_TILE, (j + 1) * IN_COL_TILE)
        o_ref[:, cols] = jnp.dot(h_scr[...], w_scr[:, cols], preferred_element_type=F32)


def _inproj(x, g, sc, sh, w, bb, lt):
    bsz, seq, d = x.shape
    rows = bb * lt
    nl = seq // lt
    n = w.shape[1]
    return pl.pallas_call(
        _inproj_kernel,
        grid=(bsz * seq // rows,),
        in_specs=[pl.BlockSpec((bb, lt, d), lambda i: (i // nl, i % nl, 0)),
                  pl.BlockSpec((1, d), lambda i: (0, 0)),
                  pl.BlockSpec((bb, 1, d), lambda i: (i // nl, 0, 0)),
                  pl.BlockSpec((bb, 1, d), lambda i: (i // nl, 0, 0)),
                  pl.BlockSpec(memory_space=pl.ANY)],
        out_specs=pl.BlockSpec((rows, n), lambda i: (i, 0)),
        out_shape=jax.ShapeDtypeStruct((bsz * seq, n), F32),
        scratch_shapes=[pltpu.VMEM(w.shape, BF16), pltpu.VMEM((rows, d), BF16), pltpu.SemaphoreType.DMA(())],
        compiler_params=_params(("arbitrary",)),
        name="in_proj",
    )(x, g, sc, sh, w)


def _split_bf16(x, pieces):
    out = []
    for _ in range(pieces):
        part = x.astype(BF16)
        out.append(part)
        x = x - part.astype(F32)
    return out


def _gla_kernel(q_ref, k_ref, v_ref, gr_ref, ga_ref, wg_ref, bg_ref, gon_ref, tri_b_ref, tri_f_ref, s0_ref,
                o_ref, sout_ref, s_scr, *, n_chunks, n_valid):
    c = GLA_CHUNK
    l_step = pl.program_id(2)

    @pl.when(l_step == 0)
    def _():
        s_scr[...] = s0_ref[...]

    chunks_per_batch = n_chunks // s_scr.shape[0]

    def load(ref):
        x = ref[...]
        if n_valid == c:
            return x
        zeros = jnp.zeros((c - n_valid, x.shape[1]), x.dtype)
        return jnp.concatenate([part for i in range(n_chunks)
                                for part in (x[i * n_valid:(i + 1) * n_valid], zeros)], axis=0)

    q = load(q_ref) * (GLA_DK ** -0.5)
    k = load(k_ref)
    v = load(v_ref).astype(BF16)
    a_hi, a_lo = _split_bf16(load(ga_ref), 2)
    w_hi, w_lo = _split_bf16(wg_ref[...], 2)
    z = jnp.dot(jnp.concatenate([a_hi, a_lo, a_hi], axis=1), jnp.concatenate([w_hi, w_hi, w_lo], axis=0),
                preferred_element_type=F32) + bg_ref[...]
    log_a = (jnp.minimum(z, 0.0) - jnp.log1p(jnp.exp(-jnp.abs(z)))) / GLA_TAU
    if n_valid < c:
        log_a = jnp.where(lax.broadcasted_iota(jnp.int32, log_a.shape, 0) % c < n_valid, log_a, 0.0)
    cum = jnp.dot(tri_b_ref[...], jnp.concatenate(_split_bf16(log_a, 3), axis=1), preferred_element_type=F32)
    b = cum[:, :GLA_DK] + cum[:, GLA_DK:2 * GLA_DK] + cum[:, 2 * GLA_DK:]
    q_dec = (q * jnp.exp(b)).astype(BF16)
    k_inv = (k * jnp.exp(-b)).astype(BF16)
    att = lax.dot_general(q_dec, k_inv, (((1,), (1,)), ((), ())), preferred_element_type=F32)
    att = jnp.where(tri_f_ref[...] > 0.0, att, 0.0).astype(BF16)
    o_intra = jnp.dot(att, v, preferred_element_type=F32)

    states = {}
    outs = []
    for idx in range(n_chunks):
        bi = idx // chunks_per_batch
        rows = slice(idx * c, (idx + 1) * c)
        s_old = states[bi] if bi in states else s_scr[bi]
        b_last = b[(idx + 1) * c - 1:(idx + 1) * c, :]
        o = o_intra[rows] + jnp.dot(q_dec[rows], s_old.astype(BF16), preferred_element_type=F32)
        k_dec = (k[rows] * jnp.exp(b_last - b[rows])).astype(BF16)
        kv = lax.dot_general(k_dec, v[rows], (((0,), (0,)), ((), ())), preferred_element_type=F32)
        decay = jnp.exp(jnp.broadcast_to(b_last, (GLA_DK, GLA_DK)).T)
        states[bi] = jnp.concatenate([decay] * (GLA_DV // GLA_DK), axis=1) * s_old + kv
        outs.append(o[:n_valid])
    for bi, s_new in states.items():
        s_scr[bi] = s_new
    o_all = _rms(jnp.concatenate(outs, axis=0), gon_ref[...])
    o_ref[...] = o_all * _silu(gr_ref[...])

    @pl.when(l_step == pl.num_programs(2) - 1)
    def _():
        sout_ref[...] = s_scr[...]


def _gla(proj, s0, wg_pad, bg, gon, bsz, seq, bb, lt):
    n_valid = math.gcd(seq, GLA_CHUNK)
    rows = bb * lt
    nl = seq // lt
    rmap = lambda b, h, l: b * nl + l
    n_chunks = rows // n_valid
    kern = functools.partial(_gla_kernel, n_chunks=n_chunks, n_valid=n_valid)
    st_spec = pl.BlockSpec((bb, None, GLA_DK, GLA_DV), lambda b, h, l: (b, h, 0, 0))
    pos = jnp.arange(n_chunks * GLA_CHUNK)
    tri = (pos[:, None] // GLA_CHUNK == pos[None, :] // GLA_CHUNK) & (pos[None, :] <= pos[:, None])
    tri_spec = pl.BlockSpec(tri.shape, lambda b, h, l: (0, 0))
    return pl.pallas_call(
        kern,
        grid=(bsz // bb, GLA_HEADS, nl),
        in_specs=[pl.BlockSpec((rows, GLA_DK), lambda b, h, l: (rmap(b, h, l), COL_GQ // GLA_DK + h)),
                  pl.BlockSpec((rows, GLA_DK), lambda b, h, l: (rmap(b, h, l), COL_GK // GLA_DK + h)),
                  pl.BlockSpec((rows, GLA_DV), lambda b, h, l: (rmap(b, h, l), COL_GV // GLA_DV + h)),
                  pl.BlockSpec((rows, GLA_DV), lambda b, h, l: (rmap(b, h, l), COL_GR // GLA_DV + h)),
                  pl.BlockSpec((rows, LANES), lambda b, h, l: (rmap(b, h, l), COL_KR // LANES)),
                  pl.BlockSpec((LANES, GLA_DK), lambda b, h, l: (0, h)),
                  pl.BlockSpec((1, GLA_DK), lambda b, h, l: (0, h)),
                  pl.BlockSpec((1, GLA_DV), lambda b, h, l: (0, 0)),
                  tri_spec, tri_spec, st_spec],
        out_specs=[pl.BlockSpec((rows, GLA_DV), lambda b, h, l: (rmap(b, h, l), h)), st_spec],
        out_shape=[jax.ShapeDtypeStruct((bsz * seq, GLA_WIDTH), F32),
                   jax.ShapeDtypeStruct((bsz, GLA_HEADS, GLA_DK, GLA_DV), F32)],
        scratch_shapes=[pltpu.VMEM((bb, GLA_DK, GLA_DV), F32)],
        compiler_params=_params(("arbitrary", "arbitrary", "arbitrary")),
        name="gla",
    )(proj, proj, proj, proj, proj, wg_pad, bg, gon, tri.astype(BF16), tri.astype(F32), s0)


def _rope(r, cos, sin_lo, sin_hi):
    return (r * cos + pltpu.roll(r, LANES - MLA_ROPE // 2, axis=1) * sin_lo
            + pltpu.roll(r, MLA_ROPE // 2, axis=1) * sin_hi)


def _mla_proj_kernel(cq_ref, ckv_ref, kr_ref, cos_ref, slo_ref, shi_ref, gq_ref, gkv_ref, wq_ref,
                     wk_ref, wv_ref, *out_refs, absorbed):
    cos, slo, shi = cos_ref[...], slo_ref[...], shi_ref[...]
    cqn = _rms(cq_ref[...], gq_ref[...]).astype(BF16)
    qf = jnp.dot(cqn, wq_ref[...], preferred_element_type=F32)
    ckv_n = _rms(ckv_ref[...], gkv_ref[...])
    k_rot = _rope(kr_ref[...], cos, slo, shi)
    if absorbed:
        qlat_ref, qr_ref, ckvn_ref, krope_ref = out_refs
    else:
        qcat_ref, kcat_ref, v_ref, ckvn_ref, krope_ref = out_refs
    ckvn_ref[...] = ckv_n
    krope_ref[...] = k_rot[:, :MLA_ROPE]
    for h in range(MLA_HEADS):
        q_nope = qf[:, h * QK_PAD:h * QK_PAD + LANES] * ATTN_SCALE
        q_rot = _rope(qf[:, h * QK_PAD + LANES:(h + 1) * QK_PAD], cos, slo, shi) * ATTN_SCALE
        if absorbed:
            qlat_ref[h] = jnp.dot(q_nope.astype(BF16), wk_ref[h], preferred_element_type=F32)
            qr_ref[h] = q_rot
        else:
            qcat_ref[:, h * QK_PAD:h * QK_PAD + LANES] = q_nope.astype(BF16)
            qcat_ref[:, h * QK_PAD + LANES:(h + 1) * QK_PAD] = q_rot.astype(BF16)
    if not absorbed:
        ckv_b = ckv_n.astype(BF16)
        k_nope = jnp.dot(ckv_b, wk_ref[...], preferred_element_type=F32)
        k_rot_b = k_rot.astype(BF16)
        for h in range(MLA_HEADS):
            kcat_ref[:, h * QK_PAD:h * QK_PAD + LANES] = k_nope[:, h * LANES:(h + 1) * LANES].astype(BF16)
            kcat_ref[:, h * QK_PAD + LANES:(h + 1) * QK_PAD] = k_rot_b
        v_ref[...] = jnp.dot(ckv_b, wv_ref[...], preferred_element_type=F32).astype(BF16)


def _mla_proj(proj, tabs, gq, gkv, wq, wk, wv, tm, absorbed):
    t = proj.shape[0]
    ntab = tabs[0].shape[0] // tm
    row = lambda c: pl.BlockSpec((tm, c), lambda i: (i, 0))
    full = lambda a: pl.BlockSpec(a.shape, lambda i: (0,) * a.ndim)
    tab = pl.BlockSpec((tm, LANES), lambda i: (i % ntab, 0))
    in_specs = [pl.BlockSpec((tm, MLA_Q_RANK), lambda i: (i, COL_CQ // MLA_Q_RANK)),
                pl.BlockSpec((tm, MLA_KV_RANK), lambda i: (i, COL_CKV // MLA_KV_RANK)),
                pl.BlockSpec((tm, LANES), lambda i: (i, COL_KR // LANES)),
                tab, tab, tab, full(gq), full(gkv), full(wq), full(wk), full(wv)]
    if absorbed:
        out_specs = [pl.BlockSpec((MLA_HEADS, tm, MLA_KV_RANK), lambda i: (0, i, 0)),
                     pl.BlockSpec((MLA_HEADS, tm, LANES), lambda i: (0, i, 0))]
        out_shape = [jax.ShapeDtypeStruct((MLA_HEADS, t, MLA_KV_RANK), F32),
                     jax.ShapeDtypeStruct((MLA_HEADS, t, LANES), F32)]
    else:
        out_specs = [row(MLA_HEADS * QK_PAD), row(MLA_HEADS * QK_PAD), row(MLA_WIDTH)]
        out_shape = [jax.ShapeDtypeStruct((t, MLA_HEADS * QK_PAD), BF16),
                     jax.ShapeDtypeStruct((t, MLA_HEADS * QK_PAD), BF16),
                     jax.ShapeDtypeStruct((t, MLA_WIDTH), BF16)]
    out_specs += [row(MLA_KV_RANK), row(MLA_ROPE)]
    out_shape += [jax.ShapeDtypeStruct((t, MLA_KV_RANK), F32), jax.ShapeDtypeStruct((t, MLA_ROPE), F32)]
    return pl.pallas_call(
        functools.partial(_mla_proj_kernel, absorbed=absorbed),
        grid=(t // tm,),
        in_specs=in_specs, out_specs=out_specs, out_shape=out_shape,
        compiler_params=_params(("arbitrary",)),
        name="mla_proj_absorbed" if absorbed else "mla_proj",
    )(proj, proj, proj, *tabs, gq, gkv, wq, wk, wv)


def _attn_kernel(q_ref, k_ref, v_ref, g_ref, o_ref, *, tq, tk):
    assert tq == tk
    qi = pl.program_id(2)
    q = q_ref[...]

    def block(j, carry, on_diagonal):
        m, l, acc = carry
        k0 = pl.multiple_of(j * tk, tk)
        s = lax.dot_general(q, k_ref[pl.ds(k0, tk), :], (((1,), (1,)), ((), ())), preferred_element_type=F32)
        if on_diagonal:
            row = lax.broadcasted_iota(jnp.int32, (tq, tk), 0)
            col = lax.broadcasted_iota(jnp.int32, (tq, tk), 1)
            s = jnp.where(col <= row, s, -jnp.inf)
        m_new = jnp.maximum(m, jnp.max(s, axis=-1, keepdims=True))
        alpha = jnp.exp(m - m_new)
        p = jnp.exp(s - m_new)
        l = alpha * l + jnp.sum(p, axis=-1, keepdims=True)
        acc = alpha * acc + jnp.dot(p.astype(BF16), v_ref[pl.ds(k0, tk), :], preferred_element_type=F32)
        return m_new, l, acc

    init = (jnp.full((tq, 1), -jnp.inf, F32), jnp.zeros((tq, 1), F32), jnp.zeros((tq, MLA_DV), F32))
    carry = lax.fori_loop(0, qi, lambda j, c: block(j, c, False), init)
    _, l, acc = block(qi, carry, True)
    o_ref[...] = _rms(acc / l, g_ref[...])


def _prompt_attention(q_cat, k_cat, v, g, bsz, seq, tq, tk):
    nq = seq // tq
    return pl.pallas_call(
        functools.partial(_attn_kernel, tq=tq, tk=tk),
        grid=(bsz, MLA_HEADS, nq),
        in_specs=[pl.BlockSpec((tq, QK_PAD), lambda b, h, i: (b * nq + i, h)),
                  pl.BlockSpec((seq, QK_PAD), lambda b, h, i: (b, h)),
                  pl.BlockSpec((seq, MLA_DV), lambda b, h, i: (b, h)),
                  pl.BlockSpec((1, MLA_DV), lambda b, h, i: (0, 0))],
        out_specs=pl.BlockSpec((tq, MLA_DV), lambda b, h, i: (b * nq + i, h)),
        out_shape=jax.ShapeDtypeStruct((bsz * seq, MLA_WIDTH), F32),
        compiler_params=_params(("arbitrary", "arbitrary", "arbitrary")),
        name="prompt_attention",
    )(q_cat, k_cat, v, g)


def _paged_attn_kernel(pt_ref, qlat_ref, qr_ref, cnew_ref, knew_ref, ckv_hbm, krt_hbm, o_ref,
                       ckv_buf, krt_buf, sems, m_scr, l_scr, acc_scr, *, pp, n_q):
    grp = pl.program_id(1)
    n_grp = pl.num_programs(1)
    step = pl.program_id(0) * n_grp + grp
    n_steps = pl.num_programs(0) * n_grp
    slot = step % 2
    rows = MLA_HEADS * n_q

    def page_copies(at_step, at_slot):
        out = []
        for j in range(pp):
            page = pt_ref[at_step * pp + j]
            out.append(pltpu.make_async_copy(ckv_hbm.at[0, page], ckv_buf.at[at_slot, j], sems.at[0, at_slot]))
            out.append(pltpu.make_async_copy(krt_hbm.at[0, page],
                                             krt_buf.at[at_slot, :, pl.ds(j * PAGE_SIZE, PAGE_SIZE)],
                                             sems.at[1, at_slot]))
        return out

    def start_pages(at_step, at_slot):
        for i, cp in enumerate(page_copies(at_step, at_slot)):
            cp.start(priority=(i // 2 + i) % DMA_THREADS)

    @pl.when(step == 0)
    def _():
        start_pages(step, slot)

    @pl.when(step + 1 < n_steps)
    def _():
        start_pages(step + 1, 1 - slot)

    @pl.when(grp == 0)
    def _():
        m_scr[...] = jnp.full(m_scr.shape, -jnp.inf, F32)
        l_scr[...] = jnp.zeros(l_scr.shape, F32)
        acc_scr[...] = jnp.zeros(acc_scr.shape, F32)

    q = qlat_ref[...].reshape(rows, MLA_KV_RANK).astype(BF16)
    qr = qr_ref[...].reshape(rows, LANES)[:, :MLA_ROPE].astype(BF16)

    def update(s, v):
        m = m_scr[...]
        m_new = jnp.maximum(m, jnp.max(s, axis=-1, keepdims=True))
        alpha = jnp.exp(m - m_new)
        p = jnp.exp(s - m_new)
        l_scr[...] = alpha * l_scr[...] + jnp.sum(p, axis=-1, keepdims=True)
        acc_scr[...] = acc_scr[...] * alpha + jnp.dot(p.astype(BF16), v, preferred_element_type=F32)
        m_scr[...] = m_new

    for cp in page_copies(step, slot):
        cp.wait()
    ckv = ckv_buf[slot].reshape(pp * PAGE_SIZE, MLA_KV_RANK).astype(BF16)
    s_past = (lax.dot_general(q, ckv, (((1,), (1,)), ((), ())), preferred_element_type=F32)
              + jnp.dot(qr, krt_buf[slot].astype(BF16), preferred_element_type=F32))
    update(s_past, ckv)

    @pl.when(grp == n_grp - 1)
    def _():
        pad = PAGE_SIZE - n_q
        c_new = jnp.concatenate([cnew_ref[...], jnp.zeros((pad, MLA_KV_RANK), F32)], axis=0).astype(BF16)
        k_new = jnp.concatenate([knew_ref[...], jnp.zeros((pad, MLA_ROPE), F32)], axis=0).astype(BF16)
        s = (lax.dot_general(q, c_new, (((1,), (1,)), ((), ())), preferred_element_type=F32)
             + lax.dot_general(qr, k_new, (((1,), (1,)), ((), ())), preferred_element_type=F32))
        q_idx = lax.broadcasted_iota(jnp.int32, s.shape, 0) % n_q
        k_idx = lax.broadcasted_iota(jnp.int32, s.shape, 1)
        update(jnp.where(k_idx <= q_idx, s, -jnp.inf), c_new)
        o_ref[...] = (acc_scr[...] / l_scr[...]).reshape(o_ref.shape)


def _paged_attention(q_lat, q_r, ckv_n, k_rope, cache_ckv, cache_krope_t, page_table, n_q, pp):
    bsz, n_pages = page_table.shape
    rows = MLA_HEADS * n_q
    grid_spec = pltpu.PrefetchScalarGridSpec(
        num_scalar_prefetch=1,
        grid=(bsz, n_pages // pp),
        in_specs=[pl.BlockSpec((MLA_HEADS, n_q, MLA_KV_RANK), lambda b, s, pt: (0, b, 0)),
                  pl.BlockSpec((MLA_HEADS, n_q, LANES), lambda b, s, pt: (0, b, 0)),
                  pl.BlockSpec((n_q, MLA_KV_RANK), lambda b, s, pt: (b, 0)),
                  pl.BlockSpec((n_q, MLA_ROPE), lambda b, s, pt: (b, 0)),
                  pl.BlockSpec(memory_space=pl.ANY),
                  pl.BlockSpec(memory_space=pl.ANY)],
        out_specs=pl.BlockSpec((MLA_HEADS, n_q, MLA_KV_RANK), lambda b, s, pt: (0, b, 0)),
        scratch_shapes=[pltpu.VMEM((2, pp, PAGE_SIZE, MLA_KV_RANK), F32),
                        pltpu.VMEM((2, MLA_ROPE, pp * PAGE_SIZE), F32),
                        pltpu.SemaphoreType.DMA((2, 2)),
                        pltpu.VMEM((rows, 1), F32), pltpu.VMEM((rows, 1), F32),
                        pltpu.VMEM((rows, MLA_KV_RANK), F32)],
    )
    return pl.pallas_call(
        functools.partial(_paged_attn_kernel, pp=pp, n_q=n_q),
        grid_spec=grid_spec,
        out_shape=jax.ShapeDtypeStruct((MLA_HEADS, bsz * n_q, MLA_KV_RANK), F32),
        compiler_params=_params(("arbitrary", "arbitrary")),
        name="paged_attention",
    )(page_table.reshape(-1), q_lat, q_r, ckv_n, k_rope, cache_ckv, cache_krope_t)


def _latent_out_kernel(o_ref, w_ref, g_ref, out_ref):
    o = jnp.dot(o_ref[...].astype(BF16), w_ref[...], preferred_element_type=F32)
    out_ref[...] = _rms(o, g_ref[...])


def _latent_out(o_lat, w_uv_h, g, tm):
    t = o_lat.shape[1]
    return pl.pallas_call(
        _latent_out_kernel,
        grid=(MLA_HEADS, t // tm),
        in_specs=[pl.BlockSpec((None, tm, MLA_KV_RANK), lambda h, i: (h, i, 0)),
                  pl.BlockSpec((None, MLA_KV_RANK, MLA_DV), lambda h, i: (h, 0, 0)),
                  pl.BlockSpec((1, MLA_DV), lambda h, i: (0, 0))],
        out_specs=pl.BlockSpec((tm, MLA_DV), lambda h, i: (i, h)),
        out_shape=jax.ShapeDtypeStruct((t, MLA_WIDTH), F32),
        compiler_params=_params(("arbitrary", "arbitrary")),
        name="latent_out",
    )(o_lat, w_uv_h, g)


def _route(logit):
    lane = lax.broadcasted_iota(jnp.int32, logit.shape, 1)
    neg = -jnp.inf
    big = jnp.int32(LANES)

    def first_max(vals):
        top = jnp.max(vals, axis=-1, keepdims=True)
        return top, jnp.min(jnp.where(vals == top, lane, big), axis=-1, keepdims=True)

    g_vals = jnp.where(lane < N_GROUPS, logit, neg)
    g_max, g_idx = first_max(g_vals)
    g_w = 1.0 / jnp.sum(jnp.exp(g_vals - g_max), axis=-1, keepdims=True)
    e_lane = lane - N_GROUPS
    in_group = (e_lane >= g_idx * EXPERTS_PER_GROUP) & (e_lane < (g_idx + 1) * EXPERTS_PER_GROUP)
    e_vals = jnp.where(in_group, logit, neg)
    v1, i1 = first_max(e_vals)
    v2, i2 = first_max(jnp.where(lane == i1, neg, e_vals))
    e2 = jnp.exp(v2 - v1)
    w1 = g_w / (1.0 + e2)
    w2 = g_w * e2 / (1.0 + e2)
    ids = jnp.where(lane == 0, i1 - N_GROUPS, jnp.where(lane == 1, i2 - N_GROUPS, 0))
    wts = jnp.where(lane == 0, w1, jnp.where(lane == 1, w2, 0.0))
    return ids, wts


def _outproj_kernel(og_ref, om_ref, wg_ref, wm_ref, x_ref, gt_ref, sc_ref, sh_ref, g_ref, wr_ref, br_ref,
                    x1_ref, h2_ref, id_ref, wt_ref):
    y = (jnp.dot(og_ref[...].astype(BF16), wg_ref[...], preferred_element_type=F32)
         + jnp.dot(om_ref[...].astype(BF16), wm_ref[...], preferred_element_type=F32))
    x1 = x_ref[...] + gt_ref[...] * y.reshape(x_ref.shape)
    x1_ref[...] = x1
    h2 = (_rms(x1, g_ref[...][None]) * (1.0 + sc_ref[...]) + sh_ref[...]).reshape(h2_ref.shape)
    h2_ref[...] = h2
    rows = h2.shape[0]
    lg = jnp.dot(jnp.concatenate(_split_bf16(h2, 2), axis=0), wr_ref[...], preferred_element_type=F32)
    logit = (lg[:rows, :LANES] + lg[:rows, LANES:] + lg[rows:, :LANES] + lg[rows:, LANES:]) + br_ref[...]
    id_ref[...], wt_ref[...] = _route(logit)


def _outproj(o_gla, o_mla, w_top, w_bot, x, gt, sc, sh, g, w_r, b_r, bb, lt):
    bsz, seq, d = x.shape
    rows = bb * lt
    nl = seq // lt
    xs = pl.BlockSpec((bb, lt, d), lambda i: (i // nl, i % nl, 0))
    ms = pl.BlockSpec((bb, 1, d), lambda i: (i // nl, 0, 0))
    full = lambda a: pl.BlockSpec(a.shape, lambda i: (0,) * a.ndim)
    wide = pl.BlockSpec((rows, LANES), lambda i: (i, 0))
    return pl.pallas_call(
        _outproj_kernel,
        grid=(bsz * seq // rows,),
        in_specs=[pl.BlockSpec((rows, GLA_WIDTH), lambda i: (i, 0)),
                  pl.BlockSpec((rows, MLA_WIDTH), lambda i: (i, 0)),
                  full(w_top), full(w_bot), xs, ms, ms, ms, full(g), full(w_r), full(b_r)],
        out_specs=[xs, pl.BlockSpec((rows, d), lambda i: (i, 0)), wide, wide],
        out_shape=[jax.ShapeDtypeStruct(x.shape, F32), jax.ShapeDtypeStruct((bsz * seq, d), F32),
                   jax.ShapeDtypeStruct((bsz * seq, LANES), jnp.int32),
                   jax.ShapeDtypeStruct((bsz * seq, LANES), F32)],
        compiler_params=_params(("arbitrary",)),
        name="out_proj_route",
    )(o_gla, o_mla, w_top, w_bot, x, gt, sc, sh, g, w_r, b_r)


def _moe_kernel(ie_ref, ir_ref, in_ref, im_ref, nl_ref, src_ref, dst_ref,
                hp_hbm, hs_hbm, wgu_hbm, wd_hbm, y_hbm,
                x_buf, xb_scr, wb_scr, act_scr, y_buf, ring, gsem, ssem, wsem):
    n_live = nl_ref[0]
    last_item = ie_ref.shape[0] - 1

    def chunk_width(j):
        return min(MOE_HALF, EXPERT_FF - j * MOE_HALF)

    def weight_copies(e, j, slot):
        sem = wsem.at[slot]
        if j < MOE_GU_CHUNKS:
            w = chunk_width(j)
            return [pltpu.make_async_copy(wgu_hbm.at[e, :, pl.ds(off + j * MOE_HALF, w)],
                                          ring.at[slot, :, pl.ds(half * w, w)], sem)
                    for half, off in enumerate((0, EXPERT_FF))]
        c = j - MOE_GU_CHUNKS
        return [pltpu.make_async_copy(wd_hbm.at[e, :, pl.ds(c * MOE_COLS, MOE_COLS)],
                                      ring.at[slot, pl.ds(0, EXPERT_FF)], sem)]

    def start_weights(e, j, slot):
        for idx, cp in enumerate(weight_copies(e, j, slot)):
            cp.start(priority=(j + idx) % DMA_THREADS)

    def gather_copy(h_hbm, src_row, p, r, k=1):
        return pltpu.make_async_copy(h_hbm.at[pl.ds(src_row, k)], x_buf.at[p, pl.ds(r, k)], gsem.at[p])

    def scatter_copy(r, dst_row, k=1):
        return pltpu.make_async_copy(y_buf.at[pl.ds(r, k)], y_hbm.at[pl.ds(dst_row, k)], ssem)

    def for_rows(lo, hi, fn, unroll=4):
        groups = (hi - lo) // unroll

        def group(g, c):
            for u in range(unroll):
                fn(lo + g * unroll + u)
            return c

        def single(r, c):
            fn(r)
            return c

        lax.fori_loop(0, groups, group, 0)
        lax.fori_loop(lo + groups * unroll, hi, single, 0)

    def wait_rows(n, copy_of_rows):
        k = 1
        while k <= MOE_ROWS:
            @pl.when((n & k) != 0)
            def _(k=k):
                copy_of_rows(k).wait()
            k *= 2

    def start_gather(i, p):
        row0, n, m = ir_ref[i], in_ref[i], im_ref[i]
        for_rows(0, m, lambda r: gather_copy(hp_hbm, src_ref[row0 + r], p, r).start())
        for_rows(m, n, lambda r: gather_copy(hs_hbm, src_ref[row0 + r], p, r).start())

    x_buf[...] = jnp.zeros(x_buf.shape, F32)
    start_gather(0, 0)
    for j in range(MOE_RING):
        start_weights(ie_ref[0], j, j)

    def item_body(i, carry):
        p = i % 2
        e, row0, n = ie_ref[i], ir_ref[i], in_ref[i]
        n_sub = (n + MOE_SUB - 1) // MOE_SUB
        has_next = i + 1 < n_live
        e_next = ie_ref[jnp.minimum(i + 1, last_item)]
        base = (i * MOE_CHUNKS) % MOE_RING

        wait_rows(n, lambda k: gather_copy(hp_hbm, 0, p, 0, k))

        @pl.when(has_next)
        def _():
            start_gather(i + 1, 1 - p)

        def for_row_blocks(fn):
            pairs = n_sub // 2

            def pair(b, c):
                fn(pl.multiple_of(b * 2 * MOE_SUB, 2 * MOE_SUB), 2 * MOE_SUB)
                return c
            lax.fori_loop(0, pairs, pair, 0)

            @pl.when(n_sub % 2 == 1)
            def _():
                fn(pl.multiple_of(pairs * 2 * MOE_SUB, 2 * MOE_SUB), MOE_SUB)

        def cast_rows(sb, c):
            r0 = pl.multiple_of(sb * MOE_SUB, MOE_SUB)
            xb_scr[pl.ds(r0, MOE_SUB), :] = x_buf[p, pl.ds(r0, MOE_SUB), :].astype(BF16)
            return c
        lax.fori_loop(0, n_sub, cast_rows, 0)

        for j in range(MOE_CHUNKS):
            slot = (base + j) % MOE_RING
            if j == MOE_GU_CHUNKS:
                @pl.when(i > 0)
                def _():
                    n_prev = in_ref[jnp.maximum(i - 1, 0)]
                    wait_rows(n_prev, lambda k: scatter_copy(0, 0, k))
            for cp in weight_copies(e, j, slot):
                cp.wait()
            if j < MOE_GU_CHUNKS:
                cols = 2 * chunk_width(j)
                wb_scr[:, :cols] = ring[slot, :, pl.ds(0, cols)].astype(BF16)
            else:
                wb_scr[:EXPERT_FF, :] = ring[slot, pl.ds(0, EXPERT_FF), :].astype(BF16)
            j_ahead = j + MOE_RING
            if j_ahead < MOE_CHUNKS:
                start_weights(e, j_ahead, slot)
            else:
                @pl.when(has_next)
                def _(j_ahead=j_ahead, slot=slot):
                    start_weights(e_next, j_ahead - MOE_CHUNKS, slot)

            if j < MOE_GU_CHUNKS:
                w = chunk_width(j)

                def gate_up(r0, rows, j=j, w=w):
                    res = jnp.dot(xb_scr[pl.ds(r0, rows), :], wb_scr[:, :2 * w], preferred_element_type=F32)
                    act = _silu(res[:, :w]) * res[:, w:]
                    act_scr[pl.ds(r0, rows), j * MOE_HALF:j * MOE_HALF + w] = act.astype(BF16)
                for_row_blocks(gate_up)
            else:
                col0 = (j - MOE_GU_CHUNKS) * MOE_COLS

                def down(r0, rows, col0=col0):
                    y_buf[pl.ds(r0, rows), col0:col0 + MOE_COLS] = jnp.dot(
                        act_scr[pl.ds(r0, rows), :], wb_scr[:EXPERT_FF, :], preferred_element_type=F32)
                for_row_blocks(down)

        for_rows(0, n, lambda r: scatter_copy(r, dst_ref[row0 + r]).start())
        return carry

    lax.fori_loop(0, n_live, item_body, 0)
    wait_rows(in_ref[jnp.maximum(n_live - 1, 0)], lambda k: scatter_copy(0, 0, k))


def _moe(plan, h2_p, h2_s, w_gu, w_down):
    d = h2_p.shape[1]
    t_all = h2_p.shape[0] + h2_s.shape[0]
    hbm = pl.BlockSpec(memory_space=pl.ANY)
    grid_spec = pltpu.PrefetchScalarGridSpec(
        num_scalar_prefetch=len(plan),
        grid=(1,),
        in_specs=[hbm, hbm, hbm, hbm],
        out_specs=hbm,
        scratch_shapes=[pltpu.VMEM((2, MOE_ROWS, d), F32),
                        pltpu.VMEM((MOE_ROWS, d), BF16),
                        pltpu.VMEM((d, MOE_COLS), BF16),
                        pltpu.VMEM((MOE_ROWS, EXPERT_FF), BF16),
                        pltpu.VMEM((MOE_ROWS, d), F32),
                        pltpu.VMEM((MOE_RING, d, MOE_COLS), F32),
                        pltpu.SemaphoreType.DMA((2,)),
                        pltpu.SemaphoreType.DMA(()),
                        pltpu.SemaphoreType.DMA((MOE_RING,))],
    )
    return pl.pallas_call(
        _moe_kernel,
        grid_spec=grid_spec,
        out_shape=jax.ShapeDtypeStruct((t_all * TOP_K, d), F32),
        compiler_params=_params(("arbitrary",)),
        name="moe_experts",
    )(*plan, h2_p, h2_s, w_gu, w_down)


def _moe_plan(e_id, t_prompt):
    n_as = e_id.shape[0]
    n_items = N_EXPERTS + n_as // MOE_ROWS
    order = jnp.argsort(e_id).astype(jnp.int32)
    cnt = jnp.zeros((N_EXPERTS,), jnp.int32).at[e_id].add(1)
    cnt_p = jnp.zeros((N_EXPERTS,), jnp.int32).at[e_id[:t_prompt * TOP_K]].add(1)
    ustart = jnp.cumsum(cnt) - cnt
    per_e = (cnt + MOE_ROWS - 1) // MOE_ROWS
    item_end = jnp.cumsum(per_e)
    total = item_end[-1]
    idx = jnp.arange(n_items, dtype=jnp.int32)
    live = idx < total
    e_of = jnp.searchsorted(item_end, jnp.minimum(idx, total - 1), side='right').astype(jnp.int32)
    local = jnp.minimum(idx, total - 1) - (item_end[e_of] - per_e[e_of])
    row0 = ustart[e_of] + local * MOE_ROWS
    n_rows = jnp.where(live, jnp.minimum(MOE_ROWS, cnt[e_of] - local * MOE_ROWS), 0)
    m_rows = jnp.clip(cnt_p[e_of] - local * MOE_ROWS, 0, n_rows)
    tok_s = order // TOP_K
    src_s = jnp.where(tok_s < t_prompt, tok_s, tok_s - t_prompt)
    dst_s = (order % TOP_K) * (n_as // TOP_K) + tok_s
    i32 = lambda a: a.astype(jnp.int32)
    return (i32(e_of), i32(row0), i32(n_rows), i32(m_rows), i32(total)[None], i32(src_s), i32(dst_s))


def _final_kernel(x_ref, gt_ref, y0_ref, y1_ref, wt_ref, g_ref, o_ref):
    wt = wt_ref[...]
    y = wt[:, 0:1] * y0_ref[...] + wt[:, 1:2] * y1_ref[...]
    x2 = x_ref[...] + gt_ref[...] * y.reshape(x_ref.shape)
    o_ref[...] = _rms(x2, g_ref[...][None])


def _final(x1, gt, y_rows, wt, g, bb, lt, row_block0):
    bsz, seq, d = x1.shape
    rows = bb * lt
    nl = seq // lt
    choice_blocks = y_rows.shape[0] // TOP_K // rows
    xs = pl.BlockSpec((bb, lt, d), lambda i: (i // nl, i % nl, 0))
    return pl.pallas_call(
        _final_kernel,
        grid=(bsz * seq // rows,),
        in_specs=[xs, pl.BlockSpec((bb, 1, d), lambda i: (i // nl, 0, 0)),
                  pl.BlockSpec((rows, d), lambda i: (row_block0 + i, 0)),
                  pl.BlockSpec((rows, d), lambda i: (choice_blocks + row_block0 + i, 0)),
                  pl.BlockSpec((rows, LANES), lambda i: (i, 0)),
                  pl.BlockSpec((1, d), lambda i: (0, 0))],
        out_specs=xs,
        out_shape=jax.ShapeDtypeStruct(x1.shape, F32),
        compiler_params=_params(("arbitrary",)),
        name="moe_combine_final_norm",
    )(x1, gt, y_rows, y_rows, wt, g)


def _rope_tables(pos, rows):
    half = MLA_ROPE // 2
    inv_freq = ROPE_BASE ** (-jnp.arange(half, dtype=F32) / half)
    ang = pos.astype(F32)[:, None] * inv_freq[None, :]
    cos, sin = jnp.cos(ang), jnp.sin(ang)
    zero = jnp.zeros_like(cos)
    pad = jnp.zeros((pos.shape[0], LANES - MLA_ROPE), F32)
    tabs = (jnp.concatenate([cos, cos, pad], axis=1),
            jnp.concatenate([-sin, zero, pad], axis=1),
            jnp.concatenate([zero, sin, pad], axis=1))
    reps = rows // pos.shape[0]
    return tuple(jnp.tile(t, (reps, 1)) for t in tabs)


def _arrange_w_in(w_in):
    d = w_in.shape[0]
    off_gv = 2 * GLA_QK
    off_ga = off_gv + GLA_WIDTH
    off_gr = off_ga + GLA_GATE_RANK
    off_cq = off_gr + GLA_WIDTH
    off_ckv = off_cq + MLA_Q_RANK
    off_kr = off_ckv + MLA_KV_RANK
    pad = jnp.zeros((d, LANES - MLA_ROPE - GLA_GATE_RANK), w_in.dtype)
    return jnp.concatenate([w_in[:, :off_ga], w_in[:, off_gr:off_kr + MLA_ROPE], w_in[:, off_ga:off_gr], pad],
                           axis=1).astype(BF16)


def _arrange_w_uq(w_uq):
    r = w_uq.shape[0]
    pad = jnp.zeros((r, MLA_HEADS, QK_PAD - MLA_NOPE - MLA_ROPE), w_uq.dtype)
    return jnp.concatenate([w_uq, pad], axis=2).reshape(r, MLA_HEADS * QK_PAD).astype(BF16)


def _mixer(x, mods, pos, s0, cache, w, prompt):
    bsz, seq, d = x.shape
    t = bsz * seq
    sh1, sc1, gt1, sh2, sc2, _ = mods
    bb, lt = (1, 512) if prompt else (512 // seq, seq)
    proj = _inproj(x, w['g_mix'], sc1, sh1, w['w_in'], bb, lt)
    gbb, glt = (1, 512) if prompt else (8, seq)
    o_gla, s_fin = _gla(proj, s0, w['wg_pad'], w['bg'], w['g_gla'], bsz, seq, gbb, glt)
    tm = 256
    if prompt:
        tabs = _rope_tables(pos, seq)
        q_cat, k_cat, v, ckv_n, k_rope = _mla_proj(proj, tabs, w['g_q'], w['g_kv'], w['w_uq'], w['w_uk'],
                                                    w['w_uv'], tm, absorbed=False)
        o_mla = _prompt_attention(q_cat, k_cat, v, w['g_mla'], bsz, seq, 512, 512)
    else:
        cache_ckv, cache_krope, page_table = cache
        tabs = _rope_tables(pos, tm)
        q_lat, q_r, ckv_n, k_rope = _mla_proj(proj, tabs, w['g_q'], w['g_kv'], w['w_uq'], w['w_uk_t'],
                                              w['w_uv'], tm, absorbed=True)
        o_lat = _paged_attention(q_lat, q_r, ckv_n, k_rope, cache_ckv, jnp.swapaxes(cache_krope, 2, 3),
                                 page_table, seq, 32)
        o_mla = _latent_out(o_lat, w['w_uv_h'], w['g_mla'], tm)
    obb, olt = (1, 256) if prompt else (256 // seq, seq)
    routed = _outproj(o_gla, o_mla, w['w_out_top'], w['w_out_bot'], x, gt1, sc2, sh2, w['g_ffn'],
                      w['w_r'], w['b_r'], obb, olt)
    return routed, ckv_n.reshape(bsz, seq, MLA_KV_RANK), k_rope.reshape(bsz, seq, MLA_ROPE), s_fin


def kernel(x_prompt, x_sample, c_prompt, c_sample, cache_ckv, cache_krope, state_gla, page_table, w_ada, b_ada, g_mix, g_ffn, w_in, gla_w_gate, gla_b_gate, gla_g_onorm, mla_g_qnorm, mla_w_uq, mla_g_kvnorm, mla_w_uk, mla_w_uv, mla_g_onorm, w_out, moe_w_group, moe_b_group, moe_w_expert, moe_b_expert, moe_w_gu, moe_w_down, g_final):
    depth = w_ada.shape[0]
    assert depth == 1
    bp, lp, d = x_prompt.shape
    bs, ls, _ = x_sample.shape
    n_past = page_table.shape[1] * PAGE_SIZE
    lyr = 0

    n_c = bp + bs
    c_rows = -(-n_c // 8) * 8
    c_all = jnp.concatenate([c_prompt, c_sample, jnp.zeros((c_rows - n_c, d), F32)], axis=0)
    mod = _ada(c_all, w_ada[lyr], b_ada[lyr][None])
    mods_p = tuple(m[:bp, None, :] for m in jnp.split(mod, 6, axis=-1))
    mods_s = tuple(m[bp:n_c, None, :] for m in jnp.split(mod, 6, axis=-1))

    wg_pad = jnp.zeros((LANES, GLA_QK), F32).at[GA_LANE:GA_LANE + GLA_GATE_RANK].set(gla_w_gate[lyr])
    w_uk = mla_w_uk[lyr]
    w_uv = mla_w_uv[lyr]
    w = {
        'g_mix': g_mix[lyr][None], 'g_ffn': g_ffn[lyr][None],
        'w_in': _arrange_w_in(w_in[lyr]),
        'wg_pad': wg_pad, 'bg': gla_b_gate[lyr][None], 'g_gla': gla_g_onorm[lyr][None],
        'g_q': mla_g_qnorm[lyr][None], 'g_kv': mla_g_kvnorm[lyr][None], 'g_mla': mla_g_onorm[lyr][None],
        'w_uq': _arrange_w_uq(mla_w_uq[lyr]),
        'w_uk': w_uk.reshape(MLA_KV_RANK, MLA_HEADS * MLA_NOPE).astype(BF16),
        'w_uk_t': w_uk.transpose(1, 2, 0).astype(BF16),
        'w_uv': w_uv.reshape(MLA_KV_RANK, MLA_WIDTH).astype(BF16),
        'w_uv_h': w_uv.transpose(1, 0, 2).astype(BF16),
        'w_out_top': w_out[lyr][:GLA_WIDTH].astype(BF16),
        'w_out_bot': w_out[lyr][GLA_WIDTH:].astype(BF16),
    }
    w_r = jnp.concatenate([moe_w_group[lyr], moe_w_expert[lyr],
                           jnp.zeros((d, LANES - N_GROUPS - N_EXPERTS), F32)], axis=1)
    w['w_r'] = jnp.concatenate(_split_bf16(w_r, 2), axis=1)
    w['b_r'] = jnp.concatenate([moe_b_group[lyr], moe_b_expert[lyr],
                                jnp.zeros((LANES - N_GROUPS - N_EXPERTS,), F32)])[None]

    pos_p = jnp.arange(lp, dtype=jnp.int32)
    pos_s = n_past + jnp.arange(ls, dtype=jnp.int32)
    s0_p = jnp.zeros((bp, GLA_HEADS, GLA_DK, GLA_DV), F32)
    (x1_p, h2_p, ids_p, wts_p), ckv_p, kr_p, st_p = _mixer(x_prompt, mods_p, pos_p, s0_p, None, w, prompt=True)
    (x1_s, h2_s, ids_s, wts_s), ckv_s, kr_s, st_s = _mixer(x_sample, mods_s, pos_s, state_gla[lyr],
                                                          (cache_ckv, cache_krope, page_table), w, prompt=False)

    e_id = jnp.concatenate([ids_p[:, :TOP_K], ids_s[:, :TOP_K]], axis=0).reshape(-1)
    y_rows = _moe(_moe_plan(e_id, bp * lp), h2_p, h2_s, moe_w_gu[lyr], moe_w_down[lyr])

    y_p = _final(x1_p, mods_p[5], y_rows, wts_p, g_final[None], 1, 256, 0)
    y_s = _final(x1_s, mods_s[5], y_rows, wts_s, g_final[None], 256 // ls, ls, bp * lp // 256)
    return (y_p, y_s, ckv_p[None], kr_p[None], st_p[None], ckv_s[None], kr_s[None], st_s[None])
```

```python
import functools
import math

import jax
import jax.numpy as jnp
from jax import lax
from jax.experimental import pallas as pl
from jax.experimental.pallas import tpu as pltpu

F32 = jnp.float32
BF16 = jnp.bfloat16

D_MODEL = 2048
PAGE_SIZE = 128
GLA_HEADS = 4
GLA_DK = 128
GLA_DV = 256
GLA_GATE_RANK = 16
GLA_TAU = 16.0
GLA_CHUNK = 64
GLA_QK = GLA_HEADS * GLA_DK
GLA_WIDTH = GLA_HEADS * GLA_DV
MLA_HEADS = 8
MLA_Q_RANK = 512
MLA_KV_RANK = 512
MLA_NOPE = 128
MLA_ROPE = 64
MLA_DV = 128
MLA_WIDTH = MLA_HEADS * MLA_DV
ROPE_BASE = 10000.0
ATTN_SCALE = (MLA_NOPE + MLA_ROPE) ** -0.5
N_GROUPS = 8
EXPERTS_PER_GROUP = 8
N_EXPERTS = N_GROUPS * EXPERTS_PER_GROUP
TOP_K = 2
EXPERT_FF = 1408
EPS = 1e-6

LANES = 128
QK_PAD = 2 * LANES
COL_GQ = 0
COL_GK = GLA_QK
COL_GV = 2 * GLA_QK
COL_GR = COL_GV + GLA_WIDTH
COL_CQ = COL_GR + GLA_WIDTH
COL_CKV = COL_CQ + MLA_Q_RANK
COL_KR = COL_CKV + MLA_KV_RANK
IN_COLS_PAD = COL_KR + LANES
GA_LANE = MLA_ROPE
IN_COL_TILE = IN_COLS_PAD // 3

MXU_COLS = 256
DMA_THREADS = 2
MOE_ROWS = 768
MOE_SUB = 256
MOE_HALF = MXU_COLS
MOE_COLS = 2 * MXU_COLS
MOE_GU_CHUNKS = -(-EXPERT_FF // MOE_HALF)
MOE_DN_CHUNKS = D_MODEL // MOE_COLS
MOE_CHUNKS = MOE_GU_CHUNKS + MOE_DN_CHUNKS
MOE_RING = 4
VMEM_LIMIT = 56 * 1024 * 1024


def _params(sem, vmem=VMEM_LIMIT):
    return pltpu.CompilerParams(dimension_semantics=sem, vmem_limit_bytes=vmem)


def _silu(x):
    return x * jax.nn.sigmoid(x)


def _rms(x, g):
    return x * lax.rsqrt(jnp.mean(x * x, axis=-1, keepdims=True) + EPS) * g


def _ada_kernel(c_ref, w_ref, b_ref, o_ref):
    a = _silu(c_ref[...]).astype(BF16)
    o_ref[...] = jnp.dot(a, w_ref[...].astype(BF16), preferred_element_type=F32) + b_ref[...]


def _ada(c, w, b):
    rows, d = c.shape
    n = w.shape[1]
    tn = 1024
    return pl.pallas_call(
        _ada_kernel,
        grid=(n // tn,),
        in_specs=[pl.BlockSpec((rows, d), lambda j: (0, 0)),
                  pl.BlockSpec((d, tn), lambda j: (0, j)),
                  pl.BlockSpec((1, tn), lambda j: (0, j))],
        out_specs=pl.BlockSpec((rows, tn), lambda j: (0, j)),
        out_shape=jax.ShapeDtypeStruct((rows, n), F32),
        compiler_params=_params(("arbitrary",)),
        name="ada_mod",
    )(c, w, b)


def _inproj_kernel(x_ref, g_ref, sc_ref, sh_ref, w_hbm, o_ref, w_scr, h_scr, sem):
    @pl.when(pl.program_id(0) == 0)
    def _():
        cp = pltpu.make_async_copy(w_hbm, w_scr, sem)
        cp.start()
        cp.wait()

    h = _rms(x_ref[...], g_ref[...][None]) * (1.0 + sc_ref[...]) + sh_ref[...]
    h_scr[...] = h.reshape(h_scr.shape).astype(BF16)
    for j in range(w_scr.shape[1] // IN_COL_TILE):
        cols = slice(j * IN_COL_TILE, (j + 1) * IN_COL_TILE)
        o_ref[:, cols] = jnp.dot(h_scr[...], w_scr[:, cols], preferred_element_type=F32)


def _inproj(x, g, sc, sh, w, bb, lt):
    bsz, seq, d = x.shape
    rows = bb * lt
    nl = seq // lt
    n = w.shape[1]
    return pl.pallas_call(
        _inproj_kernel,
        grid=(bsz * seq // rows,),
        in_specs=[pl.BlockSpec((bb, lt, d), lambda i: (i // nl, i % nl, 0)),
                  pl.BlockSpec((1, d), lambda i: (0, 0)),
                  pl.BlockSpec((bb, 1, d), lambda i: (i // nl, 0, 0)),
                  pl.BlockSpec((bb, 1, d), lambda i: (i // nl, 0, 0)),
                  pl.BlockSpec(memory_space=pl.ANY)],
        out_specs=pl.BlockSpec((rows, n), lambda i: (i, 0)),
        out_shape=jax.ShapeDtypeStruct((bsz * seq, n), F32),
        scratch_shapes=[pltpu.VMEM(w.shape, BF16), pltpu.VMEM((rows, d), BF16), pltpu.SemaphoreType.DMA(())],
        compiler_params=_params(("arbitrary",)),
        name="in_proj",
    )(x, g, sc, sh, w)


def _split_bf16(x, pieces):
    out = []
    for _ in range(pieces):
        part = x.astype(BF16)
        out.append(part)
        x = x - part.astype(F32)
    return out


def _gla_kernel(q_ref, k_ref, v_ref, gr_ref, ga_ref, wg_ref, bg_ref, gon_ref, tri_b_ref, tri_f_ref, s0_ref,
                o_ref, sout_ref, s_scr, *, n_chunks, n_valid):
    c = GLA_CHUNK
    l_step = pl.program_id(2)

    @pl.when(l_step == 0)
    def _():
        s_scr[...] = s0_ref[...]

    chunks_per_batch = n_chunks // s_scr.shape[0]

    def load(ref):
        x = ref[...]
        if n_valid == c:
            return x
        zeros = jnp.zeros((c - n_valid, x.shape[1]), x.dtype)
        return jnp.concatenate([part for i in range(n_chunks)
                                for part in (x[i * n_valid:(i + 1) * n_valid], zeros)], axis=0)

    q = load(q_ref) * (GLA_DK ** -0.5)
    k = load(k_ref)
    v = load(v_ref).astype(BF16)
    a_hi, a_lo = _split_bf16(load(ga_ref), 2)
    w_hi, w_lo = _split_bf16(wg_ref[...], 2)
    z = jnp.dot(jnp.concatenate([a_hi, a_lo, a_hi], axis=1), jnp.concatenate([w_hi, w_hi, w_lo], axis=0),
                preferred_element_type=F32) + bg_ref[...]
    log_a = (jnp.minimum(z, 0.0) - jnp.log1p(jnp.exp(-jnp.abs(z)))) / GLA_TAU
    if n_valid < c:
        log_a = jnp.where(lax.broadcasted_iota(jnp.int32, log_a.shape, 0) % c < n_valid, log_a, 0.0)
    cum = jnp.dot(tri_b_ref[...], jnp.concatenate(_split_bf16(log_a, 3), axis=1), preferred_element_type=F32)
    b = cum[:, :GLA_DK] + cum[:, GLA_DK:2 * GLA_DK] + cum[:, 2 * GLA_DK:]
    q_dec = (q * jnp.exp(b)).astype(BF16)
    k_inv = (k * jnp.exp(-b)).astype(BF16)
    att = lax.dot_general(q_dec, k_inv, (((1,), (1,)), ((), ())), preferred_element_type=F32)
    att = jnp.where(tri_f_ref[...] > 0.0, att, 0.0).astype(BF16)
    o_intra = jnp.dot(att, v, preferred_element_type=F32)

    states = {}
    outs = []
    for idx in range(n_chunks):
        bi = idx // chunks_per_batch
        rows = slice(idx * c, (idx + 1) * c)
        s_old = states[bi] if bi in states else s_scr[bi]
        b_last = b[(idx + 1) * c - 1:(idx + 1) * c, :]
        o = o_intra[rows] + jnp.dot(q_dec[rows], s_old.astype(BF16), preferred_element_type=F32)
        k_dec = (k[rows] * jnp.exp(b_last - b[rows])).astype(BF16)
        kv = lax.dot_general(k_dec, v[rows], (((0,), (0,)), ((), ())), preferred_element_type=F32)
        decay = jnp.exp(jnp.broadcast_to(b_last, (GLA_DK, GLA_DK)).T)
        states[bi] = jnp.concatenate([decay] * (GLA_DV // GLA_DK), axis=1) * s_old + kv
        outs.append(o[:n_valid])
    for bi, s_new in states.items():
        s_scr[bi] = s_new
    o_all = _rms(jnp.concatenate(outs, axis=0), gon_ref[...])
    o_ref[...] = o_all * _silu(gr_ref[...])

    @pl.when(l_step == pl.num_programs(2) - 1)
    def _():
        sout_ref[...] = s_scr[...]


def _gla(proj, s0, wg_pad, bg, gon, bsz, seq, bb, lt):
    n_valid = math.gcd(seq, GLA_CHUNK)
    rows = bb * lt
    nl = seq // lt
    rmap = lambda b, h, l: b * nl + l
    n_chunks = rows // n_valid
    kern = functools.partial(_gla_kernel, n_chunks=n_chunks, n_valid=n_valid)
    st_spec = pl.BlockSpec((bb, None, GLA_DK, GLA_DV), lambda b, h, l: (b, h, 0, 0))
    pos = jnp.arange(n_chunks * GLA_CHUNK)
    tri = (pos[:, None] // GLA_CHUNK == pos[None, :] // GLA_CHUNK) & (pos[None, :] <= pos[:, None])
    tri_spec = pl.BlockSpec(tri.shape, lambda b, h, l: (0, 0))
    return pl.pallas_call(
        kern,
        grid=(bsz // bb, GLA_HEADS, nl),
        in_specs=[pl.BlockSpec((rows, GLA_DK), lambda b, h, l: (rmap(b, h, l), COL_GQ // GLA_DK + h)),
                  pl.BlockSpec((rows, GLA_DK), lambda b, h, l: (rmap(b, h, l), COL_GK // GLA_DK + h)),
                  pl.BlockSpec((rows, GLA_DV), lambda b, h, l: (rmap(b, h, l), COL_GV // GLA_DV + h)),
                  pl.BlockSpec((rows, GLA_DV), lambda b, h, l: (rmap(b, h, l), COL_GR // GLA_DV + h)),
                  pl.BlockSpec((rows, LANES), lambda b, h, l: (rmap(b, h, l), COL_KR // LANES)),
                  pl.BlockSpec((LANES, GLA_DK), lambda b, h, l: (0, h)),
                  pl.BlockSpec((1, GLA_DK), lambda b, h, l: (0, h)),
                  pl.BlockSpec((1, GLA_DV), lambda b, h, l: (0, 0)),
                  tri_spec, tri_spec, st_spec],
        out_specs=[pl.BlockSpec((rows, GLA_DV), lambda b, h, l: (rmap(b, h, l), h)), st_spec],
        out_shape=[jax.ShapeDtypeStruct((bsz * seq, GLA_WIDTH), F32),
                   jax.ShapeDtypeStruct((bsz, GLA_HEADS, GLA_DK, GLA_DV), F32)],
        scratch_shapes=[pltpu.VMEM((bb, GLA_DK, GLA_DV), F32)],
        compiler_params=_params(("arbitrary", "arbitrary", "arbitrary")),
        name="gla",
    )(proj, proj, proj, proj, proj, wg_pad, bg, gon, tri.astype(BF16), tri.astype(F32), s0)


def _rope(r, cos, sin_lo, sin_hi):
    return (r * cos + pltpu.roll(r, LANES - MLA_ROPE // 2, axis=1) * sin_lo
            + pltpu.roll(r, MLA_ROPE // 2, axis=1) * sin_hi)


def _mla_proj_kernel(cq_ref, ckv_ref, kr_ref, cos_ref, slo_ref, shi_ref, gq_ref, gkv_ref, wq_ref,
                     wk_ref, wv_ref, *out_refs, absorbed):
    cos, slo, shi = cos_ref[...], slo_ref[...], shi_ref[...]
    cqn = _rms(cq_ref[...], gq_ref[...]).astype(BF16)
    qf = jnp.dot(cqn, wq_ref[...], preferred_element_type=F32)
    ckv_n = _rms(ckv_ref[...], gkv_ref[...])
    k_rot = _rope(kr_ref[...], cos, slo, shi)
    if absorbed:
        qlat_ref, qr_ref, ckvn_ref, krope_ref = out_refs
    else:
        qcat_ref, kcat_ref, v_ref, ckvn_ref, krope_ref = out_refs
    ckvn_ref[...] = ckv_n
    krope_ref[...] = k_rot[:, :MLA_ROPE]
    for h in range(MLA_HEADS):
        q_nope = qf[:, h * QK_PAD:h * QK_PAD + LANES] * ATTN_SCALE
        q_rot = _rope(qf[:, h * QK_PAD + LANES:(h + 1) * QK_PAD], cos, slo, shi) * ATTN_SCALE
        if absorbed:
            qlat_ref[h] = jnp.dot(q_nope.astype(BF16), wk_ref[h], preferred_element_type=F32)
            qr_ref[h] = q_rot
        else:
            qcat_ref[:, h * QK_PAD:h * QK_PAD + LANES] = q_nope.astype(BF16)
            qcat_ref[:, h * QK_PAD + LANES:(h + 1) * QK_PAD] = q_rot.astype(BF16)
    if not absorbed:
        ckv_b = ckv_n.astype(BF16)
        k_nope = jnp.dot(ckv_b, wk_ref[...], preferred_element_type=F32)
        k_rot_b = k_rot.astype(BF16)
        for h in range(MLA_HEADS):
            kcat_ref[:, h * QK_PAD:h * QK_PAD + LANES] = k_nope[:, h * LANES:(h + 1) * LANES].astype(BF16)
            kcat_ref[:, h * QK_PAD + LANES:(h + 1) * QK_PAD] = k_rot_b
        v_ref[...] = jnp.dot(ckv_b, wv_ref[...], preferred_element_type=F32).astype(BF16)


def _mla_proj(proj, tabs, gq, gkv, wq, wk, wv, tm, absorbed):
    t = proj.shape[0]
    ntab = tabs[0].shape[0] // tm
    row = lambda c: pl.BlockSpec((tm, c), lambda i: (i, 0))
    full = lambda a: pl.BlockSpec(a.shape, lambda i: (0,) * a.ndim)
    tab = pl.BlockSpec((tm, LANES), lambda i: (i % ntab, 0))
    in_specs = [pl.BlockSpec((tm, MLA_Q_RANK), lambda i: (i, COL_CQ // MLA_Q_RANK)),
                pl.BlockSpec((tm, MLA_KV_RANK), lambda i: (i, COL_CKV // MLA_KV_RANK)),
                pl.BlockSpec((tm, LANES), lambda i: (i, COL_KR // LANES)),
                tab, tab, tab, full(gq), full(gkv), full(wq), full(wk), full(wv)]
    if absorbed:
        out_specs = [pl.BlockSpec((MLA_HEADS, tm, MLA_KV_RANK), lambda i: (0, i, 0)),
                     pl.BlockSpec((MLA_HEADS, tm, LANES), lambda i: (0, i, 0))]
        out_shape = [jax.ShapeDtypeStruct((MLA_HEADS, t, MLA_KV_RANK), F32),
                     jax.ShapeDtypeStruct((MLA_HEADS, t, LANES), F32)]
    else:
        out_specs = [row(MLA_HEADS * QK_PAD), row(MLA_HEADS * QK_PAD), row(MLA_WIDTH)]
        out_shape = [jax.ShapeDtypeStruct((t, MLA_HEADS * QK_PAD), BF16),
                     jax.ShapeDtypeStruct((t, MLA_HEADS * QK_PAD), BF16),
                     jax.ShapeDtypeStruct((t, MLA_WIDTH), BF16)]
    out_specs += [row(MLA_KV_RANK), row(MLA_ROPE)]
    out_shape += [jax.ShapeDtypeStruct((t, MLA_KV_RANK), F32), jax.ShapeDtypeStruct((t, MLA_ROPE), F32)]
    return pl.pallas_call(
        functools.partial(_mla_proj_kernel, absorbed=absorbed),
        grid=(t // tm,),
        in_specs=in_specs, out_specs=out_specs, out_shape=out_shape,
        compiler_params=_params(("arbitrary",)),
        name="mla_proj_absorbed" if absorbed else "mla_proj",
    )(proj, proj, proj, *tabs, gq, gkv, wq, wk, wv)


PROMPT_TILE = 512
VISIT_FIRST, VISIT_DIAGONAL = 1, 2


def _prompt_schedule(n_streams, n_blocks):
    q_blk, k_blk, flags = [], [], []
    for _ in range(n_streams):
        for pair in range(n_blocks // 2):
            for qi in (pair, n_blocks - 1 - pair):
                for j in range(qi + 1):
                    q_blk.append(qi)
                    k_blk.append(j)
                    flags.append((VISIT_FIRST if j == 0 else 0) | (VISIT_DIAGONAL if j == qi else 0))
    as_i32 = lambda v: jnp.asarray(v, jnp.int32)
    return as_i32(q_blk), as_i32(k_blk), as_i32(flags), len(q_blk) // n_streams


def _prompt_visits(g0, n_visits, vq_ref, vk_ref, vf_ref, q_ref, k_ref, v_ref, gon_ref, o_ref,
                   m_scr, l_scr, acc_scr):
    t = PROMPT_TILE
    def visit(i, carry):
        m, l, acc = carry
        qi, kj, flag = vq_ref[g0 + i], vk_ref[g0 + i], vf_ref[g0 + i]
        q0 = pl.multiple_of(qi * t, t)
        k0 = pl.multiple_of(kj * t, t)
        first = (flag & VISIT_FIRST) != 0
        diagonal = (flag & VISIT_DIAGONAL) != 0
        m = jnp.where(first, -jnp.inf, m)
        l = jnp.where(first, 0.0, l)
        acc = jnp.where(first, 0.0, acc)
        s = lax.dot_general(q_ref[pl.ds(q0, t), :], k_ref[pl.ds(k0, t), :], (((1,), (1,)), ((), ())),
                            preferred_element_type=F32)

        def masked(x):
            row = lax.broadcasted_iota(jnp.int32, (t, t), 0)
            col = lax.broadcasted_iota(jnp.int32, (t, t), 1)
            return jnp.where(col <= row, x, -jnp.inf)

        s = lax.cond(diagonal, masked, lambda x: x, s)
        m_new = jnp.maximum(m, jnp.max(s, axis=-1, keepdims=True))
        alpha = jnp.exp(m - m_new)
        p = jnp.exp(s - m_new)
        l = alpha * l + jnp.sum(p, axis=-1, keepdims=True)
        acc = alpha * acc + jnp.dot(p.astype(BF16), v_ref[pl.ds(k0, t), :], preferred_element_type=F32)

        @pl.when(diagonal)
        def _():
            o_ref[pl.ds(q0, t), :] = _rms(acc / l, gon_ref[...])
        return m_new, l, acc

    m, l, acc = lax.fori_loop(0, n_visits, visit, (m_scr[...], l_scr[...], acc_scr[...]))
    m_scr[...], l_scr[...], acc_scr[...] = m, l, acc


def _attention_kernel(pt_ref, vq_ref, vk_ref, vf_ref, v0_ref, qlat_ref, qr_ref, cnew_ref, knew_ref, ckv_hbm,
                      krt_hbm, qp_ref, kp_ref, vp_ref, gon_ref, o_ref, op_ref,
                      ckv_buf, krt_buf, sems, m_scr, l_scr, acc_scr, pm_scr, pl_scr, pacc_scr, *, pp, n_q):
    grp = pl.program_id(1)
    n_grp = pl.num_programs(1)
    step = pl.program_id(0) * n_grp + grp
    n_steps = pl.num_programs(0) * n_grp
    slot = step % 2
    rows = MLA_HEADS * n_q

    def page_copies(at_step, at_slot):
        out = []
        for j in range(pp):
            page = pt_ref[at_step * pp + j]
            out.append(pltpu.make_async_copy(ckv_hbm.at[0, page], ckv_buf.at[at_slot, j], sems.at[0, at_slot]))
            out.append(pltpu.make_async_copy(krt_hbm.at[0, page],
                                             krt_buf.at[at_slot, :, pl.ds(j * PAGE_SIZE, PAGE_SIZE)],
                                             sems.at[1, at_slot]))
        return out

    def start_pages(at_step, at_slot):
        for i, cp in enumerate(page_copies(at_step, at_slot)):
            cp.start(priority=(i // 2 + i) % DMA_THREADS)

    @pl.when(step == 0)
    def _():
        start_pages(step, slot)
        pm_scr[...] = jnp.zeros(pm_scr.shape, F32)
        pl_scr[...] = jnp.zeros(pl_scr.shape, F32)
        pacc_scr[...] = jnp.zeros(pacc_scr.shape, F32)

    @pl.when(step + 1 < n_steps)
    def _():
        start_pages(step + 1, 1 - slot)

    @pl.when(grp == 0)
    def _():
        m_scr[...] = jnp.full(m_scr.shape, -jnp.inf, F32)
        l_scr[...] = jnp.zeros(l_scr.shape, F32)
        acc_scr[...] = jnp.zeros(acc_scr.shape, F32)

    q = qlat_ref[...].reshape(rows, MLA_KV_RANK).astype(BF16)
    qr = qr_ref[...].reshape(rows, LANES)[:, :MLA_ROPE].astype(BF16)

    def update(s, v):
        m = m_scr[...]
        m_new = jnp.maximum(m, jnp.max(s, axis=-1, keepdims=True))
        alpha = jnp.exp(m - m_new)
        p = jnp.exp(s - m_new)
        l_scr[...] = alpha * l_scr[...] + jnp.sum(p, axis=-1, keepdims=True)
        acc_scr[...] = acc_scr[...] * alpha + jnp.dot(p.astype(BF16), v, preferred_element_type=F32)
        m_scr[...] = m_new

    _prompt_visits(v0_ref[step], v0_ref[step + 1] - v0_ref[step], vq_ref, vk_ref, vf_ref, qp_ref, kp_ref, vp_ref,
                   gon_ref, op_ref, pm_scr, pl_scr, pacc_scr)

    for cp in page_copies(step, slot):
        cp.wait()
    ckv = ckv_buf[slot].reshape(pp * PAGE_SIZE, MLA_KV_RANK).astype(BF16)
    s_past = (lax.dot_general(q, ckv, (((1,), (1,)), ((), ())), preferred_element_type=F32)
              + jnp.dot(qr, krt_buf[slot].astype(BF16), preferred_element_type=F32))
    update(s_past, ckv)

    @pl.when(grp == n_grp - 1)
    def _():
        pad = PAGE_SIZE - n_q
        c_new = jnp.concatenate([cnew_ref[...], jnp.zeros((pad, MLA_KV_RANK), F32)], axis=0).astype(BF16)
        k_new = jnp.concatenate([knew_ref[...], jnp.zeros((pad, MLA_ROPE), F32)], axis=0).astype(BF16)
        s = (lax.dot_general(q, c_new, (((1,), (1,)), ((), ())), preferred_element_type=F32)
             + lax.dot_general(qr, k_new, (((1,), (1,)), ((), ())), preferred_element_type=F32))
        q_idx = lax.broadcasted_iota(jnp.int32, s.shape, 0) % n_q
        k_idx = lax.broadcasted_iota(jnp.int32, s.shape, 1)
        update(jnp.where(k_idx <= q_idx, s, -jnp.inf), c_new)
        o_ref[...] = (acc_scr[...] / l_scr[...]).reshape(o_ref.shape)


def _attention(q_lat, q_r, ckv_n, k_rope, cache_ckv, cache_krope_t, page_table, n_q, pp,
               q_cat, k_cat, v_p, g_onorm, bsz_p, seq_p):
    bsz, n_pages = page_table.shape
    n_grp = n_pages // pp
    n_steps = bsz * n_grp
    rows = MLA_HEADS * n_q
    n_streams = bsz_p * MLA_HEADS
    vq, vk, vf, per_stream = _prompt_schedule(n_streams, seq_p // PROMPT_TILE)
    steps_per_stream = n_steps // n_streams
    assert steps_per_stream * n_streams == n_steps
    v0 = jnp.asarray([(i // steps_per_stream) * per_stream + (i % steps_per_stream) * per_stream // steps_per_stream
                      for i in range(n_steps)] + [n_streams * per_stream], jnp.int32)

    def stream(b, s, *_):
        bh = (b * n_grp + s) // steps_per_stream
        return bh // MLA_HEADS, bh % MLA_HEADS

    grid_spec = pltpu.PrefetchScalarGridSpec(
        num_scalar_prefetch=5,
        grid=(bsz, n_grp),
        in_specs=[pl.BlockSpec((MLA_HEADS, n_q, MLA_KV_RANK), lambda b, s, *_: (0, b, 0)),
                  pl.BlockSpec((MLA_HEADS, n_q, LANES), lambda b, s, *_: (0, b, 0)),
                  pl.BlockSpec((n_q, MLA_KV_RANK), lambda b, s, *_: (b, 0)),
                  pl.BlockSpec((n_q, MLA_ROPE), lambda b, s, *_: (b, 0)),
                  pl.BlockSpec(memory_space=pl.ANY),
                  pl.BlockSpec(memory_space=pl.ANY),
                  pl.BlockSpec((seq_p, QK_PAD), stream),
                  pl.BlockSpec((seq_p, QK_PAD), stream),
                  pl.BlockSpec((seq_p, MLA_DV), stream),
                  pl.BlockSpec((1, MLA_DV), lambda b, s, *_: (0, 0))],
        out_specs=[pl.BlockSpec((MLA_HEADS, n_q, MLA_KV_RANK), lambda b, s, *_: (0, b, 0)),
                   pl.BlockSpec((seq_p, MLA_DV), stream)],
        scratch_shapes=[pltpu.VMEM((2, pp, PAGE_SIZE, MLA_KV_RANK), F32),
                        pltpu.VMEM((2, MLA_ROPE, pp * PAGE_SIZE), F32),
                        pltpu.SemaphoreType.DMA((2, 2)),
                        pltpu.VMEM((rows, 1), F32), pltpu.VMEM((rows, 1), F32),
                        pltpu.VMEM((rows, MLA_KV_RANK), F32),
                        pltpu.VMEM((PROMPT_TILE, 1), F32), pltpu.VMEM((PROMPT_TILE, 1), F32),
                        pltpu.VMEM((PROMPT_TILE, MLA_DV), F32)],
    )
    return pl.pallas_call(
        functools.partial(_attention_kernel, pp=pp, n_q=n_q),
        grid_spec=grid_spec,
        out_shape=[jax.ShapeDtypeStruct((MLA_HEADS, bsz * n_q, MLA_KV_RANK), F32),
                   jax.ShapeDtypeStruct((bsz_p * seq_p, MLA_WIDTH), F32)],
        compiler_params=_params(("arbitrary", "arbitrary")),
        name="attention",
    )(page_table.reshape(-1), vq, vk, vf, v0, q_lat, q_r, ckv_n, k_rope, cache_ckv, cache_krope_t,
      q_cat, k_cat, v_p, g_onorm)


def _latent_out_kernel(o_ref, w_ref, g_ref, out_ref):
    o = jnp.dot(o_ref[...].astype(BF16), w_ref[...], preferred_element_type=F32)
    out_ref[...] = _rms(o, g_ref[...])


def _latent_out(o_lat, w_uv_h, g, tm):
    t = o_lat.shape[1]
    return pl.pallas_call(
        _latent_out_kernel,
        grid=(MLA_HEADS, t // tm),
        in_specs=[pl.BlockSpec((None, tm, MLA_KV_RANK), lambda h, i: (h, i, 0)),
                  pl.BlockSpec((None, MLA_KV_RANK, MLA_DV), lambda h, i: (h, 0, 0)),
                  pl.BlockSpec((1, MLA_DV), lambda h, i: (0, 0))],
        out_specs=pl.BlockSpec((tm, MLA_DV), lambda h, i: (i, h)),
        out_shape=jax.ShapeDtypeStruct((t, MLA_WIDTH), F32),
        compiler_params=_params(("arbitrary", "arbitrary")),
        name="latent_out",
    )(o_lat, w_uv_h, g)


def _route(logit):
    lane = lax.broadcasted_iota(jnp.int32, logit.shape, 1)
    neg = -jnp.inf
    big = jnp.int32(LANES)

    def first_max(vals):
        top = jnp.max(vals, axis=-1, keepdims=True)
        return top, jnp.min(jnp.where(vals == top, lane, big), axis=-1, keepdims=True)

    g_vals = jnp.where(lane < N_GROUPS, logit, neg)
    g_max, g_idx = first_max(g_vals)
    g_w = 1.0 / jnp.sum(jnp.exp(g_vals - g_max), axis=-1, keepdims=True)
    e_lane = lane - N_GROUPS
    in_group = (e_lane >= g_idx * EXPERTS_PER_GROUP) & (e_lane < (g_idx + 1) * EXPERTS_PER_GROUP)
    e_vals = jnp.where(in_group, logit, neg)
    v1, i1 = first_max(e_vals)
    v2, i2 = first_max(jnp.where(lane == i1, neg, e_vals))
    e2 = jnp.exp(v2 - v1)
    w1 = g_w / (1.0 + e2)
    w2 = g_w * e2 / (1.0 + e2)
    ids = jnp.where(lane == 0, i1 - N_GROUPS, jnp.where(lane == 1, i2 - N_GROUPS, 0))
    wts = jnp.where(lane == 0, w1, jnp.where(lane == 1, w2, 0.0))
    return ids, wts


def _outproj_kernel(og_ref, om_ref, wg_ref, wm_ref, x_ref, gt_ref, sc_ref, sh_ref, g_ref, wr_ref, br_ref,
                    x1_ref, h2_ref, id_ref, wt_ref):
    y = (jnp.dot(og_ref[...].astype(BF16), wg_ref[...], preferred_element_type=F32)
         + jnp.dot(om_ref[...].astype(BF16), wm_ref[...], preferred_element_type=F32))
    x1 = x_ref[...] + gt_ref[...] * y.reshape(x_ref.shape)
    x1_ref[...] = x1
    h2 = (_rms(x1, g_ref[...][None]) * (1.0 + sc_ref[...]) + sh_ref[...]).reshape(h2_ref.shape)
    h2_ref[...] = h2
    rows = h2.shape[0]
    lg = jnp.dot(jnp.concatenate(_split_bf16(h2, 2), axis=0), wr_ref[...], preferred_element_type=F32)
    logit = (lg[:rows, :LANES] + lg[:rows, LANES:] + lg[rows:, :LANES] + lg[rows:, LANES:]) + br_ref[...]
    id_ref[...], wt_ref[...] = _route(logit)


def _outproj(o_gla, o_mla, w_top, w_bot, x, gt, sc, sh, g, w_r, b_r, bb, lt):
    bsz, seq, d = x.shape
    rows = bb * lt
    nl = seq // lt
    xs = pl.BlockSpec((bb, lt, d), lambda i: (i // nl, i % nl, 0))
    ms = pl.BlockSpec((bb, 1, d), lambda i: (i // nl, 0, 0))
    full = lambda a: pl.BlockSpec(a.shape, lambda i: (0,) * a.ndim)
    wide = pl.BlockSpec((rows, LANES), lambda i: (i, 0))
    return pl.pallas_call(
        _outproj_kernel,
        grid=(bsz * seq // rows,),
        in_specs=[pl.BlockSpec((rows, GLA_WIDTH), lambda i: (i, 0)),
                  pl.BlockSpec((rows, MLA_WIDTH), lambda i: (i, 0)),
                  full(w_top), full(w_bot), xs, ms, ms, ms, full(g), full(w_r), full(b_r)],
        out_specs=[xs, pl.BlockSpec((rows, d), lambda i: (i, 0)), wide, wide],
        out_shape=[jax.ShapeDtypeStruct(x.shape, F32), jax.ShapeDtypeStruct((bsz * seq, d), F32),
                   jax.ShapeDtypeStruct((bsz * seq, LANES), jnp.int32),
                   jax.ShapeDtypeStruct((bsz * seq, LANES), F32)],
        compiler_params=_params(("arbitrary",)),
        name="out_proj_route",
    )(o_gla, o_mla, w_top, w_bot, x, gt, sc, sh, g, w_r, b_r)


def _moe_kernel(ie_ref, ir_ref, in_ref, im_ref, nl_ref, src_ref, dst_ref,
                hp_hbm, hs_hbm, wgu_hbm, wd_hbm, y_hbm,
                x_buf, xb_scr, wb_scr, act_scr, y_buf, ring, gsem, ssem, wsem):
    n_live = nl_ref[0]
    last_item = ie_ref.shape[0] - 1

    def chunk_width(j):
        return min(MOE_HALF, EXPERT_FF - j * MOE_HALF)

    def weight_copies(e, j, slot):
        sem = wsem.at[slot]
        if j < MOE_GU_CHUNKS:
            w = chunk_width(j)
            return [pltpu.make_async_copy(wgu_hbm.at[e, :, pl.ds(off + j * MOE_HALF, w)],
                                          ring.at[slot, :, pl.ds(half * w, w)], sem)
                    for half, off in enumerate((0, EXPERT_FF))]
        c = j - MOE_GU_CHUNKS
        return [pltpu.make_async_copy(wd_hbm.at[e, :, pl.ds(c * MOE_COLS, MOE_COLS)],
                                      ring.at[slot, pl.ds(0, EXPERT_FF)], sem)]

    def start_weights(e, j, slot):
        for idx, cp in enumerate(weight_copies(e, j, slot)):
            cp.start(priority=(j + idx) % DMA_THREADS)

    def gather_copy(h_hbm, src_row, p, r, k=1):
        return pltpu.make_async_copy(h_hbm.at[pl.ds(src_row, k)], x_buf.at[p, pl.ds(r, k)], gsem.at[p])

    def scatter_copy(r, dst_row, k=1):
        return pltpu.make_async_copy(y_buf.at[pl.ds(r, k)], y_hbm.at[pl.ds(dst_row, k)], ssem)

    def for_rows(lo, hi, fn, unroll=4):
        groups = (hi - lo) // unroll

        def group(g, c):
            for u in range(unroll):
                fn(lo + g * unroll + u)
            return c

        def single(r, c):
            fn(r)
            return c

        lax.fori_loop(0, groups, group, 0)
        lax.fori_loop(lo + groups * unroll, hi, single, 0)

    def wait_rows(n, copy_of_rows):
        k = 1
        while k <= MOE_ROWS:
            @pl.when((n & k) != 0)
            def _(k=k):
                copy_of_rows(k).wait()
            k *= 2

    def start_gather(i, p):
        row0, n, m = ir_ref[i], in_ref[i], im_ref[i]
        for_rows(0, m, lambda r: gather_copy(hp_hbm, src_ref[row0 + r], p, r).start())
        for_rows(m, n, lambda r: gather_copy(hs_hbm, src_ref[row0 + r], p, r).start())

    x_buf[...] = jnp.zeros(x_buf.shape, F32)
    start_gather(0, 0)
    for j in range(MOE_RING):
        start_weights(ie_ref[0], j, j)

    def item_body(i, carry):
        p = i % 2
        e, row0, n = ie_ref[i], ir_ref[i], in_ref[i]
        n_sub = (n + MOE_SUB - 1) // MOE_SUB
        has_next = i + 1 < n_live
        e_next = ie_ref[jnp.minimum(i + 1, last_item)]
        base = (i * MOE_CHUNKS) % MOE_RING

        wait_rows(n, lambda k: gather_copy(hp_hbm, 0, p, 0, k))

        @pl.when(has_next)
        def _():
            start_gather(i + 1, 1 - p)

        def for_row_blocks(fn):
            pairs = n_sub // 2

            def pair(b, c):
                fn(pl.multiple_of(b * 2 * MOE_SUB, 2 * MOE_SUB), 2 * MOE_SUB)
                return c
            lax.fori_loop(0, pairs, pair, 0)

            @pl.when(n_sub % 2 == 1)
            def _():
                fn(pl.multiple_of(pairs * 2 * MOE_SUB, 2 * MOE_SUB), MOE_SUB)

        def cast_rows(sb, c):
            r0 = pl.multiple_of(sb * MOE_SUB, MOE_SUB)
            xb_scr[pl.ds(r0, MOE_SUB), :] = x_buf[p, pl.ds(r0, MOE_SUB), :].astype(BF16)
            return c
        lax.fori_loop(0, n_sub, cast_rows, 0)

        for j in range(MOE_CHUNKS):
            slot = (base + j) % MOE_RING
            if j == MOE_GU_CHUNKS:
                @pl.when(i > 0)
                def _():
                    n_prev = in_ref[jnp.maximum(i - 1, 0)]
                    wait_rows(n_prev, lambda k: scatter_copy(0, 0, k))
            for cp in weight_copies(e, j, slot):
                cp.wait()
            if j < MOE_GU_CHUNKS:
                cols = 2 * chunk_width(j)
                wb_scr[:, :cols] = ring[slot, :, pl.ds(0, cols)].astype(BF16)
            else:
                wb_scr[:EXPERT_FF, :] = ring[slot, pl.ds(0, EXPERT_FF), :].astype(BF16)
            j_ahead = j + MOE_RING
            if j_ahead < MOE_CHUNKS:
                start_weights(e, j_ahead, slot)
            else:
                @pl.when(has_next)
                def _(j_ahead=j_ahead, slot=slot):
                    start_weights(e_next, j_ahead - MOE_CHUNKS, slot)

            if j < MOE_GU_CHUNKS:
                w = chunk_width(j)

                def gate_up(r0, rows, j=j, w=w):
                    res = jnp.dot(xb_scr[pl.ds(r0, rows), :], wb_scr[:, :2 * w], preferred_element_type=F32)
                    act = _silu(res[:, :w]) * res[:, w:]
                    act_scr[pl.ds(r0, rows), j * MOE_HALF:j * MOE_HALF + w] = act.astype(BF16)
                for_row_blocks(gate_up)
            else:
                col0 = (j - MOE_GU_CHUNKS) * MOE_COLS

                def down(r0, rows, col0=col0):
                    y_buf[pl.ds(r0, rows), col0:col0 + MOE_COLS] = jnp.dot(
                        act_scr[pl.ds(r0, rows), :], wb_scr[:EXPERT_FF, :], preferred_element_type=F32)
                for_row_blocks(down)

        for_rows(0, n, lambda r: scatter_copy(r, dst_ref[row0 + r]).start())
        return carry

    lax.fori_loop(0, n_live, item_body, 0)
    wait_rows(in_ref[jnp.maximum(n_live - 1, 0)], lambda k: scatter_copy(0, 0, k))


def _moe(plan, h2_p, h2_s, w_gu, w_down):
    d = h2_p.shape[1]
    t_all = h2_p.shape[0] + h2_s.shape[0]
    hbm = pl.BlockSpec(memory_space=pl.ANY)
    grid_spec = pltpu.PrefetchScalarGridSpec(
        num_scalar_prefetch=len(plan),
        grid=(1,),
        in_specs=[hbm, hbm, hbm, hbm],
        out_specs=hbm,
        scratch_shapes=[pltpu.VMEM((2, MOE_ROWS, d), F32),
                        pltpu.VMEM((MOE_ROWS, d), BF16),
                        pltpu.VMEM((d, MOE_COLS), BF16),
                        pltpu.VMEM((MOE_ROWS, EXPERT_FF), BF16),
                        pltpu.VMEM((MOE_ROWS, d), F32),
                        pltpu.VMEM((MOE_RING, d, MOE_COLS), F32),
                        pltpu.SemaphoreType.DMA((2,)),
                        pltpu.SemaphoreType.DMA(()),
                        pltpu.SemaphoreType.DMA((MOE_RING,))],
    )
    return pl.pallas_call(
        _moe_kernel,
        grid_spec=grid_spec,
        out_shape=jax.ShapeDtypeStruct((t_all * TOP_K, d), F32),
        compiler_params=_params(("arbitrary",)),
        name="moe_experts",
    )(*plan, h2_p, h2_s, w_gu, w_down)


def _moe_plan(e_id, t_prompt):
    n_as = e_id.shape[0]
    n_items = N_EXPERTS + n_as // MOE_ROWS
    order = jnp.argsort(e_id).astype(jnp.int32)
    cnt = jnp.zeros((N_EXPERTS,), jnp.int32).at[e_id].add(1)
    cnt_p = jnp.zeros((N_EXPERTS,), jnp.int32).at[e_id[:t_prompt * TOP_K]].add(1)
    ustart = jnp.cumsum(cnt) - cnt
    per_e = (cnt + MOE_ROWS - 1) // MOE_ROWS
    item_end = jnp.cumsum(per_e)
    total = item_end[-1]
    idx = jnp.arange(n_items, dtype=jnp.int32)
    live = idx < total
    e_of = jnp.searchsorted(item_end, jnp.minimum(idx, total - 1), side='right').astype(jnp.int32)
    local = jnp.minimum(idx, total - 1) - (item_end[e_of] - per_e[e_of])
    row0 = ustart[e_of] + local * MOE_ROWS
    n_rows = jnp.where(live, jnp.minimum(MOE_ROWS, cnt[e_of] - local * MOE_ROWS), 0)
    m_rows = jnp.clip(cnt_p[e_of] - local * MOE_ROWS, 0, n_rows)
    tok_s = order // TOP_K
    src_s = jnp.where(tok_s < t_prompt, tok_s, tok_s - t_prompt)
    dst_s = (order % TOP_K) * (n_as // TOP_K) + tok_s
    i32 = lambda a: a.astype(jnp.int32)
    return (i32(e_of), i32(row0), i32(n_rows), i32(m_rows), i32(total)[None], i32(src_s), i32(dst_s))


def _final_kernel(x_ref, gt_ref, y0_ref, y1_ref, wt_ref, g_ref, o_ref):
    wt = wt_ref[...]
    y = wt[:, 0:1] * y0_ref[...] + wt[:, 1:2] * y1_ref[...]
    x2 = x_ref[...] + gt_ref[...] * y.reshape(x_ref.shape)
    o_ref[...] = _rms(x2, g_ref[...][None])


def _final(x1, gt, y_rows, wt, g, bb, lt, row_block0):
    bsz, seq, d = x1.shape
    rows = bb * lt
    nl = seq // lt
    choice_blocks = y_rows.shape[0] // TOP_K // rows
    xs = pl.BlockSpec((bb, lt, d), lambda i: (i // nl, i % nl, 0))
    return pl.pallas_call(
        _final_kernel,
        grid=(bsz * seq // rows,),
        in_specs=[xs, pl.BlockSpec((bb, 1, d), lambda i: (i // nl, 0, 0)),
                  pl.BlockSpec((rows, d), lambda i: (row_block0 + i, 0)),
                  pl.BlockSpec((rows, d), lambda i: (choice_blocks + row_block0 + i, 0)),
                  pl.BlockSpec((rows, LANES), lambda i: (i, 0)),
                  pl.BlockSpec((1, d), lambda i: (0, 0))],
        out_specs=xs,
        out_shape=jax.ShapeDtypeStruct(x1.shape, F32),
        compiler_params=_params(("arbitrary",)),
        name="moe_combine_final_norm",
    )(x1, gt, y_rows, y_rows, wt, g)


def _rope_tables(pos, rows):
    half = MLA_ROPE // 2
    inv_freq = ROPE_BASE ** (-jnp.arange(half, dtype=F32) / half)
    ang = pos.astype(F32)[:, None] * inv_freq[None, :]
    cos, sin = jnp.cos(ang), jnp.sin(ang)
    zero = jnp.zeros_like(cos)
    pad = jnp.zeros((pos.shape[0], LANES - MLA_ROPE), F32)
    tabs = (jnp.concatenate([cos, cos, pad], axis=1),
            jnp.concatenate([-sin, zero, pad], axis=1),
            jnp.concatenate([zero, sin, pad], axis=1))
    reps = rows // pos.shape[0]
    return tuple(jnp.tile(t, (reps, 1)) for t in tabs)


def _arrange_w_in(w_in):
    d = w_in.shape[0]
    off_gv = 2 * GLA_QK
    off_ga = off_gv + GLA_WIDTH
    off_gr = off_ga + GLA_GATE_RANK
    off_cq = off_gr + GLA_WIDTH
    off_ckv = off_cq + MLA_Q_RANK
    off_kr = off_ckv + MLA_KV_RANK
    pad = jnp.zeros((d, LANES - MLA_ROPE - GLA_GATE_RANK), w_in.dtype)
    return jnp.concatenate([w_in[:, :off_ga], w_in[:, off_gr:off_kr + MLA_ROPE], w_in[:, off_ga:off_gr], pad],
                           axis=1).astype(BF16)


def _arrange_w_uq(w_uq):
    r = w_uq.shape[0]
    pad = jnp.zeros((r, MLA_HEADS, QK_PAD - MLA_NOPE - MLA_ROPE), w_uq.dtype)
    return jnp.concatenate([w_uq, pad], axis=2).reshape(r, MLA_HEADS * QK_PAD).astype(BF16)


MLA_ROW_TILE = 256
PAGES_PER_STEP = 32


def _pre_attention(x, mods, pos, s0, w, prompt):
    bsz, seq, _ = x.shape
    sh1, sc1 = mods[0], mods[1]
    bb, lt = (1, 512) if prompt else (512 // seq, seq)
    proj = _inproj(x, w['g_mix'], sc1, sh1, w['w_in'], bb, lt)
    gbb, glt = (1, 512) if prompt else (8, seq)
    o_gla, s_fin = _gla(proj, s0, w['wg_pad'], w['bg'], w['g_gla'], bsz, seq, gbb, glt)
    tabs = _rope_tables(pos, seq if prompt else MLA_ROW_TILE)
    *attn_in, ckv_n, k_rope = _mla_proj(proj, tabs, w['g_q'], w['g_kv'], w['w_uq'],
                                        w['w_uk'] if prompt else w['w_uk_t'], w['w_uv'], MLA_ROW_TILE,
                                        absorbed=not prompt)
    return o_gla, attn_in, ckv_n, k_rope, s_fin


def _post_attention(x, mods, o_gla, o_mla, w, prompt):
    seq = x.shape[1]
    _, _, gt1, sh2, sc2, _ = mods
    bb, lt = (1, 256) if prompt else (256 // seq, seq)
    return _outproj(o_gla, o_mla, w['w_out_top'], w['w_out_bot'], x, gt1, sc2, sh2, w['g_ffn'],
                    w['w_r'], w['b_r'], bb, lt)


def kernel(x_prompt, x_sample, c_prompt, c_sample, cache_ckv, cache_krope, state_gla, page_table, w_ada, b_ada, g_mix, g_ffn, w_in, gla_w_gate, gla_b_gate, gla_g_onorm, mla_g_qnorm, mla_w_uq, mla_g_kvnorm, mla_w_uk, mla_w_uv, mla_g_onorm, w_out, moe_w_group, moe_b_group, moe_w_expert, moe_b_expert, moe_w_gu, moe_w_down, g_final):
    depth = w_ada.shape[0]
    assert depth == 1
    bp, lp, d = x_prompt.shape
    bs, ls, _ = x_sample.shape
    n_past = page_table.shape[1] * PAGE_SIZE
    lyr = 0

    n_c = bp + bs
    c_rows = -(-n_c // 8) * 8
    c_all = jnp.concatenate([c_prompt, c_sample, jnp.zeros((c_rows - n_c, d), F32)], axis=0)
    mod = _ada(c_all, w_ada[lyr], b_ada[lyr][None])
    mods_p = tuple(m[:bp, None, :] for m in jnp.split(mod, 6, axis=-1))
    mods_s = tuple(m[bp:n_c, None, :] for m in jnp.split(mod, 6, axis=-1))

    wg_pad = jnp.zeros((LANES, GLA_QK), F32).at[GA_LANE:GA_LANE + GLA_GATE_RANK].set(gla_w_gate[lyr])
    w_uk = mla_w_uk[lyr]
    w_uv = mla_w_uv[lyr]
    w = {
        'g_mix': g_mix[lyr][None], 'g_ffn': g_ffn[lyr][None],
        'w_in': _arrange_w_in(w_in[lyr]),
        'wg_pad': wg_pad, 'bg': gla_b_gate[lyr][None], 'g_gla': gla_g_onorm[lyr][None],
        'g_q': mla_g_qnorm[lyr][None], 'g_kv': mla_g_kvnorm[lyr][None], 'g_mla': mla_g_onorm[lyr][None],
        'w_uq': _arrange_w_uq(mla_w_uq[lyr]),
        'w_uk': w_uk.reshape(MLA_KV_RANK, MLA_HEADS * MLA_NOPE).astype(BF16),
        'w_uk_t': w_uk.transpose(1, 2, 0).astype(BF16),
        'w_uv': w_uv.reshape(MLA_KV_RANK, MLA_WIDTH).astype(BF16),
        'w_uv_h': w_uv.transpose(1, 0, 2).astype(BF16),
        'w_out_top': w_out[lyr][:GLA_WIDTH].astype(BF16),
        'w_out_bot': w_out[lyr][GLA_WIDTH:].astype(BF16),
    }
    w_r = jnp.concatenate([moe_w_group[lyr], moe_w_expert[lyr],
                           jnp.zeros((d, LANES - N_GROUPS - N_EXPERTS), F32)], axis=1)
    w['w_r'] = jnp.concatenate(_split_bf16(w_r, 2), axis=1)
    w['b_r'] = jnp.concatenate([moe_b_group[lyr], moe_b_expert[lyr],
                                jnp.zeros((LANES - N_GROUPS - N_EXPERTS,), F32)])[None]

    pos_p = jnp.arange(lp, dtype=jnp.int32)
    pos_s = n_past + jnp.arange(ls, dtype=jnp.int32)
    s0_p = jnp.zeros((bp, GLA_HEADS, GLA_DK, GLA_DV), F32)
    og_p, (q_cat, k_cat, v_p), ckv_p, kr_p, st_p = _pre_attention(x_prompt, mods_p, pos_p, s0_p, w, prompt=True)
    og_s, (q_lat, q_r), ckv_s, kr_s, st_s = _pre_attention(x_sample, mods_s, pos_s, state_gla[lyr], w,
                                                           prompt=False)
    o_lat, om_p = _attention(q_lat, q_r, ckv_s, kr_s, cache_ckv, jnp.swapaxes(cache_krope, 2, 3), page_table,
                             ls, PAGES_PER_STEP, q_cat, k_cat, v_p, w['g_mla'], bp, lp)
    om_s = _latent_out(o_lat, w['w_uv_h'], w['g_mla'], MLA_ROW_TILE)
    x1_p, h2_p, ids_p, wts_p = _post_attention(x_prompt, mods_p, og_p, om_p, w, prompt=True)
    x1_s, h2_s, ids_s, wts_s = _post_attention(x_sample, mods_s, og_s, om_s, w, prompt=False)
    ckv_p, kr_p = ckv_p.reshape(bp, lp, MLA_KV_RANK), kr_p.reshape(bp, lp, MLA_ROPE)
    ckv_s, kr_s = ckv_s.reshape(bs, ls, MLA_KV_RANK), kr_s.reshape(bs, ls, MLA_ROPE)

    e_id = jnp.concatenate([ids_p[:, :TOP_K], ids_s[:, :TOP_K]], axis=0).reshape(-1)
    y_rows = _moe(_moe_plan(e_id, bp * lp), h2_p, h2_s, moe_w_gu[lyr], moe_w_down[lyr])

    y_p = _final(x1_p, mods_p[5], y_rows, wts_p, g_final[None], 1, 256, 0)
    y_s = _final(x1_s, mods_s[5], y_rows, wts_s, g_final[None], 256 // ls, ls, bp * lp // 256)
    return (y_p, y_s, ckv_p[None], kr_p[None], st_p[None], ckv_s[None], kr_s[None], st_s[None])
```

```python
import functools
import math

import jax
import jax.numpy as jnp
from jax import lax
from jax.experimental import pallas as pl
from jax.experimental.pallas import tpu as pltpu

F32 = jnp.float32
BF16 = jnp.bfloat16

D_MODEL = 2048
PAGE_SIZE = 128
GLA_HEADS = 4
GLA_DK = 128
GLA_DV = 256
GLA_GATE_RANK = 16
GLA_TAU = 16.0
GLA_CHUNK = 64
GLA_QK = GLA_HEADS * GLA_DK
GLA_WIDTH = GLA_HEADS * GLA_DV
MLA_HEADS = 8
MLA_Q_RANK = 512
MLA_KV_RANK = 512
MLA_NOPE = 128
MLA_ROPE = 64
MLA_DV = 128
MLA_WIDTH = MLA_HEADS * MLA_DV
ROPE_BASE = 10000.0
ATTN_SCALE = (MLA_NOPE + MLA_ROPE) ** -0.5
N_GROUPS = 8
EXPERTS_PER_GROUP = 8
N_EXPERTS = N_GROUPS * EXPERTS_PER_GROUP
TOP_K = 2
EXPERT_FF = 1408
EPS = 1e-6

LANES = 128
QK_PAD = 2 * LANES
COL_GQ = 0
COL_GK = GLA_QK
COL_GV = 2 * GLA_QK
COL_GR = COL_GV + GLA_WIDTH
COL_CQ = COL_GR + GLA_WIDTH
COL_CKV = COL_CQ + MLA_Q_RANK
COL_KR = COL_CKV + MLA_KV_RANK
IN_COLS_PAD = COL_KR + LANES
GA_LANE = MLA_ROPE
IN_COL_TILE = IN_COLS_PAD // 3

MXU_COLS = 256
DMA_THREADS = 2
MOE_ROWS = 768
MOE_SUB = 256
MOE_HALF = MXU_COLS
MOE_COLS = 2 * MXU_COLS
MOE_GU_CHUNKS = -(-EXPERT_FF // MOE_HALF)
MOE_DN_CHUNKS = D_MODEL // MOE_COLS
MOE_CHUNKS = MOE_GU_CHUNKS + MOE_DN_CHUNKS
MOE_RING = 5
VMEM_LIMIT = 56 * 1024 * 1024


def _params(sem, vmem=VMEM_LIMIT):
    return pltpu.CompilerParams(dimension_semantics=sem, vmem_limit_bytes=vmem)


def _silu(x):
    return x * jax.nn.sigmoid(x)


def _rms(x, g):
    return x * lax.rsqrt(jnp.mean(x * x, axis=-1, keepdims=True) + EPS) * g


def _ada_kernel(c_ref, w_ref, b_ref, o_ref):
    a = _silu(c_ref[...]).astype(BF16)
    o_ref[...] = jnp.dot(a, w_ref[...].astype(BF16), preferred_element_type=F32) + b_ref[...]


def _ada(c, w, b):
    rows, d = c.shape
    n = w.shape[1]
    tn = 1024
    return pl.pallas_call(
        _ada_kernel,
        grid=(n // tn,),
        in_specs=[pl.BlockSpec((rows, d), lambda j: (0, 0)),
                  pl.BlockSpec((d, tn), lambda j: (0, j)),
                  pl.BlockSpec((1, tn), lambda j: (0, j))],
        out_specs=pl.BlockSpec((rows, tn), lambda j: (0, j)),
        out_shape=jax.ShapeDtypeStruct((rows, n), F32),
        compiler_params=_params(("arbitrary",)),
        name="ada_mod",
    )(c, w, b)


def _inproj_kernel(x_ref, g_ref, sc_ref, sh_ref, w_hbm, o_ref, w_scr, h_scr, sem):
    @pl.when(pl.program_id(0) == 0)
    def _():
        cp = pltpu.make_async_copy(w_hbm, w_scr, sem)
        cp.start()
        cp.wait()

    h = _rms(x_ref[...], g_ref[...][None]) * (1.0 + sc_ref[...]) + sh_ref[...]
    h_scr[...] = h.reshape(h_scr.shape).astype(BF16)
    for j in range(w_scr.shape[1] // IN_COL_TILE):
        cols = slice(j * IN_COL_TILE, (j + 1) * IN_COL_TILE)
        o_ref[:, cols] = jnp.dot(h_scr[...], w_scr[:, cols], preferred_element_type=F32)


def _inproj(x, g, sc, sh, w, bb, lt):
    bsz, seq, d = x.shape
    rows = bb * lt
    nl = seq // lt
    n = w.shape[1]
    return pl.pallas_call(
        _inproj_kernel,
        grid=(bsz * seq // rows,),
        in_specs=[pl.BlockSpec((bb, lt, d), lambda i: (i // nl, i % nl, 0)),
                  pl.BlockSpec((1, d), lambda i: (0, 0)),
                  pl.BlockSpec((bb, 1, d), lambda i: (i // nl, 0, 0)),
                  pl.BlockSpec((bb, 1, d), lambda i: (i // nl, 0, 0)),
                  pl.BlockSpec(memory_space=pl.ANY)],
        out_specs=pl.BlockSpec((rows, n), lambda i: (i, 0)),
        out_shape=jax.ShapeDtypeStruct((bsz * seq, n), F32),
        scratch_shapes=[pltpu.VMEM(w.shape, BF16), pltpu.VMEM((rows, d), BF16), pltpu.SemaphoreType.DMA(())],
        compiler_params=_params(("arbitrary",)),
        name="in_proj",
    )(x, g, sc, sh, w)


def _split_bf16(x, pieces):
    out = []
    for _ in range(pieces):
        part = x.astype(BF16)
        out.append(part)
        x = x - part.astype(F32)
    return out


def _gla_kernel(q_ref, k_ref, v_ref, gr_ref, ga_ref, wg_ref, bg_ref, gon_ref, tri_b_ref, tri_f_ref, s0_ref,
                o_ref, sout_ref, s_scr, *, n_chunks, n_valid):
    c = GLA_CHUNK
    l_step = pl.program_id(2)

    @pl.when(l_step == 0)
    def _():
        s_scr[...] = s0_ref[...]

    chunks_per_batch = n_chunks // s_scr.shape[0]

    def load(ref):
        x = ref[...]
        if n_valid == c:
            return x
        zeros = jnp.zeros((c - n_valid, x.shape[1]), x.dtype)
        return jnp.concatenate([part for i in range(n_chunks)
                                for part in (x[i * n_valid:(i + 1) * n_valid], zeros)], axis=0)

    q = load(q_ref) * (GLA_DK ** -0.5)
    k = load(k_ref)
    v = load(v_ref).astype(BF16)
    a_hi, a_lo = _split_bf16(load(ga_ref), 2)
    w_hi, w_lo = _split_bf16(wg_ref[...], 2)
    z = jnp.dot(jnp.concatenate([a_hi, a_lo, a_hi], axis=1), jnp.concatenate([w_hi, w_hi, w_lo], axis=0),
                preferred_element_type=F32) + bg_ref[...]
    log_a = (jnp.minimum(z, 0.0) - jnp.log1p(jnp.exp(-jnp.abs(z)))) / GLA_TAU
    if n_valid < c:
        log_a = jnp.where(lax.broadcasted_iota(jnp.int32, log_a.shape, 0) % c < n_valid, log_a, 0.0)
    cum = jnp.dot(tri_b_ref[...], jnp.concatenate(_split_bf16(log_a, 3), axis=1), preferred_element_type=F32)
    b = cum[:, :GLA_DK] + cum[:, GLA_DK:2 * GLA_DK] + cum[:, 2 * GLA_DK:]
    q_dec = (q * jnp.exp(b)).astype(BF16)
    k_inv = (k * jnp.exp(-b)).astype(BF16)
    att = lax.dot_general(q_dec, k_inv, (((1,), (1,)), ((), ())), preferred_element_type=F32)
    att = jnp.where(tri_f_ref[...] > 0.0, att, 0.0).astype(BF16)
    o_intra = jnp.dot(att, v, preferred_element_type=F32)

    states = {}
    outs = []
    for idx in range(n_chunks):
        bi = idx // chunks_per_batch
        rows = slice(idx * c, (idx + 1) * c)
        s_old = states[bi] if bi in states else s_scr[bi]
        b_last = b[(idx + 1) * c - 1:(idx + 1) * c, :]
        o = o_intra[rows] + jnp.dot(q_dec[rows], s_old.astype(BF16), preferred_element_type=F32)
        k_dec = (k[rows] * jnp.exp(b_last - b[rows])).astype(BF16)
        kv = lax.dot_general(k_dec, v[rows], (((0,), (0,)), ((), ())), preferred_element_type=F32)
        decay = jnp.exp(jnp.broadcast_to(b_last, (GLA_DK, GLA_DK)).T)
        states[bi] = jnp.concatenate([decay] * (GLA_DV // GLA_DK), axis=1) * s_old + kv
        outs.append(o[:n_valid])
    for bi, s_new in states.items():
        s_scr[bi] = s_new
    o_all = _rms(jnp.concatenate(outs, axis=0), gon_ref[...])
    o_ref[...] = o_all * _silu(gr_ref[...])

    @pl.when(l_step == pl.num_programs(2) - 1)
    def _():
        sout_ref[...] = s_scr[...]


def _gla(proj, s0, wg_pad, bg, gon, bsz, seq, bb, lt):
    n_valid = math.gcd(seq, GLA_CHUNK)
    rows = bb * lt
    nl = seq // lt
    rmap = lambda b, h, l: b * nl + l
    n_chunks = rows // n_valid
    kern = functools.partial(_gla_kernel, n_chunks=n_chunks, n_valid=n_valid)
    st_spec = pl.BlockSpec((bb, None, GLA_DK, GLA_DV), lambda b, h, l: (b, h, 0, 0))
    pos = jnp.arange(n_chunks * GLA_CHUNK)
    tri = (pos[:, None] // GLA_CHUNK == pos[None, :] // GLA_CHUNK) & (pos[None, :] <= pos[:, None])
    tri_spec = pl.BlockSpec(tri.shape, lambda b, h, l: (0, 0))
    return pl.pallas_call(
        kern,
        grid=(bsz // bb, GLA_HEADS, nl),
        in_specs=[pl.BlockSpec((rows, GLA_DK), lambda b, h, l: (rmap(b, h, l), COL_GQ // GLA_DK + h)),
                  pl.BlockSpec((rows, GLA_DK), lambda b, h, l: (rmap(b, h, l), COL_GK // GLA_DK + h)),
                  pl.BlockSpec((rows, GLA_DV), lambda b, h, l: (rmap(b, h, l), COL_GV // GLA_DV + h)),
                  pl.BlockSpec((rows, GLA_DV), lambda b, h, l: (rmap(b, h, l), COL_GR // GLA_DV + h)),
                  pl.BlockSpec((rows, LANES), lambda b, h, l: (rmap(b, h, l), COL_KR // LANES)),
                  pl.BlockSpec((LANES, GLA_DK), lambda b, h, l: (0, h)),
                  pl.BlockSpec((1, GLA_DK), lambda b, h, l: (0, h)),
                  pl.BlockSpec((1, GLA_DV), lambda b, h, l: (0, 0)),
                  tri_spec, tri_spec, st_spec],
        out_specs=[pl.BlockSpec((rows, GLA_DV), lambda b, h, l: (rmap(b, h, l), h)), st_spec],
        out_shape=[jax.ShapeDtypeStruct((bsz * seq, GLA_WIDTH), F32),
                   jax.ShapeDtypeStruct((bsz, GLA_HEADS, GLA_DK, GLA_DV), F32)],
        scratch_shapes=[pltpu.VMEM((bb, GLA_DK, GLA_DV), F32)],
        compiler_params=_params(("arbitrary", "arbitrary", "arbitrary")),
        name="gla",
    )(proj, proj, proj, proj, proj, wg_pad, bg, gon, tri.astype(BF16), tri.astype(F32), s0)


def _rope(r, cos, sin_lo, sin_hi):
    return (r * cos + pltpu.roll(r, LANES - MLA_ROPE // 2, axis=1) * sin_lo
            + pltpu.roll(r, MLA_ROPE // 2, axis=1) * sin_hi)


def _mla_proj_kernel(cq_ref, ckv_ref, kr_ref, cos_ref, slo_ref, shi_ref, gq_ref, gkv_ref, wq_ref,
                     wk_ref, wv_ref, *out_refs, absorbed):
    cos, slo, shi = cos_ref[...], slo_ref[...], shi_ref[...]
    cqn = _rms(cq_ref[...], gq_ref[...]).astype(BF16)
    qf = jnp.dot(cqn, wq_ref[...], preferred_element_type=F32)
    ckv_n = _rms(ckv_ref[...], gkv_ref[...])
    k_rot = _rope(kr_ref[...], cos, slo, shi)
    if absorbed:
        qlat_ref, qr_ref, ckvn_ref, krope_ref = out_refs
    else:
        qcat_ref, kcat_ref, v_ref, ckvn_ref, krope_ref = out_refs
    ckvn_ref[...] = ckv_n
    krope_ref[...] = k_rot[:, :MLA_ROPE]
    for h in range(MLA_HEADS):
        q_nope = qf[:, h * QK_PAD:h * QK_PAD + LANES] * ATTN_SCALE
        q_rot = _rope(qf[:, h * QK_PAD + LANES:(h + 1) * QK_PAD], cos, slo, shi) * ATTN_SCALE
        if absorbed:
            qlat_ref[h] = jnp.dot(q_nope.astype(BF16), wk_ref[h], preferred_element_type=F32)
            qr_ref[h] = q_rot
        else:
            qcat_ref[:, h * QK_PAD:h * QK_PAD + LANES] = q_nope.astype(BF16)
            qcat_ref[:, h * QK_PAD + LANES:(h + 1) * QK_PAD] = q_rot.astype(BF16)
    if not absorbed:
        ckv_b = ckv_n.astype(BF16)
        k_nope = jnp.dot(ckv_b, wk_ref[...], preferred_element_type=F32)
        k_rot_b = k_rot.astype(BF16)
        for h in range(MLA_HEADS):
            kcat_ref[:, h * QK_PAD:h * QK_PAD + LANES] = k_nope[:, h * LANES:(h + 1) * LANES].astype(BF16)
            kcat_ref[:, h * QK_PAD + LANES:(h + 1) * QK_PAD] = k_rot_b
        v_ref[...] = jnp.dot(ckv_b, wv_ref[...], preferred_element_type=F32).astype(BF16)


def _mla_proj(proj, tabs, gq, gkv, wq, wk, wv, tm, absorbed):
    t = proj.shape[0]
    ntab = tabs[0].shape[0] // tm
    row = lambda c: pl.BlockSpec((tm, c), lambda i: (i, 0))
    full = lambda a: pl.BlockSpec(a.shape, lambda i: (0,) * a.ndim)
    tab = pl.BlockSpec((tm, LANES), lambda i: (i % ntab, 0))
    in_specs = [pl.BlockSpec((tm, MLA_Q_RANK), lambda i: (i, COL_CQ // MLA_Q_RANK)),
                pl.BlockSpec((tm, MLA_KV_RANK), lambda i: (i, COL_CKV // MLA_KV_RANK)),
                pl.BlockSpec((tm, LANES), lambda i: (i, COL_KR // LANES)),
                tab, tab, tab, full(gq), full(gkv), full(wq), full(wk), full(wv)]
    if absorbed:
        out_specs = [pl.BlockSpec((MLA_HEADS, tm, MLA_KV_RANK), lambda i: (0, i, 0)),
                     pl.BlockSpec((MLA_HEADS, tm, LANES), lambda i: (0, i, 0))]
        out_shape = [jax.ShapeDtypeStruct((MLA_HEADS, t, MLA_KV_RANK), F32),
                     jax.ShapeDtypeStruct((MLA_HEADS, t, LANES), F32)]
    else:
        out_specs = [row(MLA_HEADS * QK_PAD), row(MLA_HEADS * QK_PAD), row(MLA_WIDTH)]
        out_shape = [jax.ShapeDtypeStruct((t, MLA_HEADS * QK_PAD), BF16),
                     jax.ShapeDtypeStruct((t, MLA_HEADS * QK_PAD), BF16),
                     jax.ShapeDtypeStruct((t, MLA_WIDTH), BF16)]
    out_specs += [row(MLA_KV_RANK), row(MLA_ROPE)]
    out_shape += [jax.ShapeDtypeStruct((t, MLA_KV_RANK), F32), jax.ShapeDtypeStruct((t, MLA_ROPE), F32)]
    return pl.pallas_call(
        functools.partial(_mla_proj_kernel, absorbed=absorbed),
        grid=(t // tm,),
        in_specs=in_specs, out_specs=out_specs, out_shape=out_shape,
        compiler_params=_params(("arbitrary",)),
        name="mla_proj_absorbed" if absorbed else "mla_proj",
    )(proj, proj, proj, *tabs, gq, gkv, wq, wk, wv)


def _attn_kernel(q_ref, k_ref, v_ref, g_ref, o_ref, *, tq, tk):
    assert tq == tk
    qi = pl.program_id(2)
    q = q_ref[...]

    def block(j, carry, on_diagonal):
        m, l, acc = carry
        k0 = pl.multiple_of(j * tk, tk)
        s = lax.dot_general(q, k_ref[pl.ds(k0, tk), :], (((1,), (1,)), ((), ())), preferred_element_type=F32)
        if on_diagonal:
            row = lax.broadcasted_iota(jnp.int32, (tq, tk), 0)
            col = lax.broadcasted_iota(jnp.int32, (tq, tk), 1)
            s = jnp.where(col <= row, s, -jnp.inf)
        m_new = jnp.maximum(m, jnp.max(s, axis=-1, keepdims=True))
        alpha = jnp.exp(m - m_new)
        p = jnp.exp(s - m_new)
        l = alpha * l + jnp.sum(p, axis=-1, keepdims=True)
        acc = alpha * acc + jnp.dot(p.astype(BF16), v_ref[pl.ds(k0, tk), :], preferred_element_type=F32)
        return m_new, l, acc

    init = (jnp.full((tq, 1), -jnp.inf, F32), jnp.zeros((tq, 1), F32), jnp.zeros((tq, MLA_DV), F32))
    carry = lax.fori_loop(0, qi, lambda j, c: block(j, c, False), init)
    _, l, acc = block(qi, carry, True)
    o_ref[...] = _rms(acc / l, g_ref[...])


def _prompt_attention(q_cat, k_cat, v, g, bsz, seq, tq, tk):
    nq = seq // tq
    return pl.pallas_call(
        functools.partial(_attn_kernel, tq=tq, tk=tk),
        grid=(bsz, MLA_HEADS, nq),
        in_specs=[pl.BlockSpec((tq, QK_PAD), lambda b, h, i: (b * nq + i, h)),
                  pl.BlockSpec((seq, QK_PAD), lambda b, h, i: (b, h)),
                  pl.BlockSpec((seq, MLA_DV), lambda b, h, i: (b, h)),
                  pl.BlockSpec((1, MLA_DV), lambda b, h, i: (0, 0))],
        out_specs=pl.BlockSpec((tq, MLA_DV), lambda b, h, i: (b * nq + i, h)),
        out_shape=jax.ShapeDtypeStruct((bsz * seq, MLA_WIDTH), F32),
        compiler_params=_params(("arbitrary", "arbitrary", "arbitrary")),
        name="prompt_attention",
    )(q_cat, k_cat, v, g)


def _paged_attn_kernel(pt_ref, qlat_ref, qr_ref, cnew_ref, knew_ref, ckv_hbm, krt_hbm, o_ref,
                       ckv_buf, krt_buf, sems, m_scr, l_scr, acc_scr, *, pp, n_q):
    grp = pl.program_id(1)
    n_grp = pl.num_programs(1)
    step = pl.program_id(0) * n_grp + grp
    n_steps = pl.num_programs(0) * n_grp
    slot = step % 2
    rows = MLA_HEADS * n_q

    def page_copies(at_step, at_slot):
        out = []
        for j in range(pp):
            page = pt_ref[at_step * pp + j]
            out.append(pltpu.make_async_copy(ckv_hbm.at[0, page], ckv_buf.at[at_slot, j], sems.at[0, at_slot]))
            out.append(pltpu.make_async_copy(krt_hbm.at[0, page],
                                             krt_buf.at[at_slot, :, pl.ds(j * PAGE_SIZE, PAGE_SIZE)],
                                             sems.at[1, at_slot]))
        return out

    def start_pages(at_step, at_slot):
        for i, cp in enumerate(page_copies(at_step, at_slot)):
            cp.start(priority=(i // 2 + i) % DMA_THREADS)

    @pl.when(step == 0)
    def _():
        start_pages(step, slot)

    @pl.when(step + 1 < n_steps)
    def _():
        start_pages(step + 1, 1 - slot)

    @pl.when(grp == 0)
    def _():
        m_scr[...] = jnp.full(m_scr.shape, -jnp.inf, F32)
        l_scr[...] = jnp.zeros(l_scr.shape, F32)
        acc_scr[...] = jnp.zeros(acc_scr.shape, F32)

    q = qlat_ref[...].reshape(rows, MLA_KV_RANK).astype(BF16)
    qr = qr_ref[...].reshape(rows, LANES)[:, :MLA_ROPE].astype(BF16)

    def update(s, v):
        m = m_scr[...]
        m_new = jnp.maximum(m, jnp.max(s, axis=-1, keepdims=True))
        alpha = jnp.exp(m - m_new)
        p = jnp.exp(s - m_new)
        l_scr[...] = alpha * l_scr[...] + jnp.sum(p, axis=-1, keepdims=True)
        acc_scr[...] = acc_scr[...] * alpha + jnp.dot(p.astype(BF16), v, preferred_element_type=F32)
        m_scr[...] = m_new

    for cp in page_copies(step, slot):
        cp.wait()
    ckv = ckv_buf[slot].reshape(pp * PAGE_SIZE, MLA_KV_RANK).astype(BF16)
    s_past = (lax.dot_general(q, ckv, (((1,), (1,)), ((), ())), preferred_element_type=F32)
              + jnp.dot(qr, krt_buf[slot].astype(BF16), preferred_element_type=F32))
    update(s_past, ckv)

    @pl.when(grp == n_grp - 1)
    def _():
        pad = PAGE_SIZE - n_q
        c_new = jnp.concatenate([cnew_ref[...], jnp.zeros((pad, MLA_KV_RANK), F32)], axis=0).astype(BF16)
        k_new = jnp.concatenate([knew_ref[...], jnp.zeros((pad, MLA_ROPE), F32)], axis=0).astype(BF16)
        s = (lax.dot_general(q, c_new, (((1,), (1,)), ((), ())), preferred_element_type=F32)
             + lax.dot_general(qr, k_new, (((1,), (1,)), ((), ())), preferred_element_type=F32))
        q_idx = lax.broadcasted_iota(jnp.int32, s.shape, 0) % n_q
        k_idx = lax.broadcasted_iota(jnp.int32, s.shape, 1)
        update(jnp.where(k_idx <= q_idx, s, -jnp.inf), c_new)
        o_ref[...] = (acc_scr[...] / l_scr[...]).reshape(o_ref.shape)


def _paged_attention(q_lat, q_r, ckv_n, k_rope, cache_ckv, cache_krope_t, page_table, n_q, pp):
    bsz, n_pages = page_table.shape
    rows = MLA_HEADS * n_q
    grid_spec = pltpu.PrefetchScalarGridSpec(
        num_scalar_prefetch=1,
        grid=(bsz, n_pages // pp),
        in_specs=[pl.BlockSpec((MLA_HEADS, n_q, MLA_KV_RANK), lambda b, s, pt: (0, b, 0)),
                  pl.BlockSpec((MLA_HEADS, n_q, LANES), lambda b, s, pt: (0, b, 0)),
                  pl.BlockSpec((n_q, MLA_KV_RANK), lambda b, s, pt: (b, 0)),
                  pl.BlockSpec((n_q, MLA_ROPE), lambda b, s, pt: (b, 0)),
                  pl.BlockSpec(memory_space=pl.ANY),
                  pl.BlockSpec(memory_space=pl.ANY)],
        out_specs=pl.BlockSpec((MLA_HEADS, n_q, MLA_KV_RANK), lambda b, s, pt: (0, b, 0)),
        scratch_shapes=[pltpu.VMEM((2, pp, PAGE_SIZE, MLA_KV_RANK), F32),
                        pltpu.VMEM((2, MLA_ROPE, pp * PAGE_SIZE), F32),
                        pltpu.SemaphoreType.DMA((2, 2)),
                        pltpu.VMEM((rows, 1), F32), pltpu.VMEM((rows, 1), F32),
                        pltpu.VMEM((rows, MLA_KV_RANK), F32)],
    )
    return pl.pallas_call(
        functools.partial(_paged_attn_kernel, pp=pp, n_q=n_q),
        grid_spec=grid_spec,
        out_shape=jax.ShapeDtypeStruct((MLA_HEADS, bsz * n_q, MLA_KV_RANK), F32),
        compiler_params=_params(("arbitrary", "arbitrary")),
        name="paged_attention",
    )(page_table.reshape(-1), q_lat, q_r, ckv_n, k_rope, cache_ckv, cache_krope_t)


def _latent_out_kernel(o_ref, w_ref, g_ref, out_ref):
    o = jnp.dot(o_ref[...].astype(BF16), w_ref[...], preferred_element_type=F32)
    out_ref[...] = _rms(o, g_ref[...])


def _latent_out(o_lat, w_uv_h, g, tm):
    t = o_lat.shape[1]
    return pl.pallas_call(
        _latent_out_kernel,
        grid=(MLA_HEADS, t // tm),
        in_specs=[pl.BlockSpec((None, tm, MLA_KV_RANK), lambda h, i: (h, i, 0)),
                  pl.BlockSpec((None, MLA_KV_RANK, MLA_DV), lambda h, i: (h, 0, 0)),
                  pl.BlockSpec((1, MLA_DV), lambda h, i: (0, 0))],
        out_specs=pl.BlockSpec((tm, MLA_DV), lambda h, i: (i, h)),
        out_shape=jax.ShapeDtypeStruct((t, MLA_WIDTH), F32),
        compiler_params=_params(("arbitrary", "arbitrary")),
        name="latent_out",
    )(o_lat, w_uv_h, g)


def _route(logit):
    lane = lax.broadcasted_iota(jnp.int32, logit.shape, 1)
    neg = -jnp.inf
    big = jnp.int32(LANES)

    def first_max(vals):
        top = jnp.max(vals, axis=-1, keepdims=True)
        return top, jnp.min(jnp.where(vals == top, lane, big), axis=-1, keepdims=True)

    g_vals = jnp.where(lane < N_GROUPS, logit, neg)
    g_max, g_idx = first_max(g_vals)
    g_w = 1.0 / jnp.sum(jnp.exp(g_vals - g_max), axis=-1, keepdims=True)
    e_lane = lane - N_GROUPS
    in_group = (e_lane >= g_idx * EXPERTS_PER_GROUP) & (e_lane < (g_idx + 1) * EXPERTS_PER_GROUP)
    e_vals = jnp.where(in_group, logit, neg)
    v1, i1 = first_max(e_vals)
    v2, i2 = first_max(jnp.where(lane == i1, neg, e_vals))
    e2 = jnp.exp(v2 - v1)
    w1 = g_w / (1.0 + e2)
    w2 = g_w * e2 / (1.0 + e2)
    ids = jnp.where(lane == 0, i1 - N_GROUPS, jnp.where(lane == 1, i2 - N_GROUPS, 0))
    wts = jnp.where(lane == 0, w1, jnp.where(lane == 1, w2, 0.0))
    return ids, wts


def _outproj_kernel(og_ref, om_ref, wg_ref, wm_ref, x_ref, gt_ref, sc_ref, sh_ref, g_ref, wr_ref, br_ref,
                    x1_ref, h2_ref, id_ref, wt_ref):
    y = (jnp.dot(og_ref[...].astype(BF16), wg_ref[...], preferred_element_type=F32)
         + jnp.dot(om_ref[...].astype(BF16), wm_ref[...], preferred_element_type=F32))
    x1 = x_ref[...] + gt_ref[...] * y.reshape(x_ref.shape)
    x1_ref[...] = x1
    h2 = (_rms(x1, g_ref[...][None]) * (1.0 + sc_ref[...]) + sh_ref[...]).reshape(h2_ref.shape)
    h2_ref[...] = h2
    rows = h2.shape[0]
    lg = jnp.dot(jnp.concatenate(_split_bf16(h2, 2), axis=0), wr_ref[...], preferred_element_type=F32)
    logit = (lg[:rows, :LANES] + lg[:rows, LANES:] + lg[rows:, :LANES] + lg[rows:, LANES:]) + br_ref[...]
    id_ref[...], wt_ref[...] = _route(logit)


def _outproj(o_gla, o_mla, w_out, x, gt, sc, sh, g, w_r, b_r, bb, lt):
    bsz, seq, d = x.shape
    rows = bb * lt
    nl = seq // lt
    xs = pl.BlockSpec((bb, lt, d), lambda i: (i // nl, i % nl, 0))
    ms = pl.BlockSpec((bb, 1, d), lambda i: (i // nl, 0, 0))
    full = lambda a: pl.BlockSpec(a.shape, lambda i: (0,) * a.ndim)
    wide = pl.BlockSpec((rows, LANES), lambda i: (i, 0))
    return pl.pallas_call(
        _outproj_kernel,
        grid=(bsz * seq // rows,),
        in_specs=[pl.BlockSpec((rows, GLA_WIDTH), lambda i: (i, 0)),
                  pl.BlockSpec((rows, MLA_WIDTH), lambda i: (i, 0)),
                  pl.BlockSpec((GLA_WIDTH, d), lambda i: (0, 0)),
                  pl.BlockSpec((MLA_WIDTH, d), lambda i: (GLA_WIDTH // MLA_WIDTH, 0)),
                  xs, ms, ms, ms, full(g), full(w_r), full(b_r)],
        out_specs=[xs, pl.BlockSpec((rows, d), lambda i: (i, 0)), wide, wide],
        out_shape=[jax.ShapeDtypeStruct(x.shape, F32), jax.ShapeDtypeStruct((bsz * seq, d), F32),
                   jax.ShapeDtypeStruct((bsz * seq, LANES), jnp.int32),
                   jax.ShapeDtypeStruct((bsz * seq, LANES), F32)],
        compiler_params=_params(("arbitrary",)),
        name="out_proj_route",
    )(o_gla, o_mla, w_out, w_out, x, gt, sc, sh, g, w_r, b_r)


def _moe_kernel(ie_ref, ir_ref, in_ref, im_ref, nl_ref, src_ref, dst_ref,
                hp_hbm, hs_hbm, wgu_hbm, wd_hbm, y_hbm,
                x_buf, xb_scr, wb_scr, act_scr, y_buf, ring, gsem, ssem, wsem):
    n_live = nl_ref[0]
    last_item = ie_ref.shape[0] - 1

    def chunk_width(j):
        return min(MOE_HALF, EXPERT_FF - j * MOE_HALF)

    def weight_copies(e, j, slot):
        sem = wsem.at[slot]
        if j < MOE_GU_CHUNKS:
            w = chunk_width(j)
            return [pltpu.make_async_copy(wgu_hbm.at[e, :, pl.ds(off + j * MOE_HALF, w)],
                                          ring.at[slot, :, pl.ds(half * w, w)], sem)
                    for half, off in enumerate((0, EXPERT_FF))]
        c = j - MOE_GU_CHUNKS
        return [pltpu.make_async_copy(wd_hbm.at[e, :, pl.ds(c * MOE_COLS, MOE_COLS)],
                                      ring.at[slot, pl.ds(0, EXPERT_FF)], sem)]

    def start_weights(e, j, slot):
        for idx, cp in enumerate(weight_copies(e, j, slot)):
            cp.start(priority=(j + idx) % DMA_THREADS)

    def gather_copy(h_hbm, src_row, p, r, k=1):
        return pltpu.make_async_copy(h_hbm.at[pl.ds(src_row, k)], x_buf.at[p, pl.ds(r, k)], gsem.at[p])

    def scatter_copy(r, dst_row, k=1):
        return pltpu.make_async_copy(y_buf.at[pl.ds(r, k)], y_hbm.at[pl.ds(dst_row, k)], ssem)

    def for_rows(lo, hi, fn, unroll=4):
        groups = (hi - lo) // unroll

        def group(g, c):
            for u in range(unroll):
                fn(lo + g * unroll + u)
            return c

        def single(r, c):
            fn(r)
            return c

        lax.fori_loop(0, groups, group, 0)
        lax.fori_loop(lo + groups * unroll, hi, single, 0)

    def wait_rows(n, copy_of_rows):
        k = 1
        while k <= MOE_ROWS:
            @pl.when((n & k) != 0)
            def _(k=k):
                copy_of_rows(k).wait()
            k *= 2

    def start_gather(i, p):
        row0, n, m = ir_ref[i], in_ref[i], im_ref[i]
        for_rows(0, m, lambda r: gather_copy(hp_hbm, src_ref[row0 + r], p, r).start())
        for_rows(m, n, lambda r: gather_copy(hs_hbm, src_ref[row0 + r], p, r).start())

    x_buf[...] = jnp.zeros(x_buf.shape, F32)
    start_gather(0, 0)
    for j in range(MOE_RING):
        start_weights(ie_ref[0], j, j)

    def item_body(i, carry):
        p = i % 2
        e, row0, n = ie_ref[i], ir_ref[i], in_ref[i]
        n_sub = (n + MOE_SUB - 1) // MOE_SUB
        has_next = i + 1 < n_live
        e_next = ie_ref[jnp.minimum(i + 1, last_item)]
        base = (i * MOE_CHUNKS) % MOE_RING

        wait_rows(n, lambda k: gather_copy(hp_hbm, 0, p, 0, k))

        @pl.when(has_next)
        def _():
            start_gather(i + 1, 1 - p)

        def for_row_blocks(fn):
            pairs = n_sub // 2

            def pair(b, c):
                fn(pl.multiple_of(b * 2 * MOE_SUB, 2 * MOE_SUB), 2 * MOE_SUB)
                return c
            lax.fori_loop(0, pairs, pair, 0)

            @pl.when(n_sub % 2 == 1)
            def _():
                fn(pl.multiple_of(pairs * 2 * MOE_SUB, 2 * MOE_SUB), MOE_SUB)

        def cast_rows(sb, c):
            r0 = pl.multiple_of(sb * MOE_SUB, MOE_SUB)
            xb_scr[pl.ds(r0, MOE_SUB), :] = x_buf[p, pl.ds(r0, MOE_SUB), :].astype(BF16)
            return c
        lax.fori_loop(0, n_sub, cast_rows, 0)

        for j in range(MOE_CHUNKS):
            slot = (base + j) % MOE_RING
            if j == MOE_GU_CHUNKS:
                @pl.when(i > 0)
                def _():
                    n_prev = in_ref[jnp.maximum(i - 1, 0)]
                    wait_rows(n_prev, lambda k: scatter_copy(0, 0, k))
            for cp in weight_copies(e, j, slot):
                cp.wait()
            if j < MOE_GU_CHUNKS:
                cols = 2 * chunk_width(j)
                wb_scr[:, :cols] = ring[slot, :, pl.ds(0, cols)].astype(BF16)
            else:
                wb_scr[:EXPERT_FF, :] = ring[slot, pl.ds(0, EXPERT_FF), :].astype(BF16)
            j_ahead = j + MOE_RING
            if j_ahead < MOE_CHUNKS:
                start_weights(e, j_ahead, slot)
            else:
                @pl.when(has_next)
                def _(j_ahead=j_ahead, slot=slot):
                    start_weights(e_next, j_ahead - MOE_CHUNKS, slot)

            if j < MOE_GU_CHUNKS:
                w = chunk_width(j)

                def gate_up(r0, rows, j=j, w=w):
                    res = jnp.dot(xb_scr[pl.ds(r0, rows), :], wb_scr[:, :2 * w], preferred_element_type=F32)
                    act = _silu(res[:, :w]) * res[:, w:]
                    act_scr[pl.ds(r0, rows), j * MOE_HALF:j * MOE_HALF + w] = act.astype(BF16)
                for_row_blocks(gate_up)
            else:
                col0 = (j - MOE_GU_CHUNKS) * MOE_COLS

                def down(r0, rows, col0=col0):
                    y_buf[pl.ds(r0, rows), col0:col0 + MOE_COLS] = jnp.dot(
                        act_scr[pl.ds(r0, rows), :], wb_scr[:EXPERT_FF, :], preferred_element_type=F32)
                for_row_blocks(down)

        for_rows(0, n, lambda r: scatter_copy(r, dst_ref[row0 + r]).start())
        return carry

    lax.fori_loop(0, n_live, item_body, 0)
    wait_rows(in_ref[jnp.maximum(n_live - 1, 0)], lambda k: scatter_copy(0, 0, k))


def _moe(plan, h2_p, h2_s, w_gu, w_down):
    d = h2_p.shape[1]
    t_all = h2_p.shape[0] + h2_s.shape[0]
    hbm = pl.BlockSpec(memory_space=pl.ANY)
    grid_spec = pltpu.PrefetchScalarGridSpec(
        num_scalar_prefetch=len(plan),
        grid=(1,),
        in_specs=[hbm, hbm, hbm, hbm],
        out_specs=hbm,
        scratch_shapes=[pltpu.VMEM((2, MOE_ROWS, d), F32),
                        pltpu.VMEM((MOE_ROWS, d), BF16),
                        pltpu.VMEM((d, MOE_COLS), BF16),
                        pltpu.VMEM((MOE_ROWS, EXPERT_FF), BF16),
                        pltpu.VMEM((MOE_ROWS, d), F32),
                        pltpu.VMEM((MOE_RING, d, MOE_COLS), F32),
                        pltpu.SemaphoreType.DMA((2,)),
                        pltpu.SemaphoreType.DMA(()),
                        pltpu.SemaphoreType.DMA((MOE_RING,))],
    )
    return pl.pallas_call(
        _moe_kernel,
        grid_spec=grid_spec,
        out_shape=jax.ShapeDtypeStruct((t_all * TOP_K, d), F32),
        compiler_params=_params(("arbitrary",)),
        name="moe_experts",
    )(*plan, h2_p, h2_s, w_gu, w_down)


def _moe_plan(e_id, t_prompt):
    n_as = e_id.shape[0]
    n_items = N_EXPERTS + n_as // MOE_ROWS
    order = jnp.argsort(e_id).astype(jnp.int32)
    cnt = jnp.zeros((N_EXPERTS,), jnp.int32).at[e_id].add(1)
    cnt_p = jnp.zeros((N_EXPERTS,), jnp.int32).at[e_id[:t_prompt * TOP_K]].add(1)
    ustart = jnp.cumsum(cnt) - cnt
    per_e = (cnt + MOE_ROWS - 1) // MOE_ROWS
    item_end = jnp.cumsum(per_e)
    total = item_end[-1]
    idx = jnp.arange(n_items, dtype=jnp.int32)
    live = idx < total
    e_of = jnp.sum(item_end[None, :] <= jnp.minimum(idx, total - 1)[:, None], axis=1).astype(jnp.int32)
    local = jnp.minimum(idx, total - 1) - (item_end[e_of] - per_e[e_of])
    row0 = ustart[e_of] + local * MOE_ROWS
    n_rows = jnp.where(live, jnp.minimum(MOE_ROWS, cnt[e_of] - local * MOE_ROWS), 0)
    m_rows = jnp.clip(cnt_p[e_of] - local * MOE_ROWS, 0, n_rows)
    tok_s = order // TOP_K
    src_s = jnp.where(tok_s < t_prompt, tok_s, tok_s - t_prompt)
    dst_s = (order % TOP_K) * (n_as // TOP_K) + tok_s
    i32 = lambda a: a.astype(jnp.int32)
    return (i32(e_of), i32(row0), i32(n_rows), i32(m_rows), i32(total)[None], i32(src_s), i32(dst_s))


def _final_kernel(x_ref, gt_ref, y0_ref, y1_ref, wt_ref, g_ref, o_ref):
    wt = wt_ref[...]
    y = wt[:, 0:1] * y0_ref[...] + wt[:, 1:2] * y1_ref[...]
    x2 = x_ref[...] + gt_ref[...] * y.reshape(x_ref.shape)
    o_ref[...] = _rms(x2, g_ref[...][None])


def _final(x1, gt, y_rows, wt, g, bb, lt, row_block0):
    bsz, seq, d = x1.shape
    rows = bb * lt
    nl = seq // lt
    choice_blocks = y_rows.shape[0] // TOP_K // rows
    xs = pl.BlockSpec((bb, lt, d), lambda i: (i // nl, i % nl, 0))
    return pl.pallas_call(
        _final_kernel,
        grid=(bsz * seq // rows,),
        in_specs=[xs, pl.BlockSpec((bb, 1, d), lambda i: (i // nl, 0, 0)),
                  pl.BlockSpec((rows, d), lambda i: (row_block0 + i, 0)),
                  pl.BlockSpec((rows, d), lambda i: (choice_blocks + row_block0 + i, 0)),
                  pl.BlockSpec((rows, LANES), lambda i: (i, 0)),
                  pl.BlockSpec((1, d), lambda i: (0, 0))],
        out_specs=xs,
        out_shape=jax.ShapeDtypeStruct(x1.shape, F32),
        compiler_params=_params(("arbitrary",)),
        name="moe_combine_final_norm",
    )(x1, gt, y_rows, y_rows, wt, g)


def _rope_tables(pos, rows):
    half = MLA_ROPE // 2
    inv_freq = ROPE_BASE ** (-jnp.arange(half, dtype=F32) / half)
    ang = pos.astype(F32)[:, None] * inv_freq[None, :]
    cos, sin = jnp.cos(ang), jnp.sin(ang)
    zero = jnp.zeros_like(cos)
    pad = jnp.zeros((pos.shape[0], LANES - MLA_ROPE), F32)
    tabs = (jnp.concatenate([cos, cos, pad], axis=1),
            jnp.concatenate([-sin, zero, pad], axis=1),
            jnp.concatenate([zero, sin, pad], axis=1))
    reps = rows // pos.shape[0]
    return tuple(jnp.tile(t, (reps, 1)) for t in tabs)


def _arrange_w_in(w_in):
    d = w_in.shape[0]
    off_gv = 2 * GLA_QK
    off_ga = off_gv + GLA_WIDTH
    off_gr = off_ga + GLA_GATE_RANK
    off_cq = off_gr + GLA_WIDTH
    off_ckv = off_cq + MLA_Q_RANK
    off_kr = off_ckv + MLA_KV_RANK
    pad = jnp.zeros((d, LANES - MLA_ROPE - GLA_GATE_RANK), w_in.dtype)
    return jnp.concatenate([w_in[:, :off_ga], w_in[:, off_gr:off_kr + MLA_ROPE], w_in[:, off_ga:off_gr], pad],
                           axis=1).astype(BF16)


def _arrange_w_uq(w_uq):
    r = w_uq.shape[0]
    pad = jnp.zeros((r, MLA_HEADS, QK_PAD - MLA_NOPE - MLA_ROPE), w_uq.dtype)
    return jnp.concatenate([w_uq, pad], axis=2).reshape(r, MLA_HEADS * QK_PAD).astype(BF16)


def _mixer(x, mods, pos, s0, cache, w, prompt):
    bsz, seq, d = x.shape
    t = bsz * seq
    sh1, sc1, gt1, sh2, sc2, _ = mods
    bb, lt = (1, 512) if prompt else (512 // seq, seq)
    proj = _inproj(x, w['g_mix'], sc1, sh1, w['w_in'], bb, lt)
    gbb, glt = (1, 512) if prompt else (8, seq)
    o_gla, s_fin = _gla(proj, s0, w['wg_pad'], w['bg'], w['g_gla'], bsz, seq, gbb, glt)
    tm = 512
    if prompt:
        tabs = _rope_tables(pos, seq)
        q_cat, k_cat, v, ckv_n, k_rope = _mla_proj(proj, tabs, w['g_q'], w['g_kv'], w['w_uq'], w['w_uk'],
                                                    w['w_uv'], tm, absorbed=False)
        o_mla = _prompt_attention(q_cat, k_cat, v, w['g_mla'], bsz, seq, 512, 512)
    else:
        cache_ckv, cache_krope, page_table = cache
        tabs = _rope_tables(pos, tm)
        q_lat, q_r, ckv_n, k_rope = _mla_proj(proj, tabs, w['g_q'], w['g_kv'], w['w_uq'], w['w_uk_t'],
                                              w['w_uv'], tm, absorbed=True)
        o_lat = _paged_attention(q_lat, q_r, ckv_n, k_rope, cache_ckv, jnp.swapaxes(cache_krope, 2, 3),
                                 page_table, seq, 32)
        o_mla = _latent_out(o_lat, w['w_uv_h'], w['g_mla'], tm)
    obb, olt = (1, 256) if prompt else (256 // seq, seq)
    routed = _outproj(o_gla, o_mla, w['w_out'], x, gt1, sc2, sh2, w['g_ffn'], w['w_r'], w['b_r'], obb, olt)
    return routed, ckv_n.reshape(bsz, seq, MLA_KV_RANK), k_rope.reshape(bsz, seq, MLA_ROPE), s_fin


def kernel(x_prompt, x_sample, c_prompt, c_sample, cache_ckv, cache_krope, state_gla, page_table, w_ada, b_ada, g_mix, g_ffn, w_in, gla_w_gate, gla_b_gate, gla_g_onorm, mla_g_qnorm, mla_w_uq, mla_g_kvnorm, mla_w_uk, mla_w_uv, mla_g_onorm, w_out, moe_w_group, moe_b_group, moe_w_expert, moe_b_expert, moe_w_gu, moe_w_down, g_final):
    depth = w_ada.shape[0]
    assert depth == 1
    bp, lp, d = x_prompt.shape
    bs, ls, _ = x_sample.shape
    n_past = page_table.shape[1] * PAGE_SIZE
    lyr = 0

    n_c = bp + bs
    c_rows = -(-n_c // 8) * 8
    c_all = jnp.concatenate([c_prompt, c_sample, jnp.zeros((c_rows - n_c, d), F32)], axis=0)
    mod = _ada(c_all, w_ada[lyr], b_ada[lyr][None])
    mods_p = tuple(m[:bp, None, :] for m in jnp.split(mod, 6, axis=-1))
    mods_s = tuple(m[bp:n_c, None, :] for m in jnp.split(mod, 6, axis=-1))

    wg_pad = jnp.zeros((LANES, GLA_QK), F32).at[GA_LANE:GA_LANE + GLA_GATE_RANK].set(gla_w_gate[lyr])
    w_uk = mla_w_uk[lyr]
    w_uv = mla_w_uv[lyr]
    w = {
        'g_mix': g_mix[lyr][None], 'g_ffn': g_ffn[lyr][None],
        'w_in': _arrange_w_in(w_in[lyr]),
        'wg_pad': wg_pad, 'bg': gla_b_gate[lyr][None], 'g_gla': gla_g_onorm[lyr][None],
        'g_q': mla_g_qnorm[lyr][None], 'g_kv': mla_g_kvnorm[lyr][None], 'g_mla': mla_g_onorm[lyr][None],
        'w_uq': _arrange_w_uq(mla_w_uq[lyr]),
        'w_uk': w_uk.reshape(MLA_KV_RANK, MLA_HEADS * MLA_NOPE).astype(BF16),
        'w_uk_t': w_uk.transpose(1, 2, 0).astype(BF16),
        'w_uv': w_uv.reshape(MLA_KV_RANK, MLA_WIDTH).astype(BF16),
        'w_uv_h': w_uv.transpose(1, 0, 2).astype(BF16),
        'w_out': w_out[lyr].astype(BF16),
    }
    w_r = jnp.concatenate([moe_w_group[lyr], moe_w_expert[lyr],
                           jnp.zeros((d, LANES - N_GROUPS - N_EXPERTS), F32)], axis=1)
    w['w_r'] = jnp.concatenate(_split_bf16(w_r, 2), axis=1)
    w['b_r'] = jnp.concatenate([moe_b_group[lyr], moe_b_expert[lyr],
                                jnp.zeros((LANES - N_GROUPS - N_EXPERTS,), F32)])[None]

    pos_p = jnp.arange(lp, dtype=jnp.int32)
    pos_s = n_past + jnp.arange(ls, dtype=jnp.int32)
    s0_p = jnp.zeros((bp, GLA_HEADS, GLA_DK, GLA_DV), F32)
    (x1_p, h2_p, ids_p, wts_p), ckv_p, kr_p, st_p = _mixer(x_prompt, mods_p, pos_p, s0_p, None, w, prompt=True)
    (x1_s, h2_s, ids_s, wts_s), ckv_s, kr_s, st_s = _mixer(x_sample, mods_s, pos_s, state_gla[lyr],
                                                          (cache_ckv, cache_krope, page_table), w, prompt=False)

    e_id = jnp.concatenate([ids_p[:, :TOP_K], ids_s[:, :TOP_K]], axis=0).reshape(-1)
    y_rows = _moe(_moe_plan(e_id, bp * lp), h2_p, h2_s, moe_w_gu[lyr], moe_w_down[lyr])

    y_p = _final(x1_p, mods_p[5], y_rows, wts_p, g_final[None], 1, 512, 0)
    y_s = _final(x1_s, mods_s[5], y_rows, wts_s, g_final[None], 512 // ls, ls, bp * lp // 512)
    return (y_p, y_s, ckv_p[None], kr_p[None], st_p[None], ckv_s[None], kr_s[None], st_s[None])
```

```python
import functools
import math

import jax
import jax.numpy as jnp
from jax import lax
from jax.experimental import pallas as pl
from jax.experimental.pallas import tpu as pltpu

F32 = jnp.float32
BF16 = jnp.bfloat16

D_MODEL = 2048
PAGE_SIZE = 128
GLA_HEADS = 4
GLA_DK = 128
GLA_DV = 256
GLA_GATE_RANK = 16
GLA_TAU = 16.0
GLA_CHUNK = 64
GLA_QK = GLA_HEADS * GLA_DK
GLA_WIDTH = GLA_HEADS * GLA_DV
MLA_HEADS = 8
MLA_Q_RANK = 512
MLA_KV_RANK = 512
MLA_NOPE = 128
MLA_ROPE = 64
MLA_DV = 128
MLA_WIDTH = MLA_HEADS * MLA_DV
ROPE_BASE = 10000.0
ATTN_SCALE = (MLA_NOPE + MLA_ROPE) ** -0.5
N_GROUPS = 8
EXPERTS_PER_GROUP = 8
N_EXPERTS = N_GROUPS * EXPERTS_PER_GROUP
TOP_K = 2
EXPERT_FF = 1408
EPS = 1e-6

LANES = 128
QK_PAD = 2 * LANES
COL_GQ = 0
COL_GK = GLA_QK
COL_GV = 2 * GLA_QK
COL_GR = COL_GV + GLA_WIDTH
COL_CQ = COL_GR + GLA_WIDTH
COL_CKV = COL_CQ + MLA_Q_RANK
COL_KR = COL_CKV + MLA_KV_RANK
IN_COLS_PAD = COL_KR + LANES
GA_LANE = MLA_ROPE
IN_COL_TILE = IN_COLS_PAD // 3

MXU_COLS = 256
DMA_THREADS = 2
MOE_ROWS = 768
MOE_SUB = 256
MOE_HALF = MXU_COLS
MOE_COLS = 2 * MXU_COLS
MOE_GU_CHUNKS = -(-EXPERT_FF // MOE_HALF)
MOE_DN_CHUNKS = D_MODEL // MOE_COLS
MOE_CHUNKS = MOE_GU_CHUNKS + MOE_DN_CHUNKS
MOE_RING = 5
VMEM_LIMIT = 56 * 1024 * 1024


def _params(sem, vmem=VMEM_LIMIT):
    return pltpu.CompilerParams(dimension_semantics=sem, vmem_limit_bytes=vmem)


def _silu(x):
    return x * jax.nn.sigmoid(x)


def _rms(x, g):
    return x * lax.rsqrt(jnp.mean(x * x, axis=-1, keepdims=True) + EPS) * g


def _ada_kernel(c_ref, w_ref, b_ref, o_ref):
    a = _silu(c_ref[...]).astype(BF16)
    o_ref[...] = jnp.dot(a, w_ref[...].astype(BF16), preferred_element_type=F32) + b_ref[...]


def _ada(c, w, b):
    rows, d = c.shape
    n = w.shape[1]
    tn = 1024
    return pl.pallas_call(
        _ada_kernel,
        grid=(n // tn,),
        in_specs=[pl.BlockSpec((rows, d), lambda j: (0, 0)),
                  pl.BlockSpec((d, tn), lambda j: (0, j)),
                  pl.BlockSpec((1, tn), lambda j: (0, j))],
        out_specs=pl.BlockSpec((rows, tn), lambda j: (0, j)),
        out_shape=jax.ShapeDtypeStruct((rows, n), F32),
        compiler_params=_params(("arbitrary",)),
        name="ada_mod",
    )(c, w, b)


def _inproj_kernel(x_ref, g_ref, sc_ref, sh_ref, w_hbm, o_ref, w_scr, h_scr, sem):
    @pl.when(pl.program_id(0) == 0)
    def _():
        cp = pltpu.make_async_copy(w_hbm, w_scr, sem)
        cp.start()
        cp.wait()

    h = _rms(x_ref[...], g_ref[...][None]) * (1.0 + sc_ref[...]) + sh_ref[...]
    h_scr[...] = h.reshape(h_scr.shape).astype(BF16)
    for j in range(w_scr.shape[1] // IN_COL_TILE):
        cols = slice(j * IN_COL_TILE, (j + 1) * IN_COL_TILE)
        o_ref[:, cols] = jnp.dot(h_scr[...], w_scr[:, cols], preferred_element_type=F32)


def _inproj(x, g, sc, sh, w, bb, lt):
    bsz, seq, d = x.shape
    rows = bb * lt
    nl = seq // lt
    n = w.shape[1]
    return pl.pallas_call(
        _inproj_kernel,
        grid=(bsz * seq // rows,),
        in_specs=[pl.BlockSpec((bb, lt, d), lambda i: (i // nl, i % nl, 0)),
                  pl.BlockSpec((1, d), lambda i: (0, 0)),
                  pl.BlockSpec((bb, 1, d), lambda i: (i // nl, 0, 0)),
                  pl.BlockSpec((bb, 1, d), lambda i: (i // nl, 0, 0)),
                  pl.BlockSpec(memory_space=pl.ANY)],
        out_specs=pl.BlockSpec((rows, n), lambda i: (i, 0)),
        out_shape=jax.ShapeDtypeStruct((bsz * seq, n), F32),
        scratch_shapes=[pltpu.VMEM(w.shape, BF16), pltpu.VMEM((rows, d), BF16), pltpu.SemaphoreType.DMA(())],
        compiler_params=_params(("arbitrary",)),
        name="in_proj",
    )(x, g, sc, sh, w)


def _split_bf16(x, pieces):
    out = []
    for _ in range(pieces):
        part = x.astype(BF16)
        out.append(part)
        x = x - part.astype(F32)
    return out


def _gla_kernel(q_ref, k_ref, v_ref, gr_ref, ga_ref, wg_ref, bg_ref, gon_ref, tri_b_ref, tri_f_ref, s0_ref,
                o_ref, sout_ref, s_scr, *, n_chunks, n_valid):
    c = GLA_CHUNK
    l_step = pl.program_id(2)

    @pl.when(l_step == 0)
    def _():
        s_scr[...] = s0_ref[...]

    chunks_per_batch = n_chunks // s_scr.shape[0]

    def load(ref):
        x = ref[...]
        if n_valid == c:
            return x
        zeros = jnp.zeros((c - n_valid, x.shape[1]), x.dtype)
        return jnp.concatenate([part for i in range(n_chunks)
                                for part in (x[i * n_valid:(i + 1) * n_valid], zeros)], axis=0)

    q = load(q_ref) * (GLA_DK ** -0.5)
    k = load(k_ref)
    v = load(v_ref).astype(BF16)
    a_hi, a_lo = _split_bf16(load(ga_ref), 2)
    w_hi, w_lo = _split_bf16(wg_ref[...], 2)
    z = jnp.dot(jnp.concatenate([a_hi, a_lo, a_hi], axis=1), jnp.concatenate([w_hi, w_hi, w_lo], axis=0),
                preferred_element_type=F32) + bg_ref[...]
    log_a = (jnp.minimum(z, 0.0) - jnp.log1p(jnp.exp(-jnp.abs(z)))) / GLA_TAU
    if n_valid < c:
        log_a = jnp.where(lax.broadcasted_iota(jnp.int32, log_a.shape, 0) % c < n_valid, log_a, 0.0)
    cum = jnp.dot(tri_b_ref[...], jnp.concatenate(_split_bf16(log_a, 3), axis=1), preferred_element_type=F32)
    b = cum[:, :GLA_DK] + cum[:, GLA_DK:2 * GLA_DK] + cum[:, 2 * GLA_DK:]
    q_dec = (q * jnp.exp(b)).astype(BF16)
    k_inv = (k * jnp.exp(-b)).astype(BF16)
    att = lax.dot_general(q_dec, k_inv, (((1,), (1,)), ((), ())), preferred_element_type=F32)
    att = jnp.where(tri_f_ref[...] > 0.0, att, 0.0).astype(BF16)
    o_intra = jnp.dot(att, v, preferred_element_type=F32)

    states = {}
    outs = []
    for idx in range(n_chunks):
        bi = idx // chunks_per_batch
        rows = slice(idx * c, (idx + 1) * c)
        s_old = states[bi] if bi in states else s_scr[bi]
        b_last = b[(idx + 1) * c - 1:(idx + 1) * c, :]
        o = o_intra[rows] + jnp.dot(q_dec[rows], s_old.astype(BF16), preferred_element_type=F32)
        k_dec = (k[rows] * jnp.exp(b_last - b[rows])).astype(BF16)
        kv = lax.dot_general(k_dec, v[rows], (((0,), (0,)), ((), ())), preferred_element_type=F32)
        decay = jnp.exp(jnp.broadcast_to(b_last, (GLA_DK, GLA_DK)).T)
        states[bi] = jnp.concatenate([decay] * (GLA_DV // GLA_DK), axis=1) * s_old + kv
        outs.append(o[:n_valid])
    for bi, s_new in states.items():
        s_scr[bi] = s_new
    o_all = _rms(jnp.concatenate(outs, axis=0), gon_ref[...])
    o_ref[...] = o_all * _silu(gr_ref[...])

    @pl.when(l_step == pl.num_programs(2) - 1)
    def _():
        sout_ref[...] = s_scr[...]


def _gla(proj, s0, wg_pad, bg, gon, bsz, seq, bb, lt):
    n_valid = math.gcd(seq, GLA_CHUNK)
    rows = bb * lt
    nl = seq // lt
    rmap = lambda b, h, l: b * nl + l
    n_chunks = rows // n_valid
    kern = functools.partial(_gla_kernel, n_chunks=n_chunks, n_valid=n_valid)
    st_spec = pl.BlockSpec((bb, None, GLA_DK, GLA_DV), lambda b, h, l: (b, h, 0, 0))
    pos = jnp.arange(n_chunks * GLA_CHUNK)
    tri = (pos[:, None] // GLA_CHUNK == pos[None, :] // GLA_CHUNK) & (pos[None, :] <= pos[:, None])
    tri_spec = pl.BlockSpec(tri.shape, lambda b, h, l: (0, 0))
    return pl.pallas_call(
        kern,
        grid=(bsz // bb, GLA_HEADS, nl),
        in_specs=[pl.BlockSpec((rows, GLA_DK), lambda b, h, l: (rmap(b, h, l), COL_GQ // GLA_DK + h)),
                  pl.BlockSpec((rows, GLA_DK), lambda b, h, l: (rmap(b, h, l), COL_GK // GLA_DK + h)),
                  pl.BlockSpec((rows, GLA_DV), lambda b, h, l: (rmap(b, h, l), COL_GV // GLA_DV + h)),
                  pl.BlockSpec((rows, GLA_DV), lambda b, h, l: (rmap(b, h, l), COL_GR // GLA_DV + h)),
                  pl.BlockSpec((rows, LANES), lambda b, h, l: (rmap(b, h, l), COL_KR // LANES)),
                  pl.BlockSpec((LANES, GLA_DK), lambda b, h, l: (0, h)),
                  pl.BlockSpec((1, GLA_DK), lambda b, h, l: (0, h)),
                  pl.BlockSpec((1, GLA_DV), lambda b, h, l: (0, 0)),
                  tri_spec, tri_spec, st_spec],
        out_specs=[pl.BlockSpec((rows, GLA_DV), lambda b, h, l: (rmap(b, h, l), h)), st_spec],
        out_shape=[jax.ShapeDtypeStruct((bsz * seq, GLA_WIDTH), F32),
                   jax.ShapeDtypeStruct((bsz, GLA_HEADS, GLA_DK, GLA_DV), F32)],
        scratch_shapes=[pltpu.VMEM((bb, GLA_DK, GLA_DV), F32)],
        compiler_params=_params(("arbitrary", "arbitrary", "arbitrary")),
        name="gla",
    )(proj, proj, proj, proj, proj, wg_pad, bg, gon, tri.astype(BF16), tri.astype(F32), s0)


def _rope(r, cos, sin_lo, sin_hi):
    return (r * cos + pltpu.roll(r, LANES - MLA_ROPE // 2, axis=1) * sin_lo
            + pltpu.roll(r, MLA_ROPE // 2, axis=1) * sin_hi)


def _mla_proj_kernel(cq_ref, ckv_ref, kr_ref, cos_ref, slo_ref, shi_ref, gq_ref, gkv_ref, wq_ref,
                     wk_ref, wv_ref, *out_refs, absorbed):
    cos, slo, shi = cos_ref[...], slo_ref[...], shi_ref[...]
    cqn = _rms(cq_ref[...], gq_ref[...]).astype(BF16)
    qf = jnp.dot(cqn, wq_ref[...], preferred_element_type=F32)
    ckv_n = _rms(ckv_ref[...], gkv_ref[...])
    k_rot = _rope(kr_ref[...], cos, slo, shi)
    if absorbed:
        qlat_ref, qr_ref, ckvn_ref, krope_ref = out_refs
    else:
        qcat_ref, kcat_ref, v_ref, ckvn_ref, krope_ref = out_refs
    ckvn_ref[...] = ckv_n
    krope_ref[...] = k_rot[:, :MLA_ROPE]
    for h in range(MLA_HEADS):
        q_nope = qf[:, h * QK_PAD:h * QK_PAD + LANES] * ATTN_SCALE
        q_rot = _rope(qf[:, h * QK_PAD + LANES:(h + 1) * QK_PAD], cos, slo, shi) * ATTN_SCALE
        if absorbed:
            qlat_ref[h] = jnp.dot(q_nope.astype(BF16), wk_ref[h], preferred_element_type=F32)
            qr_ref[h] = q_rot
        else:
            qcat_ref[:, h * QK_PAD:h * QK_PAD + LANES] = q_nope.astype(BF16)
            qcat_ref[:, h * QK_PAD + LANES:(h + 1) * QK_PAD] = q_rot.astype(BF16)
    if not absorbed:
        ckv_b = ckv_n.astype(BF16)
        k_nope = jnp.dot(ckv_b, wk_ref[...], preferred_element_type=F32)
        k_rot_b = k_rot.astype(BF16)
        for h in range(MLA_HEADS):
            kcat_ref[:, h * QK_PAD:h * QK_PAD + LANES] = k_nope[:, h * LANES:(h + 1) * LANES].astype(BF16)
            kcat_ref[:, h * QK_PAD + LANES:(h + 1) * QK_PAD] = k_rot_b
        v_ref[...] = jnp.dot(ckv_b, wv_ref[...], preferred_element_type=F32).astype(BF16)


def _mla_proj(proj, tabs, gq, gkv, wq, wk, wv, tm, absorbed):
    t = proj.shape[0]
    ntab = tabs[0].shape[0] // tm
    row = lambda c: pl.BlockSpec((tm, c), lambda i: (i, 0))
    full = lambda a: pl.BlockSpec(a.shape, lambda i: (0,) * a.ndim)
    tab = pl.BlockSpec((tm, LANES), lambda i: (i % ntab, 0))
    in_specs = [pl.BlockSpec((tm, MLA_Q_RANK), lambda i: (i, COL_CQ // MLA_Q_RANK)),
                pl.BlockSpec((tm, MLA_KV_RANK), lambda i: (i, COL_CKV // MLA_KV_RANK)),
                pl.BlockSpec((tm, LANES), lambda i: (i, COL_KR // LANES)),
                tab, tab, tab, full(gq), full(gkv), full(wq), full(wk), full(wv)]
    if absorbed:
        out_specs = [pl.BlockSpec((MLA_HEADS, tm, MLA_KV_RANK), lambda i: (0, i, 0)),
                     pl.BlockSpec((MLA_HEADS, tm, LANES), lambda i: (0, i, 0))]
        out_shape = [jax.ShapeDtypeStruct((MLA_HEADS, t, MLA_KV_RANK), F32),
                     jax.ShapeDtypeStruct((MLA_HEADS, t, LANES), F32)]
    else:
        out_specs = [row(MLA_HEADS * QK_PAD), row(MLA_HEADS * QK_PAD), row(MLA_WIDTH)]
        out_shape = [jax.ShapeDtypeStruct((t, MLA_HEADS * QK_PAD), BF16),
                     jax.ShapeDtypeStruct((t, MLA_HEADS * QK_PAD), BF16),
                     jax.ShapeDtypeStruct((t, MLA_WIDTH), BF16)]
    out_specs += [row(MLA_KV_RANK), row(MLA_ROPE)]
    out_shape += [jax.ShapeDtypeStruct((t, MLA_KV_RANK), F32), jax.ShapeDtypeStruct((t, MLA_ROPE), F32)]
    return pl.pallas_call(
        functools.partial(_mla_proj_kernel, absorbed=absorbed),
        grid=(t // tm,),
        in_specs=in_specs, out_specs=out_specs, out_shape=out_shape,
        compiler_params=_params(("arbitrary",)),
        name="mla_proj_absorbed" if absorbed else "mla_proj",
    )(proj, proj, proj, *tabs, gq, gkv, wq, wk, wv)


ATTN_HEADS_PER_STEP = 2


def _attn_kernel(*refs, tq, tk, n_heads):
    assert tq == tk
    heads = [refs[3 * i:3 * i + 3] for i in range(n_heads)]
    g_ref, o_ref = refs[3 * n_heads], refs[3 * n_heads + 1]
    qi = pl.program_id(2)
    qs = [q_ref[...] for q_ref, _, _ in heads]

    def block(j, carries, on_diagonal):
        k0 = pl.multiple_of(j * tk, tk)
        out = []
        for (_, k_ref, v_ref), q, (m, l, acc) in zip(heads, qs, carries):
            s = lax.dot_general(q, k_ref[pl.ds(k0, tk), :], (((1,), (1,)), ((), ())),
                                preferred_element_type=F32)
            if on_diagonal:
                row = lax.broadcasted_iota(jnp.int32, (tq, tk), 0)
                col = lax.broadcasted_iota(jnp.int32, (tq, tk), 1)
                s = jnp.where(col <= row, s, -jnp.inf)
            m_new = jnp.maximum(m, jnp.max(s, axis=-1, keepdims=True))
            alpha = jnp.exp(m - m_new)
            p = jnp.exp(s - m_new)
            l = alpha * l + jnp.sum(p, axis=-1, keepdims=True)
            acc = alpha * acc + jnp.dot(p.astype(BF16), v_ref[pl.ds(k0, tk), :], preferred_element_type=F32)
            out.append((m_new, l, acc))
        return tuple(out)

    init = tuple((jnp.full((tq, 1), -jnp.inf, F32), jnp.zeros((tq, 1), F32), jnp.zeros((tq, MLA_DV), F32))
                 for _ in range(n_heads))
    carries = lax.fori_loop(0, qi, lambda j, c: block(j, c, False), init)
    for t, (_, l, acc) in enumerate(block(qi, carries, True)):
        o_ref[:, t * MLA_DV:(t + 1) * MLA_DV] = _rms(acc / l, g_ref[...])


def _prompt_attention(q_cat, k_cat, v, g, bsz, seq, tq, tk):
    nq = seq // tq
    n = ATTN_HEADS_PER_STEP
    head_specs = []
    for t in range(n):
        head_specs += [pl.BlockSpec((tq, QK_PAD), lambda b, h, i, t=t: (b * nq + i, h * n + t)),
                       pl.BlockSpec((seq, QK_PAD), lambda b, h, i, t=t: (b, h * n + t)),
                       pl.BlockSpec((seq, MLA_DV), lambda b, h, i, t=t: (b, h * n + t))]
    return pl.pallas_call(
        functools.partial(_attn_kernel, tq=tq, tk=tk, n_heads=n),
        grid=(bsz, MLA_HEADS // n, nq),
        in_specs=head_specs + [pl.BlockSpec((1, MLA_DV), lambda b, h, i: (0, 0))],
        out_specs=pl.BlockSpec((tq, n * MLA_DV), lambda b, h, i: (b * nq + i, h)),
        out_shape=jax.ShapeDtypeStruct((bsz * seq, MLA_WIDTH), F32),
        compiler_params=_params(("arbitrary", "arbitrary", "arbitrary")),
        name="prompt_attention",
    )(*([q_cat, k_cat, v] * n), g)


def _paged_attn_kernel(pt_ref, qlat_ref, qr_ref, cnew_ref, knew_ref, ckv_hbm, krt_hbm, o_ref,
                       ckv_buf, krt_buf, sems, m_scr, l_scr, acc_scr, *, pp, n_q):
    grp = pl.program_id(1)
    n_grp = pl.num_programs(1)
    step = pl.program_id(0) * n_grp + grp
    n_steps = pl.num_programs(0) * n_grp
    slot = step % 2
    rows = MLA_HEADS * n_q

    def page_copies(at_step, at_slot):
        out = []
        for j in range(pp):
            page = pt_ref[at_step * pp + j]
            out.append(pltpu.make_async_copy(ckv_hbm.at[0, page], ckv_buf.at[at_slot, j], sems.at[0, at_slot]))
            out.append(pltpu.make_async_copy(krt_hbm.at[0, page],
                                             krt_buf.at[at_slot, :, pl.ds(j * PAGE_SIZE, PAGE_SIZE)],
                                             sems.at[1, at_slot]))
        return out

    def start_pages(at_step, at_slot):
        for i, cp in enumerate(page_copies(at_step, at_slot)):
            cp.start(priority=(i // 2 + i) % DMA_THREADS)

    @pl.when(step == 0)
    def _():
        start_pages(step, slot)

    @pl.when(step + 1 < n_steps)
    def _():
        start_pages(step + 1, 1 - slot)

    @pl.when(grp == 0)
    def _():
        m_scr[...] = jnp.full(m_scr.shape, -jnp.inf, F32)
        l_scr[...] = jnp.zeros(l_scr.shape, F32)
        acc_scr[...] = jnp.zeros(acc_scr.shape, F32)

    q = qlat_ref[...].reshape(rows, MLA_KV_RANK).astype(BF16)
    qr = qr_ref[...].reshape(rows, LANES)[:, :MLA_ROPE].astype(BF16)

    def update(s, v):
        m = m_scr[...]
        m_new = jnp.maximum(m, jnp.max(s, axis=-1, keepdims=True))
        alpha = jnp.exp(m - m_new)
        p = jnp.exp(s - m_new)
        l_scr[...] = alpha * l_scr[...] + jnp.sum(p, axis=-1, keepdims=True)
        acc_scr[...] = acc_scr[...] * alpha + jnp.dot(p.astype(BF16), v, preferred_element_type=F32)
        m_scr[...] = m_new

    for cp in page_copies(step, slot):
        cp.wait()
    ckv = ckv_buf[slot].reshape(pp * PAGE_SIZE, MLA_KV_RANK).astype(BF16)
    s_past = (lax.dot_general(q, ckv, (((1,), (1,)), ((), ())), preferred_element_type=F32)
              + jnp.dot(qr, krt_buf[slot].astype(BF16), preferred_element_type=F32))
    update(s_past, ckv)

    @pl.when(grp == n_grp - 1)
    def _():
        pad = PAGE_SIZE - n_q
        c_new = jnp.concatenate([cnew_ref[...], jnp.zeros((pad, MLA_KV_RANK), F32)], axis=0).astype(BF16)
        k_new = jnp.concatenate([knew_ref[...], jnp.zeros((pad, MLA_ROPE), F32)], axis=0).astype(BF16)
        s = (lax.dot_general(q, c_new, (((1,), (1,)), ((), ())), preferred_element_type=F32)
             + lax.dot_general(qr, k_new, (((1,), (1,)), ((), ())), preferred_element_type=F32))
        q_idx = lax.broadcasted_iota(jnp.int32, s.shape, 0) % n_q
        k_idx = lax.broadcasted_iota(jnp.int32, s.shape, 1)
        update(jnp.where(k_idx <= q_idx, s, -jnp.inf), c_new)
        o_ref[...] = (acc_scr[...] / l_scr[...]).reshape(o_ref.shape)


def _paged_attention(q_lat, q_r, ckv_n, k_rope, cache_ckv, cache_krope_t, page_table, n_q, pp):
    bsz, n_pages = page_table.shape
    rows = MLA_HEADS * n_q
    grid_spec = pltpu.PrefetchScalarGridSpec(
        num_scalar_prefetch=1,
        grid=(bsz, n_pages // pp),
        in_specs=[pl.BlockSpec((MLA_HEADS, n_q, MLA_KV_RANK), lambda b, s, pt: (0, b, 0)),
                  pl.BlockSpec((MLA_HEADS, n_q, LANES), lambda b, s, pt: (0, b, 0)),
                  pl.BlockSpec((n_q, MLA_KV_RANK), lambda b, s, pt: (b, 0)),
                  pl.BlockSpec((n_q, MLA_ROPE), lambda b, s, pt: (b, 0)),
                  pl.BlockSpec(memory_space=pl.ANY),
                  pl.BlockSpec(memory_space=pl.ANY)],
        out_specs=pl.BlockSpec((MLA_HEADS, n_q, MLA_KV_RANK), lambda b, s, pt: (0, b, 0)),
        scratch_shapes=[pltpu.VMEM((2, pp, PAGE_SIZE, MLA_KV_RANK), F32),
                        pltpu.VMEM((2, MLA_ROPE, pp * PAGE_SIZE), F32),
                        pltpu.SemaphoreType.DMA((2, 2)),
                        pltpu.VMEM((rows, 1), F32), pltpu.VMEM((rows, 1), F32),
                        pltpu.VMEM((rows, MLA_KV_RANK), F32)],
    )
    return pl.pallas_call(
        functools.partial(_paged_attn_kernel, pp=pp, n_q=n_q),
        grid_spec=grid_spec,
        out_shape=jax.ShapeDtypeStruct((MLA_HEADS, bsz * n_q, MLA_KV_RANK), F32),
        compiler_params=_params(("arbitrary", "arbitrary")),
        name="paged_attention",
    )(page_table.reshape(-1), q_lat, q_r, ckv_n, k_rope, cache_ckv, cache_krope_t)


def _latent_out_kernel(o_ref, w_ref, g_ref, out_ref):
    o = jnp.dot(o_ref[...].astype(BF16), w_ref[...], preferred_element_type=F32)
    out_ref[...] = _rms(o, g_ref[...])


def _latent_out(o_lat, w_uv_h, g, tm):
    t = o_lat.shape[1]
    return pl.pallas_call(
        _latent_out_kernel,
        grid=(MLA_HEADS, t // tm),
        in_specs=[pl.BlockSpec((None, tm, MLA_KV_RANK), lambda h, i: (h, i, 0)),
                  pl.BlockSpec((None, MLA_KV_RANK, MLA_DV), lambda h, i: (h, 0, 0)),
                  pl.BlockSpec((1, MLA_DV), lambda h, i: (0, 0))],
        out_specs=pl.BlockSpec((tm, MLA_DV), lambda h, i: (i, h)),
        out_shape=jax.ShapeDtypeStruct((t, MLA_WIDTH), F32),
        compiler_params=_params(("arbitrary", "arbitrary")),
        name="latent_out",
    )(o_lat, w_uv_h, g)


def _route(logit):
    lane = lax.broadcasted_iota(jnp.int32, logit.shape, 1)
    neg = -jnp.inf
    big = jnp.int32(LANES)

    def first_max(vals):
        top = jnp.max(vals, axis=-1, keepdims=True)
        return top, jnp.min(jnp.where(vals == top, lane, big), axis=-1, keepdims=True)

    g_vals = jnp.where(lane < N_GROUPS, logit, neg)
    g_max, g_idx = first_max(g_vals)
    g_w = 1.0 / jnp.sum(jnp.exp(g_vals - g_max), axis=-1, keepdims=True)
    e_lane = lane - N_GROUPS
    in_group = (e_lane >= g_idx * EXPERTS_PER_GROUP) & (e_lane < (g_idx + 1) * EXPERTS_PER_GROUP)
    e_vals = jnp.where(in_group, logit, neg)
    v1, i1 = first_max(e_vals)
    v2, i2 = first_max(jnp.where(lane == i1, neg, e_vals))
    e2 = jnp.exp(v2 - v1)
    w1 = g_w / (1.0 + e2)
    w2 = g_w * e2 / (1.0 + e2)
    ids = jnp.where(lane == 0, i1 - N_GROUPS, jnp.where(lane == 1, i2 - N_GROUPS, 0))
    wts = jnp.where(lane == 0, w1, jnp.where(lane == 1, w2, 0.0))
    return ids, wts


def _outproj_kernel(og_ref, om_ref, wg_ref, wm_ref, x_ref, gt_ref, sc_ref, sh_ref, g_ref, wr_ref, br_ref,
                    x1_ref, h2_ref, id_ref, wt_ref):
    y = (jnp.dot(og_ref[...].astype(BF16), wg_ref[...], preferred_element_type=F32)
         + jnp.dot(om_ref[...].astype(BF16), wm_ref[...], preferred_element_type=F32))
    x1 = x_ref[...] + gt_ref[...] * y.reshape(x_ref.shape)
    x1_ref[...] = x1
    h2 = (_rms(x1, g_ref[...][None]) * (1.0 + sc_ref[...]) + sh_ref[...]).reshape(h2_ref.shape)
    h2_ref[...] = h2
    rows = h2.shape[0]
    lg = jnp.dot(jnp.concatenate(_split_bf16(h2, 2), axis=0), wr_ref[...], preferred_element_type=F32)
    logit = (lg[:rows, :LANES] + lg[:rows, LANES:] + lg[rows:, :LANES] + lg[rows:, LANES:]) + br_ref[...]
    id_ref[...], wt_ref[...] = _route(logit)


def _outproj(o_gla, o_mla, w_out, x, gt, sc, sh, g, w_r, b_r, bb, lt):
    bsz, seq, d = x.shape
    rows = bb * lt
    nl = seq // lt
    xs = pl.BlockSpec((bb, lt, d), lambda i: (i // nl, i % nl, 0))
    ms = pl.BlockSpec((bb, 1, d), lambda i: (i // nl, 0, 0))
    full = lambda a: pl.BlockSpec(a.shape, lambda i: (0,) * a.ndim)
    wide = pl.BlockSpec((rows, LANES), lambda i: (i, 0))
    return pl.pallas_call(
        _outproj_kernel,
        grid=(bsz * seq // rows,),
        in_specs=[pl.BlockSpec((rows, GLA_WIDTH), lambda i: (i, 0)),
                  pl.BlockSpec((rows, MLA_WIDTH), lambda i: (i, 0)),
                  pl.BlockSpec((GLA_WIDTH, d), lambda i: (0, 0)),
                  pl.BlockSpec((MLA_WIDTH, d), lambda i: (GLA_WIDTH // MLA_WIDTH, 0)),
                  xs, ms, ms, ms, full(g), full(w_r), full(b_r)],
        out_specs=[xs, pl.BlockSpec((rows, d), lambda i: (i, 0)), wide, wide],
        out_shape=[jax.ShapeDtypeStruct(x.shape, F32), jax.ShapeDtypeStruct((bsz * seq, d), F32),
                   jax.ShapeDtypeStruct((bsz * seq, LANES), jnp.int32),
                   jax.ShapeDtypeStruct((bsz * seq, LANES), F32)],
        compiler_params=_params(("arbitrary",)),
        name="out_proj_route",
    )(o_gla, o_mla, w_out, w_out, x, gt, sc, sh, g, w_r, b_r)


def _moe_kernel(ie_ref, ir_ref, in_ref, im_ref, nl_ref, src_ref, dst_ref,
                hp_hbm, hs_hbm, wgu_hbm, wd_hbm, y_hbm,
                x_buf, xb_scr, wb_scr, act_scr, y_buf, ring, gsem, ssem, wsem):
    n_live = nl_ref[0]
    last_item = ie_ref.shape[0] - 1

    def chunk_width(j):
        return min(MOE_HALF, EXPERT_FF - j * MOE_HALF)

    def weight_copies(e, j, slot):
        sem = wsem.at[slot]
        if j < MOE_GU_CHUNKS:
            w = chunk_width(j)
            return [pltpu.make_async_copy(wgu_hbm.at[e, :, pl.ds(off + j * MOE_HALF, w)],
                                          ring.at[slot, :, pl.ds(half * w, w)], sem)
                    for half, off in enumerate((0, EXPERT_FF))]
        c = j - MOE_GU_CHUNKS
        return [pltpu.make_async_copy(wd_hbm.at[e, :, pl.ds(c * MOE_COLS, MOE_COLS)],
                                      ring.at[slot, pl.ds(0, EXPERT_FF)], sem)]

    def start_weights(e, j, slot):
        for idx, cp in enumerate(weight_copies(e, j, slot)):
            cp.start(priority=(j + idx) % DMA_THREADS)

    def gather_copy(h_hbm, src_row, p, r, k=1):
        return pltpu.make_async_copy(h_hbm.at[pl.ds(src_row, k)], x_buf.at[p, pl.ds(r, k)], gsem.at[p])

    def scatter_copy(r, dst_row, k=1):
        return pltpu.make_async_copy(y_buf.at[pl.ds(r, k)], y_hbm.at[pl.ds(dst_row, k)], ssem)

    def for_rows(lo, hi, fn, unroll=4):
        groups = (hi - lo) // unroll

        def group(g, c):
            for u in range(unroll):
                fn(lo + g * unroll + u)
            return c

        def single(r, c):
            fn(r)
            return c

        lax.fori_loop(0, groups, group, 0)
        lax.fori_loop(lo + groups * unroll, hi, single, 0)

    def wait_rows(n, copy_of_rows):
        k = 1
        while k <= MOE_ROWS:
            @pl.when((n & k) != 0)
            def _(k=k):
                copy_of_rows(k).wait()
            k *= 2

    def start_gather(i, p):
        row0, n, m = ir_ref[i], in_ref[i], im_ref[i]
        for_rows(0, m, lambda r: gather_copy(hp_hbm, src_ref[row0 + r], p, r).start())
        for_rows(m, n, lambda r: gather_copy(hs_hbm, src_ref[row0 + r], p, r).start())

    x_buf[...] = jnp.zeros(x_buf.shape, F32)
    start_gather(0, 0)
    for j in range(MOE_RING):
        start_weights(ie_ref[0], j, j)

    def item_body(i, carry):
        p = i % 2
        e, row0, n = ie_ref[i], ir_ref[i], in_ref[i]
        n_sub = (n + MOE_SUB - 1) // MOE_SUB
        has_next = i + 1 < n_live
        e_next = ie_ref[jnp.minimum(i + 1, last_item)]
        base = (i * MOE_CHUNKS) % MOE_RING

        wait_rows(n, lambda k: gather_copy(hp_hbm, 0, p, 0, k))

        @pl.when(has_next)
        def _():
            start_gather(i + 1, 1 - p)

        def for_row_blocks(fn):
            pairs = n_sub // 2

            def pair(b, c):
                fn(pl.multiple_of(b * 2 * MOE_SUB, 2 * MOE_SUB), 2 * MOE_SUB)
                return c
            lax.fori_loop(0, pairs, pair, 0)

            @pl.when(n_sub % 2 == 1)
            def _():
                fn(pl.multiple_of(pairs * 2 * MOE_SUB, 2 * MOE_SUB), MOE_SUB)

        def cast_rows(sb, c):
            r0 = pl.multiple_of(sb * MOE_SUB, MOE_SUB)
            xb_scr[pl.ds(r0, MOE_SUB), :] = x_buf[p, pl.ds(r0, MOE_SUB), :].astype(BF16)
            return c
        lax.fori_loop(0, n_sub, cast_rows, 0)

        for j in range(MOE_CHUNKS):
            slot = (base + j) % MOE_RING
            if j == MOE_GU_CHUNKS:
                @pl.when(i > 0)
                def _():
                    n_prev = in_ref[jnp.maximum(i - 1, 0)]
                    wait_rows(n_prev, lambda k: scatter_copy(0, 0, k))
            for cp in weight_copies(e, j, slot):
                cp.wait()
            if j < MOE_GU_CHUNKS:
                cols = 2 * chunk_width(j)
                wb_scr[:, :cols] = ring[slot, :, pl.ds(0, cols)].astype(BF16)
            else:
                wb_scr[:EXPERT_FF, :] = ring[slot, pl.ds(0, EXPERT_FF), :].astype(BF16)
            j_ahead = j + MOE_RING
            if j_ahead < MOE_CHUNKS:
                start_weights(e, j_ahead, slot)
            else:
                @pl.when(has_next)
                def _(j_ahead=j_ahead, slot=slot):
                    start_weights(e_next, j_ahead - MOE_CHUNKS, slot)

            if j < MOE_GU_CHUNKS:
                w = chunk_width(j)

                def gate_up(r0, rows, j=j, w=w):
                    res = jnp.dot(xb_scr[pl.ds(r0, rows), :], wb_scr[:, :2 * w], preferred_element_type=F32)
                    act = _silu(res[:, :w]) * res[:, w:]
                    act_scr[pl.ds(r0, rows), j * MOE_HALF:j * MOE_HALF + w] = act.astype(BF16)
                for_row_blocks(gate_up)
            else:
                col0 = (j - MOE_GU_CHUNKS) * MOE_COLS

                def down(r0, rows, col0=col0):
                    y_buf[pl.ds(r0, rows), col0:col0 + MOE_COLS] = jnp.dot(
                        act_scr[pl.ds(r0, rows), :], wb_scr[:EXPERT_FF, :], preferred_element_type=F32)
                for_row_blocks(down)

        for_rows(0, n, lambda r: scatter_copy(r, dst_ref[row0 + r]).start())
        return carry

    lax.fori_loop(0, n_live, item_body, 0)
    wait_rows(in_ref[jnp.maximum(n_live - 1, 0)], lambda k: scatter_copy(0, 0, k))


def _moe(plan, h2_p, h2_s, w_gu, w_down):
    d = h2_p.shape[1]
    t_all = h2_p.shape[0] + h2_s.shape[0]
    hbm = pl.BlockSpec(memory_space=pl.ANY)
    grid_spec = pltpu.PrefetchScalarGridSpec(
        num_scalar_prefetch=len(plan),
        grid=(1,),
        in_specs=[hbm, hbm, hbm, hbm],
        out_specs=hbm,
        scratch_shapes=[pltpu.VMEM((2, MOE_ROWS, d), F32),
                        pltpu.VMEM((MOE_ROWS, d), BF16),
                        pltpu.VMEM((d, MOE_COLS), BF16),
                        pltpu.VMEM((MOE_ROWS, EXPERT_FF), BF16),
                        pltpu.VMEM((MOE_ROWS, d), F32),
                        pltpu.VMEM((MOE_RING, d, MOE_COLS), F32),
                        pltpu.SemaphoreType.DMA((2,)),
                        pltpu.SemaphoreType.DMA(()),
                        pltpu.SemaphoreType.DMA((MOE_RING,))],
    )
    return pl.pallas_call(
        _moe_kernel,
        grid_spec=grid_spec,
        out_shape=jax.ShapeDtypeStruct((t_all * TOP_K, d), F32),
        compiler_params=_params(("arbitrary",)),
        name="moe_experts",
    )(*plan, h2_p, h2_s, w_gu, w_down)


def _moe_plan(e_id, t_prompt):
    n_as = e_id.shape[0]
    n_items = N_EXPERTS + n_as // MOE_ROWS
    order = jnp.argsort(e_id).astype(jnp.int32)
    cnt = jnp.zeros((N_EXPERTS,), jnp.int32).at[e_id].add(1)
    cnt_p = jnp.zeros((N_EXPERTS,), jnp.int32).at[e_id[:t_prompt * TOP_K]].add(1)
    ustart = jnp.cumsum(cnt) - cnt
    per_e = (cnt + MOE_ROWS - 1) // MOE_ROWS
    item_end = jnp.cumsum(per_e)
    total = item_end[-1]
    idx = jnp.arange(n_items, dtype=jnp.int32)
    live = idx < total
    e_of = jnp.sum(item_end[None, :] <= jnp.minimum(idx, total - 1)[:, None], axis=1).astype(jnp.int32)
    local = jnp.minimum(idx, total - 1) - (item_end[e_of] - per_e[e_of])
    row0 = ustart[e_of] + local * MOE_ROWS
    n_rows = jnp.where(live, jnp.minimum(MOE_ROWS, cnt[e_of] - local * MOE_ROWS), 0)
    m_rows = jnp.clip(cnt_p[e_of] - local * MOE_ROWS, 0, n_rows)
    tok_s = order // TOP_K
    src_s = jnp.where(tok_s < t_prompt, tok_s, tok_s - t_prompt)
    dst_s = (order % TOP_K) * (n_as // TOP_K) + tok_s
    i32 = lambda a: a.astype(jnp.int32)
    return (i32(e_of), i32(row0), i32(n_rows), i32(m_rows), i32(total)[None], i32(src_s), i32(dst_s))


def _final_kernel(x_ref, gt_ref, y0_ref, y1_ref, wt_ref, g_ref, o_ref):
    wt = wt_ref[...]
    y = wt[:, 0:1] * y0_ref[...] + wt[:, 1:2] * y1_ref[...]
    x2 = x_ref[...] + gt_ref[...] * y.reshape(x_ref.shape)
    o_ref[...] = _rms(x2, g_ref[...][None])


def _final(x1, gt, y_rows, wt, g, bb, lt, row_block0):
    bsz, seq, d = x1.shape
    rows = bb * lt
    nl = seq // lt
    choice_blocks = y_rows.shape[0] // TOP_K // rows
    xs = pl.BlockSpec((bb, lt, d), lambda i: (i // nl, i % nl, 0))
    return pl.pallas_call(
        _final_kernel,
        grid=(bsz * seq // rows,),
        in_specs=[xs, pl.BlockSpec((bb, 1, d), lambda i: (i // nl, 0, 0)),
                  pl.BlockSpec((rows, d), lambda i: (row_block0 + i, 0)),
                  pl.BlockSpec((rows, d), lambda i: (choice_blocks + row_block0 + i, 0)),
                  pl.BlockSpec((rows, LANES), lambda i: (i, 0)),
                  pl.BlockSpec((1, d), lambda i: (0, 0))],
        out_specs=xs,
        out_shape=jax.ShapeDtypeStruct(x1.shape, F32),
        compiler_params=_params(("arbitrary",)),
        name="moe_combine_final_norm",
    )(x1, gt, y_rows, y_rows, wt, g)


def _rope_tables(pos, rows):
    half = MLA_ROPE // 2
    inv_freq = ROPE_BASE ** (-jnp.arange(half, dtype=F32) / half)
    ang = pos.astype(F32)[:, None] * inv_freq[None, :]
    cos, sin = jnp.cos(ang), jnp.sin(ang)
    zero = jnp.zeros_like(cos)
    pad = jnp.zeros((pos.shape[0], LANES - MLA_ROPE), F32)
    tabs = (jnp.concatenate([cos, cos, pad], axis=1),
            jnp.concatenate([-sin, zero, pad], axis=1),
            jnp.concatenate([zero, sin, pad], axis=1))
    reps = rows // pos.shape[0]
    return tuple(jnp.tile(t, (reps, 1)) for t in tabs)


def _arrange_w_in(w_in):
    d = w_in.shape[0]
    off_gv = 2 * GLA_QK
    off_ga = off_gv + GLA_WIDTH
    off_gr = off_ga + GLA_GATE_RANK
    off_cq = off_gr + GLA_WIDTH
    off_ckv = off_cq + MLA_Q_RANK
    off_kr = off_ckv + MLA_KV_RANK
    pad = jnp.zeros((d, LANES - MLA_ROPE - GLA_GATE_RANK), w_in.dtype)
    return jnp.concatenate([w_in[:, :off_ga], w_in[:, off_gr:off_kr + MLA_ROPE], w_in[:, off_ga:off_gr], pad],
                           axis=1).astype(BF16)


def _arrange_w_uq(w_uq):
    r = w_uq.shape[0]
    pad = jnp.zeros((r, MLA_HEADS, QK_PAD - MLA_NOPE - MLA_ROPE), w_uq.dtype)
    return jnp.concatenate([w_uq, pad], axis=2).reshape(r, MLA_HEADS * QK_PAD).astype(BF16)


def _mixer(x, mods, pos, s0, cache, w, prompt):
    bsz, seq, d = x.shape
    t = bsz * seq
    sh1, sc1, gt1, sh2, sc2, _ = mods
    bb, lt = (1, 512) if prompt else (512 // seq, seq)
    proj = _inproj(x, w['g_mix'], sc1, sh1, w['w_in'], bb, lt)
    gbb, glt = (1, 512) if prompt else (8, seq)
    o_gla, s_fin = _gla(proj, s0, w['wg_pad'], w['bg'], w['g_gla'], bsz, seq, gbb, glt)
    tm = 512
    if prompt:
        tabs = _rope_tables(pos, seq)
        q_cat, k_cat, v, ckv_n, k_rope = _mla_proj(proj, tabs, w['g_q'], w['g_kv'], w['w_uq'], w['w_uk'],
                                                    w['w_uv'], tm, absorbed=False)
        o_mla = _prompt_attention(q_cat, k_cat, v, w['g_mla'], bsz, seq, 512, 512)
    else:
        cache_ckv, cache_krope, page_table = cache
        tabs = _rope_tables(pos, tm)
        q_lat, q_r, ckv_n, k_rope = _mla_proj(proj, tabs, w['g_q'], w['g_kv'], w['w_uq'], w['w_uk_t'],
                                              w['w_uv'], tm, absorbed=True)
        o_lat = _paged_attention(q_lat, q_r, ckv_n, k_rope, cache_ckv, jnp.swapaxes(cache_krope, 2, 3),
                                 page_table, seq, 32)
        o_mla = _latent_out(o_lat, w['w_uv_h'], w['g_mla'], tm)
    obb, olt = (1, 256) if prompt else (256 // seq, seq)
    routed = _outproj(o_gla, o_mla, w['w_out'], x, gt1, sc2, sh2, w['g_ffn'], w['w_r'], w['b_r'], obb, olt)
    return routed, ckv_n.reshape(bsz, seq, MLA_KV_RANK), k_rope.reshape(bsz, seq, MLA_ROPE), s_fin


def kernel(x_prompt, x_sample, c_prompt, c_sample, cache_ckv, cache_krope, state_gla, page_table, w_ada, b_ada, g_mix, g_ffn, w_in, gla_w_gate, gla_b_gate, gla_g_onorm, mla_g_qnorm, mla_w_uq, mla_g_kvnorm, mla_w_uk, mla_w_uv, mla_g_onorm, w_out, moe_w_group, moe_b_group, moe_w_expert, moe_b_expert, moe_w_gu, moe_w_down, g_final):
    depth = w_ada.shape[0]
    assert depth == 1
    bp, lp, d = x_prompt.shape
    bs, ls, _ = x_sample.shape
    n_past = page_table.shape[1] * PAGE_SIZE
    lyr = 0

    n_c = bp + bs
    c_rows = -(-n_c // 8) * 8
    c_all = jnp.concatenate([c_prompt, c_sample, jnp.zeros((c_rows - n_c, d), F32)], axis=0)
    mod = _ada(c_all, w_ada[lyr], b_ada[lyr][None])
    mods_p = tuple(m[:bp, None, :] for m in jnp.split(mod, 6, axis=-1))
    mods_s = tuple(m[bp:n_c, None, :] for m in jnp.split(mod, 6, axis=-1))

    wg_pad = jnp.zeros((LANES, GLA_QK), F32).at[GA_LANE:GA_LANE + GLA_GATE_RANK].set(gla_w_gate[lyr])
    w_uk = mla_w_uk[lyr]
    w_uv = mla_w_uv[lyr]
    w = {
        'g_mix': g_mix[lyr][None], 'g_ffn': g_ffn[lyr][None],
        'w_in': _arrange_w_in(w_in[lyr]),
        'wg_pad': wg_pad, 'bg': gla_b_gate[lyr][None], 'g_gla': gla_g_onorm[lyr][None],
        'g_q': mla_g_qnorm[lyr][None], 'g_kv': mla_g_kvnorm[lyr][None], 'g_mla': mla_g_onorm[lyr][None],
        'w_uq': _arrange_w_uq(mla_w_uq[lyr]),
        'w_uk': w_uk.reshape(MLA_KV_RANK, MLA_HEADS * MLA_NOPE).astype(BF16),
        'w_uk_t': w_uk.transpose(1, 2, 0).astype(BF16),
        'w_uv': w_uv.reshape(MLA_KV_RANK, MLA_WIDTH).astype(BF16),
        'w_uv_h': w_uv.transpose(1, 0, 2).astype(BF16),
        'w_out': w_out[lyr].astype(BF16),
    }
    w_r = jnp.concatenate([moe_w_group[lyr], moe_w_expert[lyr],
                           jnp.zeros((d, LANES - N_GROUPS - N_EXPERTS), F32)], axis=1)
    w['w_r'] = jnp.concatenate(_split_bf16(w_r, 2), axis=1)
    w['b_r'] = jnp.concatenate([moe_b_group[lyr], moe_b_expert[lyr],
                                jnp.zeros((LANES - N_GROUPS - N_EXPERTS,), F32)])[None]

    pos_p = jnp.arange(lp, dtype=jnp.int32)
    pos_s = n_past + jnp.arange(ls, dtype=jnp.int32)
    s0_p = jnp.zeros((bp, GLA_HEADS, GLA_DK, GLA_DV), F32)
    (x1_p, h2_p, ids_p, wts_p), ckv_p, kr_p, st_p = _mixer(x_prompt, mods_p, pos_p, s0_p, None, w, prompt=True)
    (x1_s, h2_s, ids_s, wts_s), ckv_s, kr_s, st_s = _mixer(x_sample, mods_s, pos_s, state_gla[lyr],
                                                          (cache_ckv, cache_krope, page_table), w, prompt=False)

    e_id = jnp.concatenate([ids_p[:, :TOP_K], ids_s[:, :TOP_K]], axis=0).reshape(-1)
    y_rows = _moe(_moe_plan(e_id, bp * lp), h2_p, h2_s, moe_w_gu[lyr], moe_w_down[lyr])

    y_p = _final(x1_p, mods_p[5], y_rows, wts_p, g_final[None], 1, 512, 0)
    y_s = _final(x1_s, mods_s[5], y_rows, wts_s, g_final[None], 512 // ls, ls, bp * lp // 512)
    return (y_p, y_s, ckv_p[None], kr_p[None], st_p[None], ckv_s[None], kr_s[None], st_s[None])
```

```python
import functools
import math

import jax
import jax.numpy as jnp
from jax import lax
from jax.experimental import pallas as pl
from jax.experimental.pallas import tpu as pltpu

F32 = jnp.float32
BF16 = jnp.bfloat16

D_MODEL = 2048
PAGE_SIZE = 128
GLA_HEADS = 4
GLA_DK = 128
GLA_DV = 256
GLA_GATE_RANK = 16
GLA_TAU = 16.0
GLA_CHUNK = 64
GLA_MIN_CHUNK = 16
GLA_QK = GLA_HEADS * GLA_DK
GLA_WIDTH = GLA_HEADS * GLA_DV
MLA_HEADS = 8
MLA_Q_RANK = 512
MLA_KV_RANK = 512
MLA_NOPE = 128
MLA_ROPE = 64
MLA_DV = 128
MLA_WIDTH = MLA_HEADS * MLA_DV
ROPE_BASE = 10000.0
ATTN_SCALE = (MLA_NOPE + MLA_ROPE) ** -0.5
N_GROUPS = 8
EXPERTS_PER_GROUP = 8
N_EXPERTS = N_GROUPS * EXPERTS_PER_GROUP
TOP_K = 2
EXPERT_FF = 1408
EPS = 1e-6

LANES = 128
QK_PAD = 2 * LANES
COL_GQ = 0
COL_GK = GLA_QK
COL_GV = 2 * GLA_QK
COL_GR = COL_GV + GLA_WIDTH
COL_CQ = COL_GR + GLA_WIDTH
COL_CKV = COL_CQ + MLA_Q_RANK
COL_KR = COL_CKV + MLA_KV_RANK
IN_COLS_PAD = COL_KR + LANES
GA_LANE = MLA_ROPE
IN_COL_TILE = IN_COLS_PAD // 3

MXU_COLS = 256
DMA_THREADS = 2
MOE_ROWS = 768
MOE_SUB = 256
MOE_HALF = MXU_COLS
MOE_COLS = 2 * MXU_COLS
MOE_GU_CHUNKS = -(-EXPERT_FF // MOE_HALF)
MOE_DN_CHUNKS = D_MODEL // MOE_COLS
MOE_CHUNKS = MOE_GU_CHUNKS + MOE_DN_CHUNKS
MOE_RING = 5
VMEM_LIMIT = 56 * 1024 * 1024


def _params(sem, vmem=VMEM_LIMIT):
    return pltpu.CompilerParams(dimension_semantics=sem, vmem_limit_bytes=vmem)


def _silu(x):
    return x * jax.nn.sigmoid(x)


def _rms(x, g):
    return x * lax.rsqrt(jnp.mean(x * x, axis=-1, keepdims=True) + EPS) * g


def _ada_kernel(c_ref, w_ref, b_ref, o_ref):
    a = _silu(c_ref[...]).astype(BF16)
    o_ref[...] = jnp.dot(a, w_ref[...].astype(BF16), preferred_element_type=F32) + b_ref[...]


def _ada(c, w, b):
    rows, d = c.shape
    n = w.shape[1]
    tn = 1024
    return pl.pallas_call(
        _ada_kernel,
        grid=(n // tn,),
        in_specs=[pl.BlockSpec((rows, d), lambda j: (0, 0)),
                  pl.BlockSpec((d, tn), lambda j: (0, j)),
                  pl.BlockSpec((1, tn), lambda j: (0, j))],
        out_specs=pl.BlockSpec((rows, tn), lambda j: (0, j)),
        out_shape=jax.ShapeDtypeStruct((rows, n), F32),
        compiler_params=_params(("arbitrary",)),
        name="ada_mod",
    )(c, w, b)


def _inproj_kernel(x_ref, g_ref, sc_ref, sh_ref, w_hbm, o_ref, w_scr, h_scr, sem):
    @pl.when(pl.program_id(0) == 0)
    def _():
        cp = pltpu.make_async_copy(w_hbm, w_scr, sem)
        cp.start()
        cp.wait()

    h = _rms(x_ref[...], g_ref[...][None]) * (1.0 + sc_ref[...]) + sh_ref[...]
    h_scr[...] = h.reshape(h_scr.shape).astype(BF16)
    for j in range(w_scr.shape[1] // IN_COL_TILE):
        cols = slice(j * IN_COL_TILE, (j + 1) * IN_COL_TILE)
        o_ref[:, cols] = jnp.dot(h_scr[...], w_scr[:, cols], preferred_element_type=F32)


def _inproj(x, g, sc, sh, w, bb, lt):
    bsz, seq, d = x.shape
    rows = bb * lt
    nl = seq // lt
    n = w.shape[1]
    return pl.pallas_call(
        _inproj_kernel,
        grid=(bsz * seq // rows,),
        in_specs=[pl.BlockSpec((bb, lt, d), lambda i: (i // nl, i % nl, 0)),
                  pl.BlockSpec((1, d), lambda i: (0, 0)),
                  pl.BlockSpec((bb, 1, d), lambda i: (i // nl, 0, 0)),
                  pl.BlockSpec((bb, 1, d), lambda i: (i // nl, 0, 0)),
                  pl.BlockSpec(memory_space=pl.ANY)],
        out_specs=pl.BlockSpec((rows, n), lambda i: (i, 0)),
        out_shape=jax.ShapeDtypeStruct((bsz * seq, n), F32),
        scratch_shapes=[pltpu.VMEM(w.shape, BF16), pltpu.VMEM((rows, d), BF16), pltpu.SemaphoreType.DMA(())],
        compiler_params=_params(("arbitrary",)),
        name="in_proj",
    )(x, g, sc, sh, w)


def _split_bf16(x, pieces):
    out = []
    for _ in range(pieces):
        part = x.astype(BF16)
        out.append(part)
        x = x - part.astype(F32)
    return out


def _gla_kernel(q_ref, k_ref, v_ref, gr_ref, ga_ref, wg_ref, bg_ref, gon_ref, tri_b_ref, tri_f_ref, s0_ref,
                o_ref, sout_ref, s_scr, *, n_chunks, n_valid, c):
    l_step = pl.program_id(2)

    @pl.when(l_step == 0)
    def _():
        s_scr[...] = s0_ref[...]

    chunks_per_batch = n_chunks // s_scr.shape[0]

    def load(ref):
        x = ref[...]
        if n_valid == c:
            return x
        zeros = jnp.zeros((c - n_valid, x.shape[1]), x.dtype)
        return jnp.concatenate([part for i in range(n_chunks)
                                for part in (x[i * n_valid:(i + 1) * n_valid], zeros)], axis=0)

    q = load(q_ref) * (GLA_DK ** -0.5)
    k = load(k_ref)
    v = load(v_ref).astype(BF16)
    a_hi, a_lo = _split_bf16(load(ga_ref), 2)
    w_hi, w_lo = _split_bf16(wg_ref[...], 2)
    z = jnp.dot(jnp.concatenate([a_hi, a_lo, a_hi], axis=1), jnp.concatenate([w_hi, w_hi, w_lo], axis=0),
                preferred_element_type=F32) + bg_ref[...]
    log_a = (jnp.minimum(z, 0.0) - jnp.log1p(jnp.exp(-jnp.abs(z)))) / GLA_TAU
    if n_valid < c:
        log_a = jnp.where(lax.broadcasted_iota(jnp.int32, log_a.shape, 0) % c < n_valid, log_a, 0.0)
    cum = jnp.dot(tri_b_ref[...], jnp.concatenate(_split_bf16(log_a, 3), axis=1), preferred_element_type=F32)
    b = cum[:, :GLA_DK] + cum[:, GLA_DK:2 * GLA_DK] + cum[:, 2 * GLA_DK:]
    q_dec = (q * jnp.exp(b)).astype(BF16)
    k_inv = (k * jnp.exp(-b)).astype(BF16)
    att = lax.dot_general(q_dec, k_inv, (((1,), (1,)), ((), ())), preferred_element_type=F32)
    att = jnp.where(tri_f_ref[...] > 0.0, att, 0.0).astype(BF16)
    o_intra = jnp.dot(att, v, preferred_element_type=F32)

    states = {}
    outs = []
    for idx in range(n_chunks):
        bi = idx // chunks_per_batch
        rows = slice(idx * c, (idx + 1) * c)
        s_old = states[bi] if bi in states else s_scr[bi]
        b_last = b[(idx + 1) * c - 1:(idx + 1) * c, :]
        o = o_intra[rows] + jnp.dot(q_dec[rows], s_old.astype(BF16), preferred_element_type=F32)
        k_dec = (k[rows] * jnp.exp(b_last - b[rows])).astype(BF16)
        kv = lax.dot_general(k_dec, v[rows], (((0,), (0,)), ((), ())), preferred_element_type=F32)
        decay = jnp.exp(jnp.broadcast_to(b_last, (GLA_DK, GLA_DK)).T)
        states[bi] = jnp.concatenate([decay] * (GLA_DV // GLA_DK), axis=1) * s_old + kv
        outs.append(o[:n_valid])
    for bi, s_new in states.items():
        s_scr[bi] = s_new
    o_all = _rms(jnp.concatenate(outs, axis=0), gon_ref[...])
    o_ref[...] = o_all * _silu(gr_ref[...])

    @pl.when(l_step == pl.num_programs(2) - 1)
    def _():
        sout_ref[...] = s_scr[...]


def _gla(proj, s0, wg_pad, bg, gon, bsz, seq, bb, lt):
    n_valid = math.gcd(seq, GLA_CHUNK)
    rows = bb * lt
    nl = seq // lt
    rmap = lambda b, h, l: b * nl + l
    n_chunks = rows // n_valid
    c = max(n_valid, GLA_MIN_CHUNK)
    kern = functools.partial(_gla_kernel, n_chunks=n_chunks, n_valid=n_valid, c=c)
    st_spec = pl.BlockSpec((bb, None, GLA_DK, GLA_DV), lambda b, h, l: (b, h, 0, 0))
    pos = jnp.arange(n_chunks * c)
    tri = (pos[:, None] // c == pos[None, :] // c) & (pos[None, :] <= pos[:, None])
    tri_spec = pl.BlockSpec(tri.shape, lambda b, h, l: (0, 0))
    return pl.pallas_call(
        kern,
        grid=(bsz // bb, GLA_HEADS, nl),
        in_specs=[pl.BlockSpec((rows, GLA_DK), lambda b, h, l: (rmap(b, h, l), COL_GQ // GLA_DK + h)),
                  pl.BlockSpec((rows, GLA_DK), lambda b, h, l: (rmap(b, h, l), COL_GK // GLA_DK + h)),
                  pl.BlockSpec((rows, GLA_DV), lambda b, h, l: (rmap(b, h, l), COL_GV // GLA_DV + h)),
                  pl.BlockSpec((rows, GLA_DV), lambda b, h, l: (rmap(b, h, l), COL_GR // GLA_DV + h)),
                  pl.BlockSpec((rows, LANES), lambda b, h, l: (rmap(b, h, l), COL_KR // LANES)),
                  pl.BlockSpec((LANES, GLA_DK), lambda b, h, l: (0, h)),
                  pl.BlockSpec((1, GLA_DK), lambda b, h, l: (0, h)),
                  pl.BlockSpec((1, GLA_DV), lambda b, h, l: (0, 0)),
                  tri_spec, tri_spec, st_spec],
        out_specs=[pl.BlockSpec((rows, GLA_DV), lambda b, h, l: (rmap(b, h, l), h)), st_spec],
        out_shape=[jax.ShapeDtypeStruct((bsz * seq, GLA_WIDTH), F32),
                   jax.ShapeDtypeStruct((bsz, GLA_HEADS, GLA_DK, GLA_DV), F32)],
        scratch_shapes=[pltpu.VMEM((bb, GLA_DK, GLA_DV), F32)],
        compiler_params=_params(("arbitrary", "arbitrary", "arbitrary")),
        name="gla",
    )(proj, proj, proj, proj, proj, wg_pad, bg, gon, tri.astype(BF16), tri.astype(F32), s0)


def _rope(r, cos, sin_lo, sin_hi):
    return (r * cos + pltpu.roll(r, LANES - MLA_ROPE // 2, axis=1) * sin_lo
            + pltpu.roll(r, MLA_ROPE // 2, axis=1) * sin_hi)


def _mla_proj_kernel(cq_ref, ckv_ref, kr_ref, cos_ref, slo_ref, shi_ref, gq_ref, gkv_ref, wq_ref,
                     wk_ref, wv_ref, *out_refs, absorbed):
    cos, slo, shi = cos_ref[...], slo_ref[...], shi_ref[...]
    cqn = _rms(cq_ref[...], gq_ref[...]).astype(BF16)
    qf = jnp.dot(cqn, wq_ref[...], preferred_element_type=F32)
    ckv_n = _rms(ckv_ref[...], gkv_ref[...])
    k_rot = _rope(kr_ref[...], cos, slo, shi)
    if absorbed:
        qlat_ref, qr_ref, ckvn_ref, krope_ref = out_refs
    else:
        qcat_ref, kcat_ref, v_ref, ckvn_ref, krope_ref = out_refs
    ckvn_ref[...] = ckv_n
    krope_ref[...] = k_rot[:, :MLA_ROPE]
    for h in range(MLA_HEADS):
        q_nope = qf[:, h * QK_PAD:h * QK_PAD + LANES] * ATTN_SCALE
        q_rot = _rope(qf[:, h * QK_PAD + LANES:(h + 1) * QK_PAD], cos, slo, shi) * ATTN_SCALE
        if absorbed:
            qlat_ref[h] = jnp.dot(q_nope.astype(BF16), wk_ref[h], preferred_element_type=F32)
            qr_ref[h] = q_rot
        else:
            qcat_ref[:, h * QK_PAD:h * QK_PAD + LANES] = q_nope.astype(BF16)
            qcat_ref[:, h * QK_PAD + LANES:(h + 1) * QK_PAD] = q_rot.astype(BF16)
    if not absorbed:
        ckv_b = ckv_n.astype(BF16)
        k_nope = jnp.dot(ckv_b, wk_ref[...], preferred_element_type=F32)
        k_rot_b = k_rot.astype(BF16)
        for h in range(MLA_HEADS):
            kcat_ref[:, h * QK_PAD:h * QK_PAD + LANES] = k_nope[:, h * LANES:(h + 1) * LANES].astype(BF16)
            kcat_ref[:, h * QK_PAD + LANES:(h + 1) * QK_PAD] = k_rot_b
        v_ref[...] = jnp.dot(ckv_b, wv_ref[...], preferred_element_type=F32).astype(BF16)


def _mla_proj(proj, tabs, gq, gkv, wq, wk, wv, tm, absorbed):
    t = proj.shape[0]
    ntab = tabs[0].shape[0] // tm
    row = lambda c: pl.BlockSpec((tm, c), lambda i: (i, 0))
    full = lambda a: pl.BlockSpec(a.shape, lambda i: (0,) * a.ndim)
    tab = pl.BlockSpec((tm, LANES), lambda i: (i % ntab, 0))
    in_specs = [pl.BlockSpec((tm, MLA_Q_RANK), lambda i: (i, COL_CQ // MLA_Q_RANK)),
                pl.BlockSpec((tm, MLA_KV_RANK), lambda i: (i, COL_CKV // MLA_KV_RANK)),
                pl.BlockSpec((tm, LANES), lambda i: (i, COL_KR // LANES)),
                tab, tab, tab, full(gq), full(gkv), full(wq), full(wk), full(wv)]
    if absorbed:
        out_specs = [pl.BlockSpec((MLA_HEADS, tm, MLA_KV_RANK), lambda i: (0, i, 0)),
                     pl.BlockSpec((MLA_HEADS, tm, LANES), lambda i: (0, i, 0))]
        out_shape = [jax.ShapeDtypeStruct((MLA_HEADS, t, MLA_KV_RANK), F32),
                     jax.ShapeDtypeStruct((MLA_HEADS, t, LANES), F32)]
    else:
        out_specs = [row(MLA_HEADS * QK_PAD), row(MLA_HEADS * QK_PAD), row(MLA_WIDTH)]
        out_shape = [jax.ShapeDtypeStruct((t, MLA_HEADS * QK_PAD), BF16),
                     jax.ShapeDtypeStruct((t, MLA_HEADS * QK_PAD), BF16),
                     jax.ShapeDtypeStruct((t, MLA_WIDTH), BF16)]
    out_specs += [row(MLA_KV_RANK), row(MLA_ROPE)]
    out_shape += [jax.ShapeDtypeStruct((t, MLA_KV_RANK), F32), jax.ShapeDtypeStruct((t, MLA_ROPE), F32)]
    return pl.pallas_call(
        functools.partial(_mla_proj_kernel, absorbed=absorbed),
        grid=(t // tm,),
        in_specs=in_specs, out_specs=out_specs, out_shape=out_shape,
        compiler_params=_params(("arbitrary",)),
        name="mla_proj_absorbed" if absorbed else "mla_proj",
    )(proj, proj, proj, *tabs, gq, gkv, wq, wk, wv)


ATTN_HEADS_PER_STEP = 4


def _attn_kernel(*refs, tq, tk, n_heads):
    assert tq == tk
    heads = [refs[3 * i:3 * i + 3] for i in range(n_heads)]
    g_ref, o_ref = refs[3 * n_heads], refs[3 * n_heads + 1]
    qi = pl.program_id(2)
    qs = [q_ref[...] for q_ref, _, _ in heads]

    def block(j, carries, on_diagonal):
        k0 = pl.multiple_of(j * tk, tk)
        out = []
        for (_, k_ref, v_ref), q, (m, l, acc) in zip(heads, qs, carries):
            s = lax.dot_general(q, k_ref[pl.ds(k0, tk), :], (((1,), (1,)), ((), ())),
                                preferred_element_type=F32)
            if on_diagonal:
                row = lax.broadcasted_iota(jnp.int32, (tq, tk), 0)
                col = lax.broadcasted_iota(jnp.int32, (tq, tk), 1)
                s = jnp.where(col <= row, s, -jnp.inf)
            m_new = jnp.maximum(m, jnp.max(s, axis=-1, keepdims=True))
            alpha = jnp.exp(m - m_new)
            p = jnp.exp(s - m_new)
            l = alpha * l + jnp.sum(p, axis=-1, keepdims=True)
            acc = alpha * acc + jnp.dot(p.astype(BF16), v_ref[pl.ds(k0, tk), :], preferred_element_type=F32)
            out.append((m_new, l, acc))
        return tuple(out)

    init = tuple((jnp.full((tq, 1), -jnp.inf, F32), jnp.zeros((tq, 1), F32), jnp.zeros((tq, MLA_DV), F32))
                 for _ in range(n_heads))
    carries = lax.fori_loop(0, qi, lambda j, c: block(j, c, False), init)
    for t, (_, l, acc) in enumerate(block(qi, carries, True)):
        o_ref[:, t * MLA_DV:(t + 1) * MLA_DV] = _rms(acc / l, g_ref[...])


def _prompt_attention(q_cat, k_cat, v, g, bsz, seq, tq, tk):
    nq = seq // tq
    n = ATTN_HEADS_PER_STEP
    head_specs = []
    for t in range(n):
        head_specs += [pl.BlockSpec((tq, QK_PAD), lambda b, h, i, t=t: (b * nq + i, h * n + t)),
                       pl.BlockSpec((seq, QK_PAD), lambda b, h, i, t=t: (b, h * n + t)),
                       pl.BlockSpec((seq, MLA_DV), lambda b, h, i, t=t: (b, h * n + t))]
    return pl.pallas_call(
        functools.partial(_attn_kernel, tq=tq, tk=tk, n_heads=n),
        grid=(bsz, MLA_HEADS // n, nq),
        in_specs=head_specs + [pl.BlockSpec((1, MLA_DV), lambda b, h, i: (0, 0))],
        out_specs=pl.BlockSpec((tq, n * MLA_DV), lambda b, h, i: (b * nq + i, h)),
        out_shape=jax.ShapeDtypeStruct((bsz * seq, MLA_WIDTH), F32),
        compiler_params=_params(("arbitrary", "arbitrary", "arbitrary")),
        name="prompt_attention",
    )(*([q_cat, k_cat, v] * n), g)


def _paged_attn_kernel(pt_ref, qlat_ref, qr_ref, cnew_ref, knew_ref, ckv_hbm, krt_hbm, o_ref,
                       ckv_buf, krt_buf, sems, m_scr, l_scr, acc_scr, *, pp, n_q):
    grp = pl.program_id(1)
    n_grp = pl.num_programs(1)
    step = pl.program_id(0) * n_grp + grp
    n_steps = pl.num_programs(0) * n_grp
    slot = step % 2
    rows = MLA_HEADS * n_q

    def page_copies(at_step, at_slot):
        out = []
        for j in range(pp):
            page = pt_ref[at_step * pp + j]
            out.append(pltpu.make_async_copy(ckv_hbm.at[0, page], ckv_buf.at[at_slot, j], sems.at[0, at_slot]))
            out.append(pltpu.make_async_copy(krt_hbm.at[0, page],
                                             krt_buf.at[at_slot, :, pl.ds(j * PAGE_SIZE, PAGE_SIZE)],
                                             sems.at[1, at_slot]))
        return out

    def start_pages(at_step, at_slot):
        for i, cp in enumerate(page_copies(at_step, at_slot)):
            cp.start(priority=(i // 2 + i) % DMA_THREADS)

    @pl.when(step == 0)
    def _():
        start_pages(step, slot)

    @pl.when(step + 1 < n_steps)
    def _():
        start_pages(step + 1, 1 - slot)

    @pl.when(grp == 0)
    def _():
        m_scr[...] = jnp.full(m_scr.shape, -jnp.inf, F32)
        l_scr[...] = jnp.zeros(l_scr.shape, F32)
        acc_scr[...] = jnp.zeros(acc_scr.shape, F32)

    q = qlat_ref[...].reshape(rows, MLA_KV_RANK).astype(BF16)
    qr = qr_ref[...].reshape(rows, LANES)[:, :MLA_ROPE].astype(BF16)

    def update(s, v):
        m = m_scr[...]
        m_new = jnp.maximum(m, jnp.max(s, axis=-1, keepdims=True))
        alpha = jnp.exp(m - m_new)
        p = jnp.exp(s - m_new)
        l_scr[...] = alpha * l_scr[...] + jnp.sum(p, axis=-1, keepdims=True)
        acc_scr[...] = acc_scr[...] * alpha + jnp.dot(p.astype(BF16), v, preferred_element_type=F32)
        m_scr[...] = m_new

    for cp in page_copies(step, slot):
        cp.wait()
    ckv = ckv_buf[slot].reshape(pp * PAGE_SIZE, MLA_KV_RANK).astype(BF16)
    s_past = (lax.dot_general(q, ckv, (((1,), (1,)), ((), ())), preferred_element_type=F32)
              + jnp.dot(qr, krt_buf[slot].astype(BF16), preferred_element_type=F32))
    update(s_past, ckv)

    @pl.when(grp == n_grp - 1)
    def _():
        pad = PAGE_SIZE - n_q
        c_new = jnp.concatenate([cnew_ref[...], jnp.zeros((pad, MLA_KV_RANK), F32)], axis=0).astype(BF16)
        k_new = jnp.concatenate([knew_ref[...], jnp.zeros((pad, MLA_ROPE), F32)], axis=0).astype(BF16)
        s = (lax.dot_general(q, c_new, (((1,), (1,)), ((), ())), preferred_element_type=F32)
             + lax.dot_general(qr, k_new, (((1,), (1,)), ((), ())), preferred_element_type=F32))
        q_idx = lax.broadcasted_iota(jnp.int32, s.shape, 0) % n_q
        k_idx = lax.broadcasted_iota(jnp.int32, s.shape, 1)
        update(jnp.where(k_idx <= q_idx, s, -jnp.inf), c_new)
        o_ref[...] = (acc_scr[...] / l_scr[...]).reshape(o_ref.shape)


def _paged_attention(q_lat, q_r, ckv_n, k_rope, cache_ckv, cache_krope_t, page_table, n_q, pp):
    bsz, n_pages = page_table.shape
    rows = MLA_HEADS * n_q
    grid_spec = pltpu.PrefetchScalarGridSpec(
        num_scalar_prefetch=1,
        grid=(bsz, n_pages // pp),
        in_specs=[pl.BlockSpec((MLA_HEADS, n_q, MLA_KV_RANK), lambda b, s, pt: (0, b, 0)),
                  pl.BlockSpec((MLA_HEADS, n_q, LANES), lambda b, s, pt: (0, b, 0)),
                  pl.BlockSpec((n_q, MLA_KV_RANK), lambda b, s, pt: (b, 0)),
                  pl.BlockSpec((n_q, MLA_ROPE), lambda b, s, pt: (b, 0)),
                  pl.BlockSpec(memory_space=pl.ANY),
                  pl.BlockSpec(memory_space=pl.ANY)],
        out_specs=pl.BlockSpec((MLA_HEADS, n_q, MLA_KV_RANK), lambda b, s, pt: (0, b, 0)),
        scratch_shapes=[pltpu.VMEM((2, pp, PAGE_SIZE, MLA_KV_RANK), F32),
                        pltpu.VMEM((2, MLA_ROPE, pp * PAGE_SIZE), F32),
                        pltpu.SemaphoreType.DMA((2, 2)),
                        pltpu.VMEM((rows, 1), F32), pltpu.VMEM((rows, 1), F32),
                        pltpu.VMEM((rows, MLA_KV_RANK), F32)],
    )
    return pl.pallas_call(
        functools.partial(_paged_attn_kernel, pp=pp, n_q=n_q),
        grid_spec=grid_spec,
        out_shape=jax.ShapeDtypeStruct((MLA_HEADS, bsz * n_q, MLA_KV_RANK), F32),
        compiler_params=_params(("arbitrary", "arbitrary")),
        name="paged_attention",
    )(page_table.reshape(-1), q_lat, q_r, ckv_n, k_rope, cache_ckv, cache_krope_t)


def _latent_out_kernel(o_ref, w_ref, g_ref, out_ref):
    o = jnp.dot(o_ref[...].astype(BF16), w_ref[...], preferred_element_type=F32)
    out_ref[...] = _rms(o, g_ref[...])


def _latent_out(o_lat, w_uv_h, g, tm):
    t = o_lat.shape[1]
    return pl.pallas_call(
        _latent_out_kernel,
        grid=(MLA_HEADS, t // tm),
        in_specs=[pl.BlockSpec((None, tm, MLA_KV_RANK), lambda h, i: (h, i, 0)),
                  pl.BlockSpec((None, MLA_KV_RANK, MLA_DV), lambda h, i: (h, 0, 0)),
                  pl.BlockSpec((1, MLA_DV), lambda h, i: (0, 0))],
        out_specs=pl.BlockSpec((tm, MLA_DV), lambda h, i: (i, h)),
        out_shape=jax.ShapeDtypeStruct((t, MLA_WIDTH), F32),
        compiler_params=_params(("arbitrary", "arbitrary")),
        name="latent_out",
    )(o_lat, w_uv_h, g)


def _route(logit):
    lane = lax.broadcasted_iota(jnp.int32, logit.shape, 1)
    neg = -jnp.inf
    big = jnp.int32(LANES)

    def first_max(vals):
        top = jnp.max(vals, axis=-1, keepdims=True)
        return top, jnp.min(jnp.where(vals == top, lane, big), axis=-1, keepdims=True)

    g_vals = jnp.where(lane < N_GROUPS, logit, neg)
    g_max, g_idx = first_max(g_vals)
    g_w = 1.0 / jnp.sum(jnp.exp(g_vals - g_max), axis=-1, keepdims=True)
    e_lane = lane - N_GROUPS
    in_group = (e_lane >= g_idx * EXPERTS_PER_GROUP) & (e_lane < (g_idx + 1) * EXPERTS_PER_GROUP)
    e_vals = jnp.where(in_group, logit, neg)
    v1, i1 = first_max(e_vals)
    v2, i2 = first_max(jnp.where(lane == i1, neg, e_vals))
    e2 = jnp.exp(v2 - v1)
    w1 = g_w / (1.0 + e2)
    w2 = g_w * e2 / (1.0 + e2)
    ids = jnp.where(lane == 0, i1 - N_GROUPS, jnp.where(lane == 1, i2 - N_GROUPS, 0))
    wts = jnp.where(lane == 0, w1, jnp.where(lane == 1, w2, 0.0))
    return ids, wts


def _outproj_kernel(og_ref, om_ref, wg_ref, wm_ref, x_ref, gt_ref, sc_ref, sh_ref, g_ref, wr_ref, br_ref,
                    x1_ref, h2_ref, id_ref, wt_ref):
    y = (jnp.dot(og_ref[...].astype(BF16), wg_ref[...], preferred_element_type=F32)
         + jnp.dot(om_ref[...].astype(BF16), wm_ref[...], preferred_element_type=F32))
    x1 = x_ref[...] + gt_ref[...] * y.reshape(x_ref.shape)
    x1_ref[...] = x1
    h2 = (_rms(x1, g_ref[...][None]) * (1.0 + sc_ref[...]) + sh_ref[...]).reshape(h2_ref.shape)
    h2_ref[...] = h2
    x_hi, x_lo = _split_bf16(h2, 2)
    lg_hi = jnp.dot(x_hi, wr_ref[...], preferred_element_type=F32)
    lg_lo = jnp.dot(x_lo, wr_ref[...], preferred_element_type=F32)
    logit = (lg_hi[:, :LANES] + lg_hi[:, LANES:] + lg_lo[:, :LANES] + lg_lo[:, LANES:]) + br_ref[...]
    id_ref[...], wt_ref[...] = _route(logit)


def _outproj(o_gla, o_mla, w_out, x, gt, sc, sh, g, w_r, b_r, bb, lt):
    bsz, seq, d = x.shape
    rows = bb * lt
    nl = seq // lt
    xs = pl.BlockSpec((bb, lt, d), lambda i: (i // nl, i % nl, 0))
    ms = pl.BlockSpec((bb, 1, d), lambda i: (i // nl, 0, 0))
    full = lambda a: pl.BlockSpec(a.shape, lambda i: (0,) * a.ndim)
    wide = pl.BlockSpec((rows, LANES), lambda i: (i, 0))
    return pl.pallas_call(
        _outproj_kernel,
        grid=(bsz * seq // rows,),
        in_specs=[pl.BlockSpec((rows, GLA_WIDTH), lambda i: (i, 0)),
                  pl.BlockSpec((rows, MLA_WIDTH), lambda i: (i, 0)),
                  pl.BlockSpec((GLA_WIDTH, d), lambda i: (0, 0)),
                  pl.BlockSpec((MLA_WIDTH, d), lambda i: (GLA_WIDTH // MLA_WIDTH, 0)),
                  xs, ms, ms, ms, full(g), full(w_r), full(b_r)],
        out_specs=[xs, pl.BlockSpec((rows, d), lambda i: (i, 0)), wide, wide],
        out_shape=[jax.ShapeDtypeStruct(x.shape, F32), jax.ShapeDtypeStruct((bsz * seq, d), F32),
                   jax.ShapeDtypeStruct((bsz * seq, LANES), jnp.int32),
                   jax.ShapeDtypeStruct((bsz * seq, LANES), F32)],
        compiler_params=_params(("arbitrary",)),
        name="out_proj_route",
    )(o_gla, o_mla, w_out, w_out, x, gt, sc, sh, g, w_r, b_r)


def _moe_kernel(ie_ref, ir_ref, in_ref, im_ref, nl_ref, src_ref, dst_ref,
                hp_hbm, hs_hbm, wgu_hbm, wd_hbm, y_hbm,
                x_buf, xb_scr, wb_scr, act_scr, y_buf, ring, gsem, ssem, wsem):
    n_live = nl_ref[0]
    last_item = ie_ref.shape[0] - 1

    def chunk_width(j):
        return min(MOE_HALF, EXPERT_FF - j * MOE_HALF)

    def weight_copies(e, j, slot):
        sem = wsem.at[slot]
        if j < MOE_GU_CHUNKS:
            w = chunk_width(j)
            return [pltpu.make_async_copy(wgu_hbm.at[e, :, pl.ds(off + j * MOE_HALF, w)],
                                          ring.at[slot, :, pl.ds(half * w, w)], sem)
                    for half, off in enumerate((0, EXPERT_FF))]
        c = j - MOE_GU_CHUNKS
        return [pltpu.make_async_copy(wd_hbm.at[e, :, pl.ds(c * MOE_COLS, MOE_COLS)],
                                      ring.at[slot, pl.ds(0, EXPERT_FF)], sem)]

    def start_weights(e, j, slot):
        for idx, cp in enumerate(weight_copies(e, j, slot)):
            cp.start(priority=(j + idx) % DMA_THREADS)

    def gather_copy(h_hbm, src_row, p, r, k=1):
        return pltpu.make_async_copy(h_hbm.at[pl.ds(src_row, k)], x_buf.at[p, pl.ds(r, k)], gsem.at[p])

    def scatter_copy(r, dst_row, k=1):
        return pltpu.make_async_copy(y_buf.at[pl.ds(r, k)], y_hbm.at[pl.ds(dst_row, k)], ssem)

    def for_rows(lo, hi, fn, unroll=4):
        groups = (hi - lo) // unroll

        def group(g, c):
            for u in range(unroll):
                fn(lo + g * unroll + u)
            return c

        def single(r, c):
            fn(r)
            return c

        lax.fori_loop(0, groups, group, 0)
        lax.fori_loop(lo + groups * unroll, hi, single, 0)

    def wait_rows(n, copy_of_rows):
        k = 1
        while k <= MOE_ROWS:
            @pl.when((n & k) != 0)
            def _(k=k):
                copy_of_rows(k).wait()
            k *= 2

    def start_gather(i, p):
        row0, n, m = ir_ref[i], in_ref[i], im_ref[i]
        for_rows(0, m, lambda r: gather_copy(hp_hbm, src_ref[row0 + r], p, r).start())
        for_rows(m, n, lambda r: gather_copy(hs_hbm, src_ref[row0 + r], p, r).start())

    x_buf[...] = jnp.zeros(x_buf.shape, F32)
    start_gather(0, 0)
    for j in range(MOE_RING):
        start_weights(ie_ref[0], j, j)

    def item_body(i, carry):
        p = i % 2
        e, row0, n = ie_ref[i], ir_ref[i], in_ref[i]
        n_sub = (n + MOE_SUB - 1) // MOE_SUB
        has_next = i + 1 < n_live
        e_next = ie_ref[jnp.minimum(i + 1, last_item)]
        base = (i * MOE_CHUNKS) % MOE_RING

        wait_rows(n, lambda k: gather_copy(hp_hbm, 0, p, 0, k))

        @pl.when(has_next)
        def _():
            start_gather(i + 1, 1 - p)

        def for_row_blocks(fn):
            pairs = n_sub // 2

            def pair(b, c):
                fn(pl.multiple_of(b * 2 * MOE_SUB, 2 * MOE_SUB), 2 * MOE_SUB)
                return c
            lax.fori_loop(0, pairs, pair, 0)

            @pl.when(n_sub % 2 == 1)
            def _():
                fn(pl.multiple_of(pairs * 2 * MOE_SUB, 2 * MOE_SUB), MOE_SUB)

        def cast_rows(sb, c):
            r0 = pl.multiple_of(sb * MOE_SUB, MOE_SUB)
            xb_scr[pl.ds(r0, MOE_SUB), :] = x_buf[p, pl.ds(r0, MOE_SUB), :].astype(BF16)
            return c
        lax.fori_loop(0, n_sub, cast_rows, 0)

        for j in range(MOE_CHUNKS):
            slot = (base + j) % MOE_RING
            if j == MOE_GU_CHUNKS:
                @pl.when(i > 0)
                def _():
                    n_prev = in_ref[jnp.maximum(i - 1, 0)]
                    wait_rows(n_prev, lambda k: scatter_copy(0, 0, k))
            for cp in weight_copies(e, j, slot):
                cp.wait()
            if j < MOE_GU_CHUNKS:
                cols = 2 * chunk_width(j)
                wb_scr[:, :cols] = ring[slot, :, pl.ds(0, cols)].astype(BF16)
            else:
                wb_scr[:EXPERT_FF, :] = ring[slot, pl.ds(0, EXPERT_FF), :].astype(BF16)
            j_ahead = j + MOE_RING
            if j_ahead < MOE_CHUNKS:
                start_weights(e, j_ahead, slot)
            else:
                @pl.when(has_next)
                def _(j_ahead=j_ahead, slot=slot):
                    start_weights(e_next, j_ahead - MOE_CHUNKS, slot)

            if j < MOE_GU_CHUNKS:
                w = chunk_width(j)

                def gate_up(r0, rows, j=j, w=w):
                    res = jnp.dot(xb_scr[pl.ds(r0, rows), :], wb_scr[:, :2 * w], preferred_element_type=F32)
                    act = _silu(res[:, :w]) * res[:, w:]
                    act_scr[pl.ds(r0, rows), j * MOE_HALF:j * MOE_HALF + w] = act.astype(BF16)
                for_row_blocks(gate_up)
            else:
                col0 = (j - MOE_GU_CHUNKS) * MOE_COLS

                def down(r0, rows, col0=col0):
                    y_buf[pl.ds(r0, rows), col0:col0 + MOE_COLS] = jnp.dot(
                        act_scr[pl.ds(r0, rows), :], wb_scr[:EXPERT_FF, :], preferred_element_type=F32)
                for_row_blocks(down)

        for_rows(0, n, lambda r: scatter_copy(r, dst_ref[row0 + r]).start())
        return carry

    lax.fori_loop(0, n_live, item_body, 0)
    wait_rows(in_ref[jnp.maximum(n_live - 1, 0)], lambda k: scatter_copy(0, 0, k))


def _moe(plan, h2_p, h2_s, w_gu, w_down):
    d = h2_p.shape[1]
    t_all = h2_p.shape[0] + h2_s.shape[0]
    hbm = pl.BlockSpec(memory_space=pl.ANY)
    grid_spec = pltpu.PrefetchScalarGridSpec(
        num_scalar_prefetch=len(plan),
        grid=(1,),
        in_specs=[hbm, hbm, hbm, hbm],
        out_specs=hbm,
        scratch_shapes=[pltpu.VMEM((2, MOE_ROWS, d), F32),
                        pltpu.VMEM((MOE_ROWS, d), BF16),
                        pltpu.VMEM((d, MOE_COLS), BF16),
                        pltpu.VMEM((MOE_ROWS, EXPERT_FF), BF16),
                        pltpu.VMEM((MOE_ROWS, d), F32),
                        pltpu.VMEM((MOE_RING, d, MOE_COLS), F32),
                        pltpu.SemaphoreType.DMA((2,)),
                        pltpu.SemaphoreType.DMA(()),
                        pltpu.SemaphoreType.DMA((MOE_RING,))],
    )
    return pl.pallas_call(
        _moe_kernel,
        grid_spec=grid_spec,
        out_shape=jax.ShapeDtypeStruct((t_all * TOP_K, d), F32),
        compiler_params=_params(("arbitrary",)),
        name="moe_experts",
    )(*plan, h2_p, h2_s, w_gu, w_down)


def _moe_plan(e_id, t_prompt):
    n_as = e_id.shape[0]
    n_items = N_EXPERTS + n_as // MOE_ROWS
    order = jnp.argsort(e_id).astype(jnp.int32)
    cnt = jnp.zeros((N_EXPERTS,), jnp.int32).at[e_id].add(1)
    cnt_p = jnp.zeros((N_EXPERTS,), jnp.int32).at[e_id[:t_prompt * TOP_K]].add(1)
    ustart = jnp.cumsum(cnt) - cnt
    per_e = (cnt + MOE_ROWS - 1) // MOE_ROWS
    item_end = jnp.cumsum(per_e)
    total = item_end[-1]
    idx = jnp.arange(n_items, dtype=jnp.int32)
    live = idx < total
    e_of = jnp.sum(item_end[None, :] <= jnp.minimum(idx, total - 1)[:, None], axis=1).astype(jnp.int32)
    local = jnp.minimum(idx, total - 1) - (item_end[e_of] - per_e[e_of])
    row0 = ustart[e_of] + local * MOE_ROWS
    n_rows = jnp.where(live, jnp.minimum(MOE_ROWS, cnt[e_of] - local * MOE_ROWS), 0)
    m_rows = jnp.clip(cnt_p[e_of] - local * MOE_ROWS, 0, n_rows)
    tok_s = order // TOP_K
    src_s = jnp.where(tok_s < t_prompt, tok_s, tok_s - t_prompt)
    dst_s = (order % TOP_K) * (n_as // TOP_K) + tok_s
    i32 = lambda a: a.astype(jnp.int32)
    return (i32(e_of), i32(row0), i32(n_rows), i32(m_rows), i32(total)[None], i32(src_s), i32(dst_s))


def _final_kernel(x_ref, gt_ref, y0_ref, y1_ref, wt_ref, g_ref, o_ref):
    wt = wt_ref[...]
    y = wt[:, 0:1] * y0_ref[...] + wt[:, 1:2] * y1_ref[...]
    x2 = x_ref[...] + gt_ref[...] * y.reshape(x_ref.shape)
    o_ref[...] = _rms(x2, g_ref[...][None])


def _final(x1, gt, y_rows, wt, g, bb, lt, row_block0):
    bsz, seq, d = x1.shape
    rows = bb * lt
    nl = seq // lt
    choice_blocks = y_rows.shape[0] // TOP_K // rows
    xs = pl.BlockSpec((bb, lt, d), lambda i: (i // nl, i % nl, 0))
    return pl.pallas_call(
        _final_kernel,
        grid=(bsz * seq // rows,),
        in_specs=[xs, pl.BlockSpec((bb, 1, d), lambda i: (i // nl, 0, 0)),
                  pl.BlockSpec((rows, d), lambda i: (row_block0 + i, 0)),
                  pl.BlockSpec((rows, d), lambda i: (choice_blocks + row_block0 + i, 0)),
                  pl.BlockSpec((rows, LANES), lambda i: (i, 0)),
                  pl.BlockSpec((1, d), lambda i: (0, 0))],
        out_specs=xs,
        out_shape=jax.ShapeDtypeStruct(x1.shape, F32),
        compiler_params=_params(("arbitrary",)),
        name="moe_combine_final_norm",
    )(x1, gt, y_rows, y_rows, wt, g)


def _rope_tables(pos, rows):
    half = MLA_ROPE // 2
    inv_freq = ROPE_BASE ** (-jnp.arange(half, dtype=F32) / half)
    ang = pos.astype(F32)[:, None] * inv_freq[None, :]
    cos, sin = jnp.cos(ang), jnp.sin(ang)
    zero = jnp.zeros_like(cos)
    pad = jnp.zeros((pos.shape[0], LANES - MLA_ROPE), F32)
    tabs = (jnp.concatenate([cos, cos, pad], axis=1),
            jnp.concatenate([-sin, zero, pad], axis=1),
            jnp.concatenate([zero, sin, pad], axis=1))
    reps = rows // pos.shape[0]
    return tuple(jnp.tile(t, (reps, 1)) for t in tabs)


def _arrange_w_in(w_in):
    d = w_in.shape[0]
    off_gv = 2 * GLA_QK
    off_ga = off_gv + GLA_WIDTH
    off_gr = off_ga + GLA_GATE_RANK
    off_cq = off_gr + GLA_WIDTH
    off_ckv = off_cq + MLA_Q_RANK
    off_kr = off_ckv + MLA_KV_RANK
    pad = jnp.zeros((d, LANES - MLA_ROPE - GLA_GATE_RANK), w_in.dtype)
    return jnp.concatenate([w_in[:, :off_ga], w_in[:, off_gr:off_kr + MLA_ROPE], w_in[:, off_ga:off_gr], pad],
                           axis=1).astype(BF16)


def _arrange_w_uq(w_uq):
    r = w_uq.shape[0]
    pad = jnp.zeros((r, MLA_HEADS, QK_PAD - MLA_NOPE - MLA_ROPE), w_uq.dtype)
    return jnp.concatenate([w_uq, pad], axis=2).reshape(r, MLA_HEADS * QK_PAD).astype(BF16)


def _mixer(x, mods, pos, s0, cache, w, prompt):
    bsz, seq, d = x.shape
    t = bsz * seq
    sh1, sc1, gt1, sh2, sc2, _ = mods
    bb, lt = (1, 512) if prompt else (512 // seq, seq)
    proj = _inproj(x, w['g_mix'], sc1, sh1, w['w_in'], bb, lt)
    gbb, glt = (1, 512) if prompt else (512 // GLA_MIN_CHUNK, seq)
    o_gla, s_fin = _gla(proj, s0, w['wg_pad'], w['bg'], w['g_gla'], bsz, seq, gbb, glt)
    tm = 512
    if prompt:
        tabs = _rope_tables(pos, seq)
        q_cat, k_cat, v, ckv_n, k_rope = _mla_proj(proj, tabs, w['g_q'], w['g_kv'], w['w_uq'], w['w_uk'],
                                                    w['w_uv'], tm, absorbed=False)
        o_mla = _prompt_attention(q_cat, k_cat, v, w['g_mla'], bsz, seq, 512, 512)
    else:
        cache_ckv, cache_krope, page_table = cache
        tabs = _rope_tables(pos, tm)
        q_lat, q_r, ckv_n, k_rope = _mla_proj(proj, tabs, w['g_q'], w['g_kv'], w['w_uq'], w['w_uk_t'],
                                              w['w_uv'], tm, absorbed=True)
        o_lat = _paged_attention(q_lat, q_r, ckv_n, k_rope, cache_ckv, jnp.swapaxes(cache_krope, 2, 3),
                                 page_table, seq, 32)
        o_mla = _latent_out(o_lat, w['w_uv_h'], w['g_mla'], tm)
    obb, olt = (1, 256) if prompt else (256 // seq, seq)
    routed = _outproj(o_gla, o_mla, w['w_out'], x, gt1, sc2, sh2, w['g_ffn'], w['w_r'], w['b_r'], obb, olt)
    return routed, ckv_n.reshape(bsz, seq, MLA_KV_RANK), k_rope.reshape(bsz, seq, MLA_ROPE), s_fin


def kernel(x_prompt, x_sample, c_prompt, c_sample, cache_ckv, cache_krope, state_gla, page_table, w_ada, b_ada, g_mix, g_ffn, w_in, gla_w_gate, gla_b_gate, gla_g_onorm, mla_g_qnorm, mla_w_uq, mla_g_kvnorm, mla_w_uk, mla_w_uv, mla_g_onorm, w_out, moe_w_group, moe_b_group, moe_w_expert, moe_b_expert, moe_w_gu, moe_w_down, g_final):
    depth = w_ada.shape[0]
    assert depth == 1
    bp, lp, d = x_prompt.shape
    bs, ls, _ = x_sample.shape
    n_past = page_table.shape[1] * PAGE_SIZE
    lyr = 0

    n_c = bp + bs
    c_rows = -(-n_c // 8) * 8
    c_all = jnp.concatenate([c_prompt, c_sample, jnp.zeros((c_rows - n_c, d), F32)], axis=0)
    mod = _ada(c_all, w_ada[lyr], b_ada[lyr][None])
    mods_p = tuple(m[:bp, None, :] for m in jnp.split(mod, 6, axis=-1))
    mods_s = tuple(m[bp:n_c, None, :] for m in jnp.split(mod, 6, axis=-1))

    wg_pad = jnp.zeros((LANES, GLA_QK), F32).at[GA_LANE:GA_LANE + GLA_GATE_RANK].set(gla_w_gate[lyr])
    w_uk = mla_w_uk[lyr]
    w_uv = mla_w_uv[lyr]
    w = {
        'g_mix': g_mix[lyr][None], 'g_ffn': g_ffn[lyr][None],
        'w_in': _arrange_w_in(w_in[lyr]),
        'wg_pad': wg_pad, 'bg': gla_b_gate[lyr][None], 'g_gla': gla_g_onorm[lyr][None],
        'g_q': mla_g_qnorm[lyr][None], 'g_kv': mla_g_kvnorm[lyr][None], 'g_mla': mla_g_onorm[lyr][None],
        'w_uq': _arrange_w_uq(mla_w_uq[lyr]),
        'w_uk': w_uk.reshape(MLA_KV_RANK, MLA_HEADS * MLA_NOPE).astype(BF16),
        'w_uk_t': w_uk.transpose(1, 2, 0).astype(BF16),
        'w_uv': w_uv.reshape(MLA_KV_RANK, MLA_WIDTH).astype(BF16),
        'w_uv_h': w_uv.transpose(1, 0, 2).astype(BF16),
        'w_out': w_out[lyr].astype(BF16),
    }
    w_r = jnp.concatenate([moe_w_group[lyr], moe_w_expert[lyr],
                           jnp.zeros((d, LANES - N_GROUPS - N_EXPERTS), F32)], axis=1)
    w['w_r'] = jnp.concatenate(_split_bf16(w_r, 2), axis=1)
    w['b_r'] = jnp.concatenate([moe_b_group[lyr], moe_b_expert[lyr],
                                jnp.zeros((LANES - N_GROUPS - N_EXPERTS,), F32)])[None]

    pos_p = jnp.arange(lp, dtype=jnp.int32)
    pos_s = n_past + jnp.arange(ls, dtype=jnp.int32)
    s0_p = jnp.zeros((bp, GLA_HEADS, GLA_DK, GLA_DV), F32)
    (x1_p, h2_p, ids_p, wts_p), ckv_p, kr_p, st_p = _mixer(x_prompt, mods_p, pos_p, s0_p, None, w, prompt=True)
    (x1_s, h2_s, ids_s, wts_s), ckv_s, kr_s, st_s = _mixer(x_sample, mods_s, pos_s, state_gla[lyr],
                                                          (cache_ckv, cache_krope, page_table), w, prompt=False)

    e_id = jnp.concatenate([ids_p[:, :TOP_K], ids_s[:, :TOP_K]], axis=0).reshape(-1)
    y_rows = _moe(_moe_plan(e_id, bp * lp), h2_p, h2_s, moe_w_gu[lyr], moe_w_down[lyr])

    y_p = _final(x1_p, mods_p[5], y_rows, wts_p, g_final[None], 1, 512, 0)
    y_s = _final(x1_s, mods_s[5], y_rows, wts_s, g_final[None], 512 // ls, ls, bp * lp // 512)
    return (y_p, y_s, ckv_p[None], kr_p[None], st_p[None], ckv_s[None], kr_s[None], st_s[None])
```
